```python
import math
import jax, jax.numpy as jnp
from jax import lax
import numpy as np

D_MODEL = 1024
BATCH = 4
SEQ = 4096
DEPTH = 4
DEC_BATCH = 4
DEC_SEQ = 8192
PAST_LEN = 128

N_MIXERS = 2
N_MLA_LAYERS = (DEPTH + 1) // 2
N_RET_LAYERS = DEPTH // 2
N_NORMS = 4
D_FF = 2816
MLA_HEADS = 8
QK_NOPE_DIM = 128
QK_ROPE_DIM = 64
QK_HEAD_DIM = QK_NOPE_DIM + QK_ROPE_DIM
V_HEAD_DIM = 128
Q_LORA_RANK = 384
KV_LORA_RANK = 256
ROPE_BASE = 10000.0
Q_BLOCK = 128
RET_HEADS = 4
RET_DK = D_MODEL // RET_HEADS
RET_DV = 2 * D_MODEL // RET_HEADS
RET_CHUNK = 128
EPS = 1e-6

kernel_name = "hybrid_mla_retention_macaron_encoder"


def rmsnorm(x, g):
    xf = x.astype(jnp.float32)
    y = xf * lax.rsqrt(jnp.mean(xf * xf, axis=-1, keepdims=True) + EPS)
    return (y * g.astype(jnp.float32)).astype(x.dtype)


def rope_tables(length, dim):
    inv = ROPE_BASE ** (-jnp.arange(0, dim, 2, dtype=jnp.float32) / dim)
    ang = jnp.arange(length, dtype=jnp.float32)[:, None] * inv[None, :]
    return jnp.cos(ang), jnp.sin(ang)


def apply_rope(x, cos, sin):
    half = x.shape[-1] // 2
    xf = x.astype(jnp.float32)
    x1, x2 = xf[..., :half], xf[..., half:]
    c, s = cos[None, :, None, :], sin[None, :, None, :]
    return jnp.concatenate([x1 * c - x2 * s, x2 * c + x1 * s], axis=-1).astype(x.dtype)


def swiglu(x, w_gate, w_up, w_down):
    return (jax.nn.silu(x @ w_gate) * (x @ w_up)) @ w_down


def dense_bidir_attention(q, k, v):
    b, length, h, dh = q.shape
    nq = length // Q_BLOCK
    scale = dh ** -0.5
    kf = k.astype(jnp.float32)
    vf = v.astype(jnp.float32)
    qb = q.reshape(b, nq, Q_BLOCK, h, dh).transpose(1, 0, 2, 3, 4)

    def one_block(qi):
        s = jnp.einsum('bqhd,bkhd->bhqk', qi.astype(jnp.float32), kf) * scale
        p = jax.nn.softmax(s, axis=-1)
        return jnp.einsum('bhqk,bkhd->bqhd', p, vf)

    out = lax.map(one_block, qb)
    return out.transpose(1, 0, 2, 3, 4).reshape(b, length, h, -1).astype(v.dtype)


def mla_mixer(x, w_dq, q_a_norm, w_uq, w_dkv, kv_a_norm, w_ukv, q_norm, k_norm, w_o):
    b, length, _ = x.shape
    cq = rmsnorm(x @ w_dq, q_a_norm)
    q = (cq @ w_uq).reshape(b, length, MLA_HEADS, QK_HEAD_DIM)
    ckv_full = x @ w_dkv
    ckv = rmsnorm(ckv_full[..., :KV_LORA_RANK], kv_a_norm)
    k_pe = ckv_full[..., KV_LORA_RANK:]
    kv = (ckv @ w_ukv).reshape(b, length, MLA_HEADS, QK_NOPE_DIM + V_HEAD_DIM)
    k_nope, v = kv[..., :QK_NOPE_DIM], kv[..., QK_NOPE_DIM:]
    k = jnp.concatenate([k_nope, jnp.broadcast_to(k_pe[:, :, None, :], (b, length, MLA_HEADS, QK_ROPE_DIM))], axis=-1)
    q = rmsnorm(q, q_norm)
    k = rmsnorm(k, k_norm)
    cos, sin = rope_tables(length, QK_ROPE_DIM)
    q = jnp.concatenate([q[..., :QK_NOPE_DIM], apply_rope(q[..., QK_NOPE_DIM:], cos, sin)], axis=-1)
    k = jnp.concatenate([k[..., :QK_NOPE_DIM], apply_rope(k[..., QK_NOPE_DIM:], cos, sin)], axis=-1)
    o = dense_bidir_attention(q, k, v)
    return o.reshape(b, length, MLA_HEADS * V_HEAD_DIM) @ w_o


def retention_scan(q, k, v, log_gamma, strict):
    b, length, h, _ = q.shape
    nc = length // RET_CHUNK
    idx = jnp.arange(RET_CHUNK, dtype=jnp.float32)
    diff = idx[:, None] - idx[None, :]
    mask = (diff > 0) if strict else (diff >= 0)
    decay_in = jnp.where(mask[None], jnp.exp(log_gamma[:, None, None] * jnp.maximum(diff, 0.0)[None]), 0.0)
    xi = jnp.exp(log_gamma[None, :] * (idx[:, None] + 1.0))
    zeta = jnp.exp(log_gamma[None, :] * (RET_CHUNK - 1.0 - idx[:, None]))
    g_chunk = jnp.exp(log_gamma * RET_CHUNK)

    def chunks(t):
        return t.reshape(b, nc, RET_CHUNK, h, -1).transpose(1, 0, 2, 3, 4)

    def step(state, qkv):
        qc, kc, vc = qkv
        s = jnp.einsum('bihd,bjhd->bhij', qc, kc) * decay_in[None]
        inner = jnp.einsum('bhij,bjhe->bihe', s, vc)
        cross = jnp.einsum('bihd,bhde->bihe', qc, state) * xi[None, :, :, None]
        state = state * g_chunk[None, :, None, None] + jnp.einsum('bjhd,bjhe->bhde', kc * zeta[None, :, :, None], vc)
        return state, inner + cross

    state0 = jnp.zeros((b, h, q.shape[-1], v.shape[-1]), jnp.float32)
    _, out = lax.scan(step, state0, (chunks(q), chunks(k), chunks(v)))
    return out.transpose(1, 0, 2, 3, 4).reshape(b, length, h, -1)


def retention_mixer(x, w_in, decay_fwd, decay_bwd, gn_g, w_o):
    b, length, _ = x.shape
    hq = RET_HEADS * RET_DK
    hv = RET_HEADS * RET_DV
    proj = x @ w_in
    q = proj[..., :hq].reshape(b, length, RET_HEADS, RET_DK)
    k = proj[..., hq:2 * hq].reshape(b, length, RET_HEADS, RET_DK)
    v = proj[..., 2 * hq:2 * hq + hv].reshape(b, length, RET_HEADS, RET_DV)
    g = proj[..., 2 * hq + hv:]
    cos, sin = rope_tables(length, RET_DK)
    q = apply_rope(q, cos, sin).astype(jnp.float32)
    k = apply_rope(k, cos, sin).astype(jnp.float32) * (RET_DK ** -0.5)
    vf = v.astype(jnp.float32)
    lg_f = jnp.log1p(-jnp.exp(decay_fwd.astype(jnp.float32)))
    lg_b = jnp.log1p(-jnp.exp(decay_bwd.astype(jnp.float32)))
    o_f = retention_scan(q, k, vf, lg_f, strict=False)
    o_b = jnp.flip(retention_scan(jnp.flip(q, 1), jnp.flip(k, 1), jnp.flip(vf, 1), lg_b, strict=True), 1)
    o = o_f + o_b
    mu = jnp.mean(o, axis=-1, keepdims=True)
    var = jnp.mean(jnp.square(o - mu), axis=-1, keepdims=True)
    o = ((o - mu) * lax.rsqrt(var + EPS)).reshape(b, length, hv) * gn_g.astype(jnp.float32)
    return (jax.nn.silu(g) * o.astype(x.dtype)) @ w_o


def trunk(x, norm_g, ffn_w_gate, ffn_w_up, ffn_w_down,
          mla_w_dq, mla_q_a_norm, mla_w_uq, mla_w_dkv, mla_kv_a_norm, mla_w_ukv,
          mla_q_norm, mla_k_norm, mla_w_o,
          ret_w_in, ret_decay_fwd, ret_decay_bwd, ret_gn_g, ret_w_o):
    for i in range(DEPTH):
        j = i // N_MIXERS
        h = x + 0.5 * swiglu(rmsnorm(x, norm_g[i, 0]), ffn_w_gate[i, 0], ffn_w_up[i, 0], ffn_w_down[i, 0])
        hn = rmsnorm(h, norm_g[i, 1])
        if i % N_MIXERS == 0:
            h = h + mla_mixer(hn, mla_w_dq[j], mla_q_a_norm[j], mla_w_uq[j], mla_w_dkv[j], mla_kv_a_norm[j],
                              mla_w_ukv[j], mla_q_norm[j], mla_k_norm[j], mla_w_o[j])
        else:
            h = h + retention_mixer(hn, ret_w_in[j], ret_decay_fwd[j], ret_decay_bwd[j], ret_gn_g[j], ret_w_o[j])
        h = h + 0.5 * swiglu(rmsnorm(h, norm_g[i, 2]), ffn_w_gate[i, 1], ffn_w_up[i, 1], ffn_w_down[i, 1])
        x = rmsnorm(h, norm_g[i, 3])
    return x


def setup_inputs(seed: int = 0) -> dict:
    key = jax.random.key(seed)
    ks = iter(jax.random.split(key, 32))

    def w(shape, fan_in):
        return jax.random.normal(next(ks), shape, jnp.float32) * (fan_in ** -0.5)

    def gain(shape):
        return 1.0 + 0.02 * jax.random.normal(next(ks), shape, jnp.float32)

    base_decay = -(5.0 + jnp.arange(RET_HEADS, dtype=jnp.float32)) * math.log(2.0)
    return {
        "x_prompt": jax.random.normal(next(ks), (BATCH, SEQ, D_MODEL), jnp.float32),
        "x_sample": jax.random.normal(next(ks), (DEC_BATCH, DEC_SEQ, D_MODEL), jnp.float32),
        "norm_g": gain((DEPTH, N_NORMS, D_MODEL)),
        "ffn_w_gate": w((DEPTH, 2, D_MODEL, D_FF), D_MODEL),
        "ffn_w_up": w((DEPTH, 2, D_MODEL, D_FF), D_MODEL),
        "ffn_w_down": w((DEPTH, 2, D_FF, D_MODEL), D_FF),
        "mla_w_dq": w((N_MLA_LAYERS, D_MODEL, Q_LORA_RANK), D_MODEL),
        "mla_q_a_norm": gain((N_MLA_LAYERS, Q_LORA_RANK)),
        "mla_w_uq": w((N_MLA_LAYERS, Q_LORA_RANK, MLA_HEADS * QK_HEAD_DIM), Q_LORA_RANK),
        "mla_w_dkv": w((N_MLA_LAYERS, D_MODEL, KV_LORA_RANK + QK_ROPE_DIM), D_MODEL),
        "mla_kv_a_norm": gain((N_MLA_LAYERS, KV_LORA_RANK)),
        "mla_w_ukv": w((N_MLA_LAYERS, KV_LORA_RANK, MLA_HEADS * (QK_NOPE_DIM + V_HEAD_DIM)), KV_LORA_RANK),
        "mla_q_norm": gain((N_MLA_LAYERS, QK_HEAD_DIM)),
        "mla_k_norm": gain((N_MLA_LAYERS, QK_HEAD_DIM)),
        "mla_w_o": w((N_MLA_LAYERS, MLA_HEADS * V_HEAD_DIM, D_MODEL), MLA_HEADS * V_HEAD_DIM),
        "ret_w_in": w((N_RET_LAYERS, D_MODEL, RET_HEADS * (2 * RET_DK + 2 * RET_DV)), D_MODEL),
        "ret_decay_fwd": base_decay[None, :] + 0.05 * jax.random.normal(next(ks), (N_RET_LAYERS, RET_HEADS), jnp.float32),
        "ret_decay_bwd": base_decay[None, :] + 0.05 * jax.random.normal(next(ks), (N_RET_LAYERS, RET_HEADS), jnp.float32),
        "ret_gn_g": gain((N_RET_LAYERS, RET_HEADS * RET_DV)),
        "ret_w_o": w((N_RET_LAYERS, RET_HEADS * RET_DV, D_MODEL), RET_HEADS * RET_DV),
    }


def reference(x_prompt, x_sample, norm_g, ffn_w_gate, ffn_w_up, ffn_w_down,
              mla_w_dq, mla_q_a_norm, mla_w_uq, mla_w_dkv, mla_kv_a_norm, mla_w_ukv,
              mla_q_norm, mla_k_norm, mla_w_o,
              ret_w_in, ret_decay_fwd, ret_decay_bwd, ret_gn_g, ret_w_o):
    y_prompt = trunk(x_prompt, norm_g, ffn_w_gate, ffn_w_up, ffn_w_down,
                     mla_w_dq, mla_q_a_norm, mla_w_uq, mla_w_dkv, mla_kv_a_norm, mla_w_ukv,
                     mla_q_norm, mla_k_norm, mla_w_o,
                     ret_w_in, ret_decay_fwd, ret_decay_bwd, ret_gn_g, ret_w_o)
    y_sample = trunk(x_sample, norm_g, ffn_w_gate, ffn_w_up, ffn_w_down,
                     mla_w_dq, mla_q_a_norm, mla_w_uq, mla_w_dkv, mla_kv_a_norm, mla_w_ukv,
                     mla_q_norm, mla_k_norm, mla_w_o,
                     ret_w_in, ret_decay_fwd, ret_decay_bwd, ret_gn_g, ret_w_o)
    return (y_prompt, y_sample)
```

```python
import functools
import math

import jax
import jax.numpy as jnp
from jax import lax
from jax.experimental import pallas as pl
from jax.experimental.pallas import tpu as pltpu

F32 = jnp.float32
BF16 = jnp.bfloat16

D_MODEL = 1024
DEPTH = 4
D_FF = 2816
MLA_HEADS = 8
QK_NOPE_DIM = 128
QK_ROPE_DIM = 64
QK_HEAD_DIM = QK_NOPE_DIM + QK_ROPE_DIM
V_HEAD_DIM = 128
Q_LORA_RANK = 384
KV_LORA_RANK = 256
ROPE_BASE = 10000.0
RET_HEADS = 4
RET_DK = D_MODEL // RET_HEADS
RET_DV = 2 * D_MODEL // RET_HEADS
EPS = 1e-6

LANES_V7X = 128
MXU_DIM_V7X = 256
VMEM_LIMIT_BYTES_V7X = 56 * 1024 * 1024

QK_SLOT = QK_NOPE_DIM + 2 * QK_ROPE_DIM
assert QK_SLOT == MXU_DIM_V7X

ROW_TILE = 512
ATTN_TQ = 512
ATTN_TK = 512
RET_CHUNK = MXU_DIM_V7X
RET_BLOCK = 1024

_NT = (((1,), (1,)), ((), ()))


def _params(*sem):
    return pltpu.CompilerParams(dimension_semantics=sem, vmem_limit_bytes=VMEM_LIMIT_BYTES_V7X)


def _resident(shape):
    return pl.BlockSpec(shape, lambda *_: (0,) * len(shape), pipeline_mode=pl.Buffered(1))


def _rms(x, g):
    return x * lax.rsqrt(jnp.mean(x * x, axis=-1, keepdims=True) + EPS) * g


def _mm(a, b):
    return jnp.dot(a, b, preferred_element_type=F32)


def _ffn_body(x_ref, g_ref, wg_ref, wu_ref, wd_ref, gp_ref, o_ref, *, post_norm):
    x = x_ref[...]
    xn = _rms(x, g_ref[...]).astype(BF16)
    gate = _mm(xn, wg_ref[...])
    up = _mm(xn, wu_ref[...])
    h = (gate * jax.nn.sigmoid(gate) * up).astype(BF16)
    y = x + 0.5 * _mm(h, wd_ref[...])
    if post_norm:
        y = _rms(y, gp_ref[...])
    o_ref[...] = y


def _ffn(x, g, wg, wu, wd, g_post, post_norm):
    n, d = x.shape
    row = pl.BlockSpec((ROW_TILE, d), lambda i: (i, 0))
    return pl.pallas_call(
        functools.partial(_ffn_body, post_norm=post_norm),
        grid=(n // ROW_TILE,),
        in_specs=[row, _resident((1, d)), _resident(wg.shape), _resident(wu.shape),
                  _resident(wd.shape), _resident((1, d))],
        out_specs=row,
        out_shape=jax.ShapeDtypeStruct((n, d), F32),
        compiler_params=_params("parallel"),
        name="ffn",
    )(x, g, wg, wu, wd, g_post)


def _proj_res_body(h_ref, a_ref, w_ref, o_ref):
    o_ref[...] = h_ref[...] + _mm(a_ref[...], w_ref[...])


def _proj_res(h, a, w):
    n, d = h.shape
    return pl.pallas_call(
        _proj_res_body,
        grid=(n // ROW_TILE,),
        in_specs=[pl.BlockSpec((ROW_TILE, d), lambda i: (i, 0)),
                  pl.BlockSpec((ROW_TILE, a.shape[1]), lambda i: (i, 0)),
                  _resident(w.shape)],
        out_specs=pl.BlockSpec((ROW_TILE, d), lambda i: (i, 0)),
        out_shape=jax.ShapeDtypeStruct((n, d), F32),
        compiler_params=_params("parallel"),
        name="proj_res",
    )(h, a, w)


def _rope_slot(t, gain, cs):
    a = t * gain * cs
    lane = lax.broadcasted_iota(jnp.int32, a.shape, 1)
    return jnp.where(lane < QK_ROPE_DIM, a + pltpu.roll(a, QK_ROPE_DIM, axis=1), 0.0)


def _sumsq_rope(t):
    lane = lax.broadcasted_iota(jnp.int32, t.shape, 1)
    return jnp.sum(jnp.where(lane < QK_ROPE_DIM, t * t, 0.0), axis=-1, keepdims=True)


def _mla_proj_body(h_ref, g_ref, wdq_ref, qa_ref, wuq_ref, wdkv_ref, kva_ref, wk_ref, wvt_ref,
                   qg_ref, kg_ref, cs_ref, q_ref, k_ref, vt_ref):
    hn = _rms(h_ref[...], g_ref[...]).astype(BF16)
    cs = cs_ref[...]
    qg = qg_ref[...]
    kg = kg_ref[...]
    inv_dim = 1.0 / QK_HEAD_DIM

    ckv_full = _mm(hn, wdkv_ref[...])
    ckv = _rms(ckv_full[:, :KV_LORA_RANK], kva_ref[...]).astype(BF16)
    vt_ref[...] = lax.dot_general(wvt_ref[...], ckv, _NT, preferred_element_type=F32).astype(BF16)
    k_pe_raw = ckv_full[:, KV_LORA_RANK:]
    k_pe_ss = _sumsq_rope(k_pe_raw)
    k_pe = _rope_slot(k_pe_raw, kg[:, QK_NOPE_DIM:], cs)
    k_nope = _mm(ckv, wk_ref[...])
    for hd in range(MLA_HEADS):
        kn = k_nope[:, hd * QK_NOPE_DIM:(hd + 1) * QK_NOPE_DIM]
        r = lax.rsqrt((jnp.sum(kn * kn, axis=-1, keepdims=True) + k_pe_ss) * inv_dim + EPS)
        k_ref[:, hd * QK_SLOT:hd * QK_SLOT + QK_NOPE_DIM] = (kn * r * kg[:, :QK_NOPE_DIM]).astype(BF16)
        k_ref[:, hd * QK_SLOT + QK_NOPE_DIM:(hd + 1) * QK_SLOT] = (k_pe * r).astype(BF16)

    cq = _rms(_mm(hn, wdq_ref[...]), qa_ref[...]).astype(BF16)
    scale = QK_HEAD_DIM ** -0.5
    for hd in range(MLA_HEADS):
        qh = _mm(cq, wuq_ref[:, hd * QK_SLOT:(hd + 1) * QK_SLOT])
        qn, qp = qh[:, :QK_NOPE_DIM], qh[:, QK_NOPE_DIM:]
        ss = jnp.sum(qn * qn, axis=-1, keepdims=True) + _sumsq_rope(qp)
        r = lax.rsqrt(ss * inv_dim + EPS) * scale
        q_ref[:, hd * QK_SLOT:hd * QK_SLOT + QK_NOPE_DIM] = (qn * r * qg[:, :QK_NOPE_DIM]).astype(BF16)
        q_ref[:, hd * QK_SLOT + QK_NOPE_DIM:(hd + 1) * QK_SLOT] = (
            _rope_slot(qp, qg[:, QK_NOPE_DIM:], cs) * r).astype(BF16)


def _mla_proj(h, g, w, cs, batch, length):
    n, d = h.shape
    nblk = length // ROW_TILE
    slots = MLA_HEADS * QK_SLOT
    dv = MLA_HEADS * V_HEAD_DIM
    row = lambda width: pl.BlockSpec((ROW_TILE, width), lambda i: (i, 0))
    return pl.pallas_call(
        _mla_proj_body,
        grid=(n // ROW_TILE,),
        in_specs=[row(d), _resident((1, d)), _resident(w["dq"].shape), _resident((1, Q_LORA_RANK)),
                  _resident(w["uq"].shape), _resident(w["dkv"].shape), _resident((1, KV_LORA_RANK)),
                  _resident(w["k"].shape), _resident(w["vt"].shape),
                  _resident((1, QK_SLOT)), _resident((1, QK_SLOT)),
                  pl.BlockSpec((ROW_TILE, LANES_V7X), lambda i: (i % nblk, 0))],
        out_specs=[row(slots), row(slots),
                   pl.BlockSpec((None, dv, ROW_TILE), lambda i: (i // nblk, 0, i % nblk))],
        out_shape=[jax.ShapeDtypeStruct((n, slots), BF16), jax.ShapeDtypeStruct((n, slots), BF16),
                   jax.ShapeDtypeStruct((batch, dv, length), BF16)],
        compiler_params=_params("parallel"),
        name="mla_proj",
    )(h, g, w["dq"], w["qa"], w["uq"], w["dkv"], w["kva"], w["k"], w["vt"], w["qg"], w["kg"], cs)


def _attn_body(q_ref, k_ref, vt_ref, o_ref, m_ref, l_ref, acc_ref):
    q = q_ref[...]
    m_ref[...] = jnp.full(m_ref.shape, -jnp.inf, F32)
    l_ref[...] = jnp.zeros(l_ref.shape, F32)
    acc_ref[...] = jnp.zeros(acc_ref.shape, F32)

    def step(j, carry):
        off = pl.multiple_of(j * ATTN_TK, ATTN_TK)
        st = lax.dot_general(k_ref[pl.ds(off, ATTN_TK), :], q, _NT, preferred_element_type=F32)
        m_old = m_ref[...]
        m_new = jnp.maximum(m_old, jnp.max(st, axis=0, keepdims=True))
        p = jnp.exp(st - m_new)
        alpha = jnp.exp(m_old - m_new)
        l_ref[...] = alpha * l_ref[...] + jnp.sum(p, axis=0, keepdims=True)
        acc_ref[...] = alpha * acc_ref[...] + _mm(vt_ref[:, pl.ds(off, ATTN_TK)], p.astype(BF16))
        m_ref[...] = m_new
        return carry

    lax.fori_loop(0, k_ref.shape[0] // ATTN_TK, step, 0)
    o_ref[...] = (acc_ref[...] / l_ref[...]).T.astype(BF16)


def _attention(q, k, vt, batch, length):
    q = q.reshape(batch, length, MLA_HEADS * QK_SLOT)
    k = k.reshape(batch, length, MLA_HEADS * QK_SLOT)
    out = pl.pallas_call(
        _attn_body,
        grid=(batch, MLA_HEADS, length // ATTN_TQ),
        in_specs=[pl.BlockSpec((None, ATTN_TQ, QK_SLOT), lambda b, h, i: (b, i, h)),
                  pl.BlockSpec((None, length, QK_SLOT), lambda b, h, i: (b, 0, h)),
                  pl.BlockSpec((None, V_HEAD_DIM, length), lambda b, h, i: (b, h, 0))],
        out_specs=pl.BlockSpec((None, ATTN_TQ, V_HEAD_DIM), lambda b, h, i: (b, i, h)),
        out_shape=jax.ShapeDtypeStruct((batch, length, MLA_HEADS * V_HEAD_DIM), BF16),
        scratch_shapes=[pltpu.VMEM((1, ATTN_TQ), F32), pltpu.VMEM((1, ATTN_TQ), F32),
                        pltpu.VMEM((V_HEAD_DIM, ATTN_TQ), F32)],
        compiler_params=_params("parallel", "parallel", "parallel"),
        name="mla_attn",
    )(q, k, vt)
    return out.reshape(batch * length, MLA_HEADS * V_HEAD_DIM)


def _ret_proj_body(h_ref, g_ref, w_ref, cos_ref, sin_ref, q_ref, k_ref, v_ref, gate_ref):
    hn = _rms(h_ref[...], g_ref[...]).astype(BF16)
    cos = cos_ref[...]
    sin = sin_ref[...]
    hq = RET_HEADS * RET_DK
    hv = RET_HEADS * RET_DV
    half = RET_DK // 2

    def rope_store(dst, col0, mult):
        for hd in range(RET_HEADS):
            x = _mm(hn, w_ref[:, col0 + hd * RET_DK:col0 + (hd + 1) * RET_DK])
            x1, x2 = x[:, :half], x[:, half:]
            dst[:, hd * RET_DK:hd * RET_DK + half] = ((x1 * cos - x2 * sin) * mult).astype(BF16)
            dst[:, hd * RET_DK + half:(hd + 1) * RET_DK] = ((x2 * cos + x1 * sin) * mult).astype(BF16)

    rope_store(q_ref, 0, 1.0)
    rope_store(k_ref, hq, RET_DK ** -0.5)
    v_ref[...] = _mm(hn, w_ref[:, 2 * hq:2 * hq + hv]).astype(BF16)
    gate_ref[...] = _mm(hn, w_ref[:, 2 * hq + hv:]).astype(BF16)


def _ret_proj(h, g, w_in, cos, sin, length):
    n, d = h.shape
    nblk = length // ROW_TILE
    hq = RET_HEADS * RET_DK
    hv = RET_HEADS * RET_DV
    row = lambda width: pl.BlockSpec((ROW_TILE, width), lambda i: (i, 0))
    pos = pl.BlockSpec((ROW_TILE, RET_DK // 2), lambda i: (i % nblk, 0))
    return pl.pallas_call(
        _ret_proj_body,
        grid=(n // ROW_TILE,),
        in_specs=[row(d), _resident((1, d)), _resident(w_in.shape), pos, pos],
        out_specs=[row(hq), row(hq), row(hv), row(hv)],
        out_shape=[jax.ShapeDtypeStruct((n, hq), BF16), jax.ShapeDtypeStruct((n, hq), BF16),
                   jax.ShapeDtypeStruct((n, hv), BF16), jax.ShapeDtypeStruct((n, hv), BF16)],
        compiler_params=_params("parallel"),
        name="ret_proj",
    )(h, g, w_in, cos, sin)


def _ret_scan_body(lg_ref, q_ref, k_ref, v_ref, gate_ref, gn_ref, o_ref,
                   state_ref, ob_ref, dmat_ref, xi_ref, zeta_ref, gc_ref):
    hd = pl.program_id(1)
    direction = pl.program_id(2)
    blk = pl.program_id(3)
    nblk = pl.num_programs(3)
    c = RET_CHUNK
    n_chunks = RET_BLOCK // c

    @pl.when(blk == 0)
    def _init():
        lg = lg_ref[direction, hd]
        sign = 2 * direction - 1
        state_ref[...] = jnp.zeros(state_ref.shape, F32)
        row = lax.broadcasted_iota(jnp.int32, (c, c), 0)
        col = lax.broadcasted_iota(jnp.int32, (c, c), 1)
        diff = (row - col) * sign
        keep = diff >= 1 - direction
        dmat_ref[...] = jnp.where(keep, jnp.exp(lg * jnp.maximum(diff, 0).astype(F32)), 0.0)
        qi = lax.broadcasted_iota(jnp.int32, xi_ref.shape, 0)
        xi_ref[...] = jnp.exp(lg * (direction * (qi + 1) + (1 - direction) * (c - qi)).astype(F32))
        kj = lax.broadcasted_iota(jnp.int32, zeta_ref.shape, 0)
        zeta_ref[...] = jnp.exp(lg * (direction * (c - 1 - kj) + (1 - direction) * kj).astype(F32))
        gc_ref[...] = jnp.exp(jnp.zeros(gc_ref.shape, F32) + lg * c)

    def chunk(ci):
        rows = pl.ds(ci * c, c)
        q = q_ref[rows, :]
        k = k_ref[rows, :]
        v = v_ref[rows, :]
        s = lax.dot_general(q, k, _NT, preferred_element_type=F32) * dmat_ref[...]
        o = _mm(s.astype(BF16), v) + _mm(q, state_ref[...].astype(BF16)) * xi_ref[...]
        kz_t = (k.astype(F32) * zeta_ref[...]).T.astype(BF16)
        state_ref[...] = state_ref[...] * gc_ref[...] + _mm(kz_t, v)
        return o

    @pl.when(direction == 0)
    def _backward():
        base = (nblk - 1 - blk) * RET_BLOCK
        for ci in reversed(range(n_chunks)):
            ob_ref[pl.ds(pl.multiple_of(base + ci * c, c), c), :] = chunk(ci)

    @pl.when(direction == 1)
    def _forward():
        base = blk * RET_BLOCK
        for ci in range(n_chunks):
            o = chunk(ci) + ob_ref[pl.ds(pl.multiple_of(base + ci * c, c), c), :]
            mu = jnp.mean(o, axis=-1, keepdims=True)
            dev = o - mu
            var = jnp.mean(dev * dev, axis=-1, keepdims=True)
            on = dev * lax.rsqrt(var + EPS) * gn_ref[...]
            gate = gate_ref[pl.ds(ci * c, c), :].astype(F32)
            o_ref[pl.ds(ci * c, c), :] = (gate * jax.nn.sigmoid(gate) * on).astype(BF16)


def _ret_scan(q, k, v, gate, lg, gn_g, batch, length):
    hq = RET_HEADS * RET_DK
    hv = RET_HEADS * RET_DV
    nblk = length // RET_BLOCK
    q = q.reshape(batch, length, hq)
    k = k.reshape(batch, length, hq)
    v = v.reshape(batch, length, hv)
    gate = gate.reshape(batch, length, hv)

    def sweep(b, h, d, i):
        return (b, d * i + (1 - d) * (nblk - 1 - i), h)

    def fwd_only(b, h, d, i):
        return (b, d * i, h)

    out = pl.pallas_call(
        _ret_scan_body,
        grid=(batch, RET_HEADS, 2, nblk),
        in_specs=[pl.BlockSpec(memory_space=pltpu.SMEM),
                  pl.BlockSpec((None, RET_BLOCK, RET_DK), sweep),
                  pl.BlockSpec((None, RET_BLOCK, RET_DK), sweep),
                  pl.BlockSpec((None, RET_BLOCK, RET_DV), sweep),
                  pl.BlockSpec((None, RET_BLOCK, RET_DV), fwd_only),
                  pl.BlockSpec((1, RET_DV), lambda b, h, d, i: (0, h))],
        out_specs=pl.BlockSpec((None, RET_BLOCK, RET_DV), fwd_only),
        out_shape=jax.ShapeDtypeStruct((batch, length, hv), BF16),
        scratch_shapes=[pltpu.VMEM((RET_DK, RET_DV), F32),
                        pltpu.VMEM((length, RET_DV), F32),
                        pltpu.VMEM((RET_CHUNK, RET_CHUNK), F32),
                        pltpu.VMEM((RET_CHUNK, RET_DV), F32),
                        pltpu.VMEM((RET_CHUNK, RET_DK), F32),
                        pltpu.VMEM((RET_DK, RET_DV), F32)],
        compiler_params=_params("parallel", "parallel", "arbitrary", "arbitrary"),
        name="ret_scan",
    )(lg, q, k, v, gate, gn_g)
    return out.reshape(batch * length, hv)


def _swap_halves(a):
    half = a.shape[-1] // 2
    return jnp.concatenate([a[..., half:], a[..., :half]], axis=-1)


def _rope_slot_layout(a):
    pe = a[..., QK_NOPE_DIM:]
    return jnp.concatenate([a[..., :QK_NOPE_DIM], pe, _swap_halves(pe)], axis=-1)


def _mla_weights(w_dq, q_a_norm, w_uq, w_dkv, kv_a_norm, w_ukv, q_norm, k_norm):
    uq = w_uq.reshape(Q_LORA_RANK, MLA_HEADS, QK_HEAD_DIM)
    uq = _rope_slot_layout(uq).reshape(Q_LORA_RANK, MLA_HEADS * QK_SLOT)
    pe = w_dkv[:, KV_LORA_RANK:]
    dkv = jnp.concatenate([w_dkv[:, :KV_LORA_RANK], pe, _swap_halves(pe)], axis=-1)
    ukv = w_ukv.reshape(KV_LORA_RANK, MLA_HEADS, QK_NOPE_DIM + V_HEAD_DIM)
    wk = ukv[..., :QK_NOPE_DIM].reshape(KV_LORA_RANK, MLA_HEADS * QK_NOPE_DIM)
    wvt = ukv[..., QK_NOPE_DIM:].reshape(KV_LORA_RANK, MLA_HEADS * V_HEAD_DIM).T
    return {
        "dq": w_dq.astype(BF16), "qa": q_a_norm[None, :], "uq": uq.astype(BF16),
        "dkv": dkv.astype(BF16), "kva": kv_a_norm[None, :],
        "k": wk.astype(BF16), "vt": wvt.astype(BF16),
        "qg": _rope_slot_layout(q_norm)[None, :], "kg": _rope_slot_layout(k_norm)[None, :],
    }


def _rope_tables(length, dim):
    inv = ROPE_BASE ** (-jnp.arange(0, dim, 2, dtype=F32) / dim)
    ang = jnp.arange(length, dtype=F32)[:, None] * inv[None, :]
    return jnp.cos(ang), jnp.sin(ang)


def _trunk(x, p):
    batch, length, d = x.shape
    x = x.reshape(batch * length, d)
    cos_m, sin_m = _rope_tables(length, QK_ROPE_DIM)
    cs_mla = jnp.concatenate([cos_m, cos_m, -sin_m, sin_m], axis=-1)
    cos_r, sin_r = _rope_tables(length, RET_DK)
    for i in range(DEPTH):
        j = i // 2
        ng = p["norm_g"][i]
        f = p["ffn"][i]
        h = _ffn(x, ng[0:1], f[0]["g"], f[0]["u"], f[0]["d"], ng[0:1], False)
        if i % 2 == 0:
            w = p["mla"][j]
            q, k, vt = _mla_proj(h, ng[1:2], w, cs_mla, batch, length)
            o = _attention(q, k, vt, batch, length)
            h = _proj_res(h, o, w["o"])
        else:
            w = p["ret"][j]
            q, k, v, gate = _ret_proj(h, ng[1:2], w["in"], cos_r, sin_r, length)
            o = _ret_scan(q, k, v, gate, w["lg"], w["gn"], batch, length)
            h = _proj_res(h, o, w["o"])
        x = _ffn(h, ng[2:3], f[1]["g"], f[1]["u"], f[1]["d"], ng[3:4], True)
    return x.reshape(batch, length, d)


def kernel(x_prompt, x_sample, norm_g, ffn_w_gate, ffn_w_up, ffn_w_down, mla_w_dq, mla_q_a_norm, mla_w_uq, mla_w_dkv, mla_kv_a_norm, mla_w_ukv, mla_q_norm, mla_k_norm, mla_w_o, ret_w_in, ret_decay_fwd, ret_decay_bwd, ret_gn_g, ret_w_o):
    p = {
        "norm_g": norm_g,
        "ffn": [[{"g": ffn_w_gate[i, s].astype(BF16), "u": ffn_w_up[i, s].astype(BF16),
                  "d": ffn_w_down[i, s].astype(BF16)} for s in range(2)] for i in range(DEPTH)],
        "mla": [], "ret": [],
    }
    for j in range(mla_w_dq.shape[0]):
        w = _mla_weights(mla_w_dq[j], mla_q_a_norm[j], mla_w_uq[j], mla_w_dkv[j], mla_kv_a_norm[j],
                         mla_w_ukv[j], mla_q_norm[j], mla_k_norm[j])
        w["o"] = mla_w_o[j].astype(BF16)
        p["mla"].append(w)
    for j in range(ret_w_in.shape[0]):
        lg = jnp.stack([jnp.log1p(-jnp.exp(ret_decay_bwd[j].astype(F32))),
                        jnp.log1p(-jnp.exp(ret_decay_fwd[j].astype(F32)))])
        p["ret"].append({"in": ret_w_in[j].astype(BF16), "lg": lg, "gn": ret_gn_g[j][None, :],
                         "o": ret_w_o[j].astype(BF16)})
    return _trunk(x_prompt, p), _trunk(x_sample, p)
```

```python
import functools
import math

import jax
import jax.numpy as jnp
from jax import lax
from jax.experimental import pallas as pl
from jax.experimental.pallas import tpu as pltpu

F32 = jnp.float32
BF16 = jnp.bfloat16

D_MODEL = 1024
DEPTH = 4
D_FF = 2816
MLA_HEADS = 8
QK_NOPE_DIM = 128
QK_ROPE_DIM = 64
QK_HEAD_DIM = QK_NOPE_DIM + QK_ROPE_DIM
V_HEAD_DIM = 128
Q_LORA_RANK = 384
KV_LORA_RANK = 256
ROPE_BASE = 10000.0
RET_HEADS = 4
RET_DK = D_MODEL // RET_HEADS
RET_DV = 2 * D_MODEL // RET_HEADS
EPS = 1e-6

LANES_V7X = 128
MXU_DIM_V7X = 256
VMEM_LIMIT_BYTES_V7X = 56 * 1024 * 1024

QK_SLOT = QK_NOPE_DIM + 2 * QK_ROPE_DIM
assert QK_SLOT == MXU_DIM_V7X

ROW_TILE = 512
ATTN_TQ = 512
ATTN_TK = 512
RET_CHUNK = MXU_DIM_V7X
RET_BLOCK = 1024

_NT = (((1,), (1,)), ((), ()))


def _params(*sem):
    return pltpu.CompilerParams(dimension_semantics=sem, vmem_limit_bytes=VMEM_LIMIT_BYTES_V7X)


def _resident(shape):
    return pl.BlockSpec(shape, lambda *_: (0,) * len(shape), pipeline_mode=pl.Buffered(1))


def _rms(x, g):
    return x * lax.rsqrt(jnp.mean(x * x, axis=-1, keepdims=True) + EPS) * g


def _mm(a, b):
    return jnp.dot(a, b, preferred_element_type=F32)


def _ffn_body(x_ref, g_ref, wg_ref, wu_ref, wd_ref, gp_ref, o_ref, *, post_norm):
    x = x_ref[...]
    xn = _rms(x, g_ref[...]).astype(BF16)
    gate = _mm(xn, wg_ref[...])
    up = _mm(xn, wu_ref[...])
    h = (gate * jax.nn.sigmoid(gate) * up).astype(BF16)
    y = x + 0.5 * _mm(h, wd_ref[...])
    if post_norm:
        y = _rms(y, gp_ref[...])
    o_ref[...] = y


def _ffn(x, g, wg, wu, wd, g_post, post_norm):
    n, d = x.shape
    row = pl.BlockSpec((ROW_TILE, d), lambda i: (i, 0))
    return pl.pallas_call(
        functools.partial(_ffn_body, post_norm=post_norm),
        grid=(n // ROW_TILE,),
        in_specs=[row, _resident((1, d)), _resident(wg.shape), _resident(wu.shape),
                  _resident(wd.shape), _resident((1, d))],
        out_specs=row,
        out_shape=jax.ShapeDtypeStruct((n, d), F32),
        compiler_params=_params("parallel"),
        name="ffn",
    )(x, g, wg, wu, wd, g_post)


def _proj_res_body(h_ref, a_ref, w_ref, o_ref):
    o_ref[...] = h_ref[...] + _mm(a_ref[...], w_ref[...])


def _proj_res(h, a, w):
    n, d = h.shape
    return pl.pallas_call(
        _proj_res_body,
        grid=(n // ROW_TILE,),
        in_specs=[pl.BlockSpec((ROW_TILE, d), lambda i: (i, 0)),
                  pl.BlockSpec((ROW_TILE, a.shape[1]), lambda i: (i, 0)),
                  _resident(w.shape)],
        out_specs=pl.BlockSpec((ROW_TILE, d), lambda i: (i, 0)),
        out_shape=jax.ShapeDtypeStruct((n, d), F32),
        compiler_params=_params("parallel"),
        name="proj_res",
    )(h, a, w)


def _rope_slot(t, gain, cs):
    a = t * gain * cs
    lane = lax.broadcasted_iota(jnp.int32, a.shape, 1)
    return jnp.where(lane < QK_ROPE_DIM, a + pltpu.roll(a, QK_ROPE_DIM, axis=1), 0.0)


def _sumsq_rope(t):
    lane = lax.broadcasted_iota(jnp.int32, t.shape, 1)
    return jnp.sum(jnp.where(lane < QK_ROPE_DIM, t * t, 0.0), axis=-1, keepdims=True)


def _mla_proj_body(h_ref, g_ref, wdq_ref, qa_ref, wuq_ref, wdkv_ref, kva_ref, wk_ref, wvt_ref,
                   qg_ref, kg_ref, cs_ref, q_ref, k_ref, vt_ref):
    hn = _rms(h_ref[...], g_ref[...]).astype(BF16)
    cs = cs_ref[...]
    qg = qg_ref[...]
    kg = kg_ref[...]
    inv_dim = 1.0 / QK_HEAD_DIM

    ckv_full = _mm(hn, wdkv_ref[...])
    ckv = _rms(ckv_full[:, :KV_LORA_RANK], kva_ref[...]).astype(BF16)
    vt_ref[...] = lax.dot_general(wvt_ref[...], ckv, _NT, preferred_element_type=F32).astype(BF16)
    k_pe_raw = ckv_full[:, KV_LORA_RANK:]
    k_pe_ss = _sumsq_rope(k_pe_raw)
    k_pe = _rope_slot(k_pe_raw, kg[:, QK_NOPE_DIM:], cs)
    k_nope = _mm(ckv, wk_ref[...])
    for hd in range(MLA_HEADS):
        kn = k_nope[:, hd * QK_NOPE_DIM:(hd + 1) * QK_NOPE_DIM]
        r = lax.rsqrt((jnp.sum(kn * kn, axis=-1, keepdims=True) + k_pe_ss) * inv_dim + EPS)
        k_ref[:, hd * QK_SLOT:hd * QK_SLOT + QK_NOPE_DIM] = (kn * r * kg[:, :QK_NOPE_DIM]).astype(BF16)
        k_ref[:, hd * QK_SLOT + QK_NOPE_DIM:(hd + 1) * QK_SLOT] = (k_pe * r).astype(BF16)

    cq = _rms(_mm(hn, wdq_ref[...]), qa_ref[...]).astype(BF16)
    scale = QK_HEAD_DIM ** -0.5 * math.log2(math.e)
    for hd in range(MLA_HEADS):
        qh = _mm(cq, wuq_ref[:, hd * QK_SLOT:(hd + 1) * QK_SLOT])
        qn, qp = qh[:, :QK_NOPE_DIM], qh[:, QK_NOPE_DIM:]
        ss = jnp.sum(qn * qn, axis=-1, keepdims=True) + _sumsq_rope(qp)
        r = lax.rsqrt(ss * inv_dim + EPS) * scale
        q_ref[:, hd * QK_SLOT:hd * QK_SLOT + QK_NOPE_DIM] = (qn * r * qg[:, :QK_NOPE_DIM]).astype(BF16)
        q_ref[:, hd * QK_SLOT + QK_NOPE_DIM:(hd + 1) * QK_SLOT] = (
            _rope_slot(qp, qg[:, QK_NOPE_DIM:], cs) * r).astype(BF16)


def _mla_proj(h, g, w, cs, batch, length):
    n, d = h.shape
    nblk = length // ROW_TILE
    slots = MLA_HEADS * QK_SLOT
    dv = MLA_HEADS * V_HEAD_DIM
    row = lambda width: pl.BlockSpec((ROW_TILE, width), lambda i: (i, 0))
    return pl.pallas_call(
        _mla_proj_body,
        grid=(n // ROW_TILE,),
        in_specs=[row(d), _resident((1, d)), _resident(w["dq"].shape), _resident((1, Q_LORA_RANK)),
                  _resident(w["uq"].shape), _resident(w["dkv"].shape), _resident((1, KV_LORA_RANK)),
                  _resident(w["k"].shape), _resident(w["vt"].shape),
                  _resident((1, QK_SLOT)), _resident((1, QK_SLOT)),
                  pl.BlockSpec((ROW_TILE, LANES_V7X), lambda i: (i % nblk, 0))],
        out_specs=[row(slots), row(slots),
                   pl.BlockSpec((None, dv, ROW_TILE), lambda i: (i // nblk, 0, i % nblk))],
        out_shape=[jax.ShapeDtypeStruct((n, slots), BF16), jax.ShapeDtypeStruct((n, slots), BF16),
                   jax.ShapeDtypeStruct((batch, dv, length), BF16)],
        compiler_params=_params("parallel"),
        name="mla_proj",
    )(h, g, w["dq"], w["qa"], w["uq"], w["dkv"], w["kva"], w["k"], w["vt"], w["qg"], w["kg"], cs)


def _attn_body(q_ref, k_ref, vt_ref, o_ref, qt_ref, s_ref, m_ref, l_ref, acc_ref):
    n_pairs = k_ref.shape[0] // (2 * ATTN_TK)
    qt_ref[...] = q_ref[...].astype(F32).T.astype(BF16)
    m_ref[...] = jnp.full(m_ref.shape, -jnp.inf, F32)
    l_ref[...] = jnp.zeros(l_ref.shape, F32)
    acc_ref[...] = jnp.zeros(acc_ref.shape, F32)

    def scores(j, slot):
        off = pl.multiple_of(j * ATTN_TK, ATTN_TK)
        s_ref[slot] = _mm(k_ref[pl.ds(off, ATTN_TK), :], qt_ref[...])

    def update(j, slot):
        off = pl.multiple_of(j * ATTN_TK, ATTN_TK)
        st = s_ref[slot]
        m_old = m_ref[...]
        m_new = jnp.maximum(m_old, jnp.max(st, axis=0, keepdims=True))
        p = jnp.exp2(st - m_new)
        alpha = jnp.exp2(m_old - m_new)
        l_ref[...] = alpha * l_ref[...] + jnp.sum(p, axis=0, keepdims=True)
        acc_ref[...] = alpha * acc_ref[...] + _mm(vt_ref[:, pl.ds(off, ATTN_TK)], p.astype(BF16))
        m_ref[...] = m_new

    scores(0, 0)

    def pair(jj, carry):
        j = 2 * jj
        scores(j + 1, 1)
        update(j, 0)
        scores(lax.rem(j + 2, 2 * n_pairs), 0)
        update(j + 1, 1)
        return carry

    lax.fori_loop(0, n_pairs, pair, 0)
    o_ref[...] = (acc_ref[...] / l_ref[...]).T.astype(BF16)


def _attention(q, k, vt, batch, length):
    q = q.reshape(batch, length, MLA_HEADS * QK_SLOT)
    k = k.reshape(batch, length, MLA_HEADS * QK_SLOT)
    out = pl.pallas_call(
        _attn_body,
        grid=(batch, MLA_HEADS, length // ATTN_TQ),
        in_specs=[pl.BlockSpec((None, ATTN_TQ, QK_SLOT), lambda b, h, i: (b, i, h)),
                  pl.BlockSpec((None, length, QK_SLOT), lambda b, h, i: (b, 0, h)),
                  pl.BlockSpec((None, V_HEAD_DIM, length), lambda b, h, i: (b, h, 0))],
        out_specs=pl.BlockSpec((None, ATTN_TQ, V_HEAD_DIM), lambda b, h, i: (b, i, h)),
        out_shape=jax.ShapeDtypeStruct((batch, length, MLA_HEADS * V_HEAD_DIM), BF16),
        scratch_shapes=[pltpu.VMEM((QK_SLOT, ATTN_TQ), BF16),
                        pltpu.VMEM((2, ATTN_TK, ATTN_TQ), F32),
                        pltpu.VMEM((1, ATTN_TQ), F32), pltpu.VMEM((1, ATTN_TQ), F32),
                        pltpu.VMEM((V_HEAD_DIM, ATTN_TQ), F32)],
        compiler_params=_params("parallel", "parallel", "parallel"),
        name="mla_attn",
    )(q, k, vt)
    return out.reshape(batch * length, MLA_HEADS * V_HEAD_DIM)


def _ret_proj_body(h_ref, g_ref, w_ref, cos_ref, sin_ref, q_ref, k_ref, v_ref, gate_ref):
    hn = _rms(h_ref[...], g_ref[...]).astype(BF16)
    cos = cos_ref[...]
    sin = sin_ref[...]
    hq = RET_HEADS * RET_DK
    hv = RET_HEADS * RET_DV
    half = RET_DK // 2

    def rope_store(dst, col0, mult):
        for hd in range(RET_HEADS):
            x = _mm(hn, w_ref[:, col0 + hd * RET_DK:col0 + (hd + 1) * RET_DK])
            x1, x2 = x[:, :half], x[:, half:]
            dst[:, hd * RET_DK:hd * RET_DK + half] = ((x1 * cos - x2 * sin) * mult).astype(BF16)
            dst[:, hd * RET_DK + half:(hd + 1) * RET_DK] = ((x2 * cos + x1 * sin) * mult).astype(BF16)

    rope_store(q_ref, 0, 1.0)
    rope_store(k_ref, hq, RET_DK ** -0.5)
    v_ref[...] = _mm(hn, w_ref[:, 2 * hq:2 * hq + hv]).astype(BF16)
    gate_ref[...] = _mm(hn, w_ref[:, 2 * hq + hv:]).astype(BF16)


def _ret_proj(h, g, w_in, cos, sin, length):
    n, d = h.shape
    nblk = length // ROW_TILE
    hq = RET_HEADS * RET_DK
    hv = RET_HEADS * RET_DV
    row = lambda width: pl.BlockSpec((ROW_TILE, width), lambda i: (i, 0))
    pos = pl.BlockSpec((ROW_TILE, RET_DK // 2), lambda i: (i % nblk, 0))
    return pl.pallas_call(
        _ret_proj_body,
        grid=(n // ROW_TILE,),
        in_specs=[row(d), _resident((1, d)), _resident(w_in.shape), pos, pos],
        out_specs=[row(hq), row(hq), row(hv), row(hv)],
        out_shape=[jax.ShapeDtypeStruct((n, hq), BF16), jax.ShapeDtypeStruct((n, hq), BF16),
                   jax.ShapeDtypeStruct((n, hv), BF16), jax.ShapeDtypeStruct((n, hv), BF16)],
        compiler_params=_params("parallel"),
        name="ret_proj",
    )(h, g, w_in, cos, sin)


def _ret_scan_body(lg_ref, q_ref, k_ref, v_ref, gate_ref, gn_ref, o_ref,
                   state_ref, ob_ref, dmat_ref, xi_ref, zeta_ref, gc_ref):
    hd = pl.program_id(1)
    direction = pl.program_id(2)
    blk = pl.program_id(3)
    nblk = pl.num_programs(3)
    c = RET_CHUNK
    n_chunks = RET_BLOCK // c

    @pl.when(blk == 0)
    def _init():
        lg = lg_ref[direction, hd]
        sign = 2 * direction - 1
        state_ref[...] = jnp.zeros(state_ref.shape, F32)
        row = lax.broadcasted_iota(jnp.int32, (c, c), 0)
        col = lax.broadcasted_iota(jnp.int32, (c, c), 1)
        diff = (row - col) * sign
        keep = diff >= 1 - direction
        dmat_ref[...] = jnp.where(keep, jnp.exp(lg * jnp.maximum(diff, 0).astype(F32)), 0.0)
        qi = lax.broadcasted_iota(jnp.int32, xi_ref.shape, 0)
        xi_ref[...] = jnp.exp(lg * (direction * (qi + 1) + (1 - direction) * (c - qi)).astype(F32))
        kj = lax.broadcasted_iota(jnp.int32, zeta_ref.shape, 0)
        zeta_ref[...] = jnp.exp(lg * (direction * (c - 1 - kj) + (1 - direction) * kj).astype(F32))
        gc_ref[...] = jnp.exp(jnp.zeros(gc_ref.shape, F32) + lg * c)

    def chunk(ci):
        rows = pl.ds(ci * c, c)
        q = q_ref[rows, :]
        k = k_ref[rows, :]
        v = v_ref[rows, :]
        s = lax.dot_general(q, k, _NT, preferred_element_type=F32) * dmat_ref[...]
        o = _mm(s.astype(BF16), v) + _mm(q, state_ref[...].astype(BF16)) * xi_ref[...]
        kz_t = (k.astype(F32) * zeta_ref[...]).T.astype(BF16)
        state_ref[...] = state_ref[...] * gc_ref[...] + _mm(kz_t, v)
        return o

    @pl.when(direction == 0)
    def _backward():
        base = (nblk - 1 - blk) * RET_BLOCK
        for ci in reversed(range(n_chunks)):
            ob_ref[pl.ds(pl.multiple_of(base + ci * c, c), c), :] = chunk(ci)

    @pl.when(direction == 1)
    def _forward():
        base = blk * RET_BLOCK
        for ci in range(n_chunks):
            o = chunk(ci) + ob_ref[pl.ds(pl.multiple_of(base + ci * c, c), c), :]
            mu = jnp.mean(o, axis=-1, keepdims=True)
            dev = o - mu
            var = jnp.mean(dev * dev, axis=-1, keepdims=True)
            on = dev * lax.rsqrt(var + EPS) * gn_ref[...]
            gate = gate_ref[pl.ds(ci * c, c), :].astype(F32)
            o_ref[pl.ds(ci * c, c), :] = (gate * jax.nn.sigmoid(gate) * on).astype(BF16)


def _ret_scan(q, k, v, gate, lg, gn_g, batch, length):
    hq = RET_HEADS * RET_DK
    hv = RET_HEADS * RET_DV
    nblk = length // RET_BLOCK
    q = q.reshape(batch, length, hq)
    k = k.reshape(batch, length, hq)
    v = v.reshape(batch, length, hv)
    gate = gate.reshape(batch, length, hv)

    def sweep(b, h, d, i):
        return (b, d * i + (1 - d) * (nblk - 1 - i), h)

    def fwd_only(b, h, d, i):
        return (b, d * i, h)

    out = pl.pallas_call(
        _ret_scan_body,
        grid=(batch, RET_HEADS, 2, nblk),
        in_specs=[pl.BlockSpec(memory_space=pltpu.SMEM),
                  pl.BlockSpec((None, RET_BLOCK, RET_DK), sweep),
                  pl.BlockSpec((None, RET_BLOCK, RET_DK), sweep),
                  pl.BlockSpec((None, RET_BLOCK, RET_DV), sweep),
                  pl.BlockSpec((None, RET_BLOCK, RET_DV), fwd_only),
                  pl.BlockSpec((1, RET_DV), lambda b, h, d, i: (0, h))],
        out_specs=pl.BlockSpec((None, RET_BLOCK, RET_DV), fwd_only),
        out_shape=jax.ShapeDtypeStruct((batch, length, hv), BF16),
        scratch_shapes=[pltpu.VMEM((RET_DK, RET_DV), F32),
                        pltpu.VMEM((length, RET_DV), F32),
                        pltpu.VMEM((RET_CHUNK, RET_CHUNK), F32),
                        pltpu.VMEM((RET_CHUNK, RET_DV), F32),
                        pltpu.VMEM((RET_CHUNK, RET_DK), F32),
                        pltpu.VMEM((RET_DK, RET_DV), F32)],
        compiler_params=_params("parallel", "parallel", "arbitrary", "arbitrary"),
        name="ret_scan",
    )(lg, q, k, v, gate, gn_g)
    return out.reshape(batch * length, hv)


def _swap_halves(a):
    half = a.shape[-1] // 2
    return jnp.concatenate([a[..., half:], a[..., :half]], axis=-1)


def _rope_slot_layout(a):
    pe = a[..., QK_NOPE_DIM:]
    return jnp.concatenate([a[..., :QK_NOPE_DIM], pe, _swap_halves(pe)], axis=-1)


def _mla_weights(w_dq, q_a_norm, w_uq, w_dkv, kv_a_norm, w_ukv, q_norm, k_norm):
    uq = w_uq.reshape(Q_LORA_RANK, MLA_HEADS, QK_HEAD_DIM)
    uq = _rope_slot_layout(uq).reshape(Q_LORA_RANK, MLA_HEADS * QK_SLOT)
    pe = w_dkv[:, KV_LORA_RANK:]
    dkv = jnp.concatenate([w_dkv[:, :KV_LORA_RANK], pe, _swap_halves(pe)], axis=-1)
    ukv = w_ukv.reshape(KV_LORA_RANK, MLA_HEADS, QK_NOPE_DIM + V_HEAD_DIM)
    wk = ukv[..., :QK_NOPE_DIM].reshape(KV_LORA_RANK, MLA_HEADS * QK_NOPE_DIM)
    wvt = ukv[..., QK_NOPE_DIM:].reshape(KV_LORA_RANK, MLA_HEADS * V_HEAD_DIM).T
    return {
        "dq": w_dq.astype(BF16), "qa": q_a_norm[None, :], "uq": uq.astype(BF16),
        "dkv": dkv.astype(BF16), "kva": kv_a_norm[None, :],
        "k": wk.astype(BF16), "vt": wvt.astype(BF16),
        "qg": _rope_slot_layout(q_norm)[None, :], "kg": _rope_slot_layout(k_norm)[None, :],
    }


def _rope_tables(length, dim):
    inv = ROPE_BASE ** (-jnp.arange(0, dim, 2, dtype=F32) / dim)
    ang = jnp.arange(length, dtype=F32)[:, None] * inv[None, :]
    return jnp.cos(ang), jnp.sin(ang)


def _trunk(x, p):
    batch, length, d = x.shape
    x = x.reshape(batch * length, d)
    cos_m, sin_m = _rope_tables(length, QK_ROPE_DIM)
    cs_mla = jnp.concatenate([cos_m, cos_m, -sin_m, sin_m], axis=-1)
    cos_r, sin_r = _rope_tables(length, RET_DK)
    for i in range(DEPTH):
        j = i // 2
        ng = p["norm_g"][i]
        f = p["ffn"][i]
        h = _ffn(x, ng[0:1], f[0]["g"], f[0]["u"], f[0]["d"], ng[0:1], False)
        if i % 2 == 0:
            w = p["mla"][j]
            q, k, vt = _mla_proj(h, ng[1:2], w, cs_mla, batch, length)
            o = _attention(q, k, vt, batch, length)
            h = _proj_res(h, o, w["o"])
        else:
            w = p["ret"][j]
            q, k, v, gate = _ret_proj(h, ng[1:2], w["in"], cos_r, sin_r, length)
            o = _ret_scan(q, k, v, gate, w["lg"], w["gn"], batch, length)
            h = _proj_res(h, o, w["o"])
        x = _ffn(h, ng[2:3], f[1]["g"], f[1]["u"], f[1]["d"], ng[3:4], True)
    return x.reshape(batch, length, d)


def kernel(x_prompt, x_sample, norm_g, ffn_w_gate, ffn_w_up, ffn_w_down, mla_w_dq, mla_q_a_norm, mla_w_uq, mla_w_dkv, mla_kv_a_norm, mla_w_ukv, mla_q_norm, mla_k_norm, mla_w_o, ret_w_in, ret_decay_fwd, ret_decay_bwd, ret_gn_g, ret_w_o):
    p = {
        "norm_g": norm_g,
        "ffn": [[{"g": ffn_w_gate[i, s].astype(BF16), "u": ffn_w_up[i, s].astype(BF16),
                  "d": ffn_w_down[i, s].astype(BF16)} for s in range(2)] for i in range(DEPTH)],
        "mla": [], "ret": [],
    }
    for j in range(mla_w_dq.shape[0]):
        w = _mla_weights(mla_w_dq[j], mla_q_a_norm[j], mla_w_uq[j], mla_w_dkv[j], mla_kv_a_norm[j],
                         mla_w_ukv[j], mla_q_norm[j], mla_k_norm[j])
        w["o"] = mla_w_o[j].astype(BF16)
        p["mla"].append(w)
    for j in range(ret_w_in.shape[0]):
        lg = jnp.stack([jnp.log1p(-jnp.exp(ret_decay_bwd[j].astype(F32))),
                        jnp.log1p(-jnp.exp(ret_decay_fwd[j].astype(F32)))])
        p["ret"].append({"in": ret_w_in[j].astype(BF16), "lg": lg, "gn": ret_gn_g[j][None, :],
                         "o": ret_w_o[j].astype(BF16)})
    return _trunk(x_prompt, p), _trunk(x_sample, p)
```

```python
import functools
import math

import jax
import jax.numpy as jnp
from jax import lax
from jax.experimental import pallas as pl
from jax.experimental.pallas import tpu as pltpu

F32 = jnp.float32
BF16 = jnp.bfloat16

D_MODEL = 1024
DEPTH = 4
D_FF = 2816
MLA_HEADS = 8
QK_NOPE_DIM = 128
QK_ROPE_DIM = 64
QK_HEAD_DIM = QK_NOPE_DIM + QK_ROPE_DIM
V_HEAD_DIM = 128
Q_LORA_RANK = 384
KV_LORA_RANK = 256
ROPE_BASE = 10000.0
RET_HEADS = 4
RET_DK = D_MODEL // RET_HEADS
RET_DV = 2 * D_MODEL // RET_HEADS
EPS = 1e-6

LANES_V7X = 128
MXU_DIM_V7X = 256
VMEM_LIMIT_BYTES_V7X = 56 * 1024 * 1024

QK_SLOT = QK_NOPE_DIM + 2 * QK_ROPE_DIM
assert QK_SLOT == MXU_DIM_V7X
BF16_SUBLANES_V7X = 16
V_SLOT = V_HEAD_DIM + BF16_SUBLANES_V7X

ROW_TILE = 512
ATTN_TQ = 512
ATTN_TK = 512
ATTN_LOOKAHEAD = 2
ATTN_BUFFERS = 4
RET_CHUNK = MXU_DIM_V7X
RET_BLOCK = 1024

_NT = (((1,), (1,)), ((), ()))


def _params(*sem):
    return pltpu.CompilerParams(dimension_semantics=sem, vmem_limit_bytes=VMEM_LIMIT_BYTES_V7X)


def _resident(shape):
    return pl.BlockSpec(shape, lambda *_: (0,) * len(shape), pipeline_mode=pl.Buffered(1))


def _rms(x, g):
    return x * lax.rsqrt(jnp.mean(x * x, axis=-1, keepdims=True) + EPS) * g


def _mm(a, b):
    return jnp.dot(a, b, preferred_element_type=F32)


def _ffn_body(x_ref, g_ref, wg_ref, wu_ref, wd_ref, gp_ref, o_ref, *, post_norm):
    x = x_ref[...]
    xn = _rms(x, g_ref[...]).astype(BF16)
    gate = _mm(xn, wg_ref[...])
    up = _mm(xn, wu_ref[...])
    h = (gate * jax.nn.sigmoid(gate) * up).astype(BF16)
    y = x + 0.5 * _mm(h, wd_ref[...])
    if post_norm:
        y = _rms(y, gp_ref[...])
    o_ref[...] = y


def _ffn(x, g, wg, wu, wd, g_post, post_norm):
    n, d = x.shape
    row = pl.BlockSpec((ROW_TILE, d), lambda i: (i, 0))
    return pl.pallas_call(
        functools.partial(_ffn_body, post_norm=post_norm),
        grid=(n // ROW_TILE,),
        in_specs=[row, _resident((1, d)), _resident(wg.shape), _resident(wu.shape),
                  _resident(wd.shape), _resident((1, d))],
        out_specs=row,
        out_shape=jax.ShapeDtypeStruct((n, d), F32),
        compiler_params=_params("parallel"),
        name="ffn",
    )(x, g, wg, wu, wd, g_post)


def _proj_res_body(h_ref, a_ref, w_ref, o_ref):
    o_ref[...] = h_ref[...] + _mm(a_ref[...], w_ref[...])


def _proj_res(h, a, w):
    n, d = h.shape
    return pl.pallas_call(
        _proj_res_body,
        grid=(n // ROW_TILE,),
        in_specs=[pl.BlockSpec((ROW_TILE, d), lambda i: (i, 0)),
                  pl.BlockSpec((ROW_TILE, a.shape[1]), lambda i: (i, 0)),
                  _resident(w.shape)],
        out_specs=pl.BlockSpec((ROW_TILE, d), lambda i: (i, 0)),
        out_shape=jax.ShapeDtypeStruct((n, d), F32),
        compiler_params=_params("parallel"),
        name="proj_res",
    )(h, a, w)


def _rope_slot(t, gain, cs):
    a = t * gain * cs
    lane = lax.broadcasted_iota(jnp.int32, a.shape, 1)
    return jnp.where(lane < QK_ROPE_DIM, a + pltpu.roll(a, QK_ROPE_DIM, axis=1), 0.0)


def _sumsq_rope(t):
    lane = lax.broadcasted_iota(jnp.int32, t.shape, 1)
    return jnp.sum(jnp.where(lane < QK_ROPE_DIM, t * t, 0.0), axis=-1, keepdims=True)


def _mla_proj_body(h_ref, g_ref, wdq_ref, qa_ref, wuqt_ref, wdkv_ref, kva_ref, wk_ref, wvt_ref,
                   qgt_ref, kg_ref, cs_ref, cst_ref, qt_ref, k_ref, vt_ref):
    hn = _rms(h_ref[...], g_ref[...]).astype(BF16)
    cs = cs_ref[...]
    kg = kg_ref[...]
    inv_dim = 1.0 / QK_HEAD_DIM

    ckv_full = _mm(hn, wdkv_ref[...])
    ckv = _rms(ckv_full[:, :KV_LORA_RANK], kva_ref[...]).astype(BF16)
    vt = lax.dot_general(wvt_ref[...], ckv, _NT, preferred_element_type=F32).astype(BF16)
    for hd in range(MLA_HEADS):
        vt_ref[hd * V_SLOT:hd * V_SLOT + V_HEAD_DIM, :] = vt[hd * V_HEAD_DIM:(hd + 1) * V_HEAD_DIM, :]
        vt_ref[hd * V_SLOT + V_HEAD_DIM:(hd + 1) * V_SLOT, :] = jnp.ones((BF16_SUBLANES_V7X, vt.shape[1]), BF16)
    k_pe_raw = ckv_full[:, KV_LORA_RANK:]
    k_pe_ss = _sumsq_rope(k_pe_raw)
    k_pe = _rope_slot(k_pe_raw, kg[:, QK_NOPE_DIM:], cs)
    k_nope = _mm(ckv, wk_ref[...])
    for hd in range(MLA_HEADS):
        kn = k_nope[:, hd * QK_NOPE_DIM:(hd + 1) * QK_NOPE_DIM]
        r = lax.rsqrt((jnp.sum(kn * kn, axis=-1, keepdims=True) + k_pe_ss) * inv_dim + EPS)
        k_ref[:, hd * QK_SLOT:hd * QK_SLOT + QK_NOPE_DIM] = (kn * r * kg[:, :QK_NOPE_DIM]).astype(BF16)
        k_ref[:, hd * QK_SLOT + QK_NOPE_DIM:(hd + 1) * QK_SLOT] = (k_pe * r).astype(BF16)

    cq = _rms(_mm(hn, wdq_ref[...]), qa_ref[...]).astype(BF16)
    scale = QK_HEAD_DIM ** -0.5 * math.log2(math.e)
    cst = cst_ref[...]
    qgt = qgt_ref[...]
    for hd in range(MLA_HEADS):
        base = hd * QK_SLOT
        qt = lax.dot_general(wuqt_ref[base:base + QK_SLOT, :], cq, _NT, preferred_element_type=F32)
        qn, qp = qt[:QK_NOPE_DIM], qt[QK_NOPE_DIM:]
        pe = qp[:QK_ROPE_DIM]
        ss = jnp.sum(qn * qn, axis=0, keepdims=True) + jnp.sum(pe * pe, axis=0, keepdims=True)
        r = lax.rsqrt(ss * inv_dim + EPS) * scale
        a = qp * qgt[QK_NOPE_DIM:] * cst
        qt_ref[base:base + QK_NOPE_DIM, :] = (qn * r * qgt[:QK_NOPE_DIM]).astype(BF16)
        qt_ref[base + QK_NOPE_DIM:base + QK_HEAD_DIM, :] = (
            (a[:QK_ROPE_DIM] + a[QK_ROPE_DIM:]) * r).astype(BF16)
        qt_ref[base + QK_HEAD_DIM:base + QK_SLOT, :] = jnp.zeros((QK_SLOT - QK_HEAD_DIM, qt.shape[1]), BF16)


def _mla_proj(h, g, w, cs, batch, length):
    n, d = h.shape
    nblk = length // ROW_TILE
    slots = MLA_HEADS * QK_SLOT
    dv = MLA_HEADS * V_SLOT
    row = lambda width: pl.BlockSpec((ROW_TILE, width), lambda i: (i, 0))
    return pl.pallas_call(
        _mla_proj_body,
        grid=(n // ROW_TILE,),
        in_specs=[row(d), _resident((1, d)), _resident(w["dq"].shape), _resident((1, Q_LORA_RANK)),
                  _resident(w["uqt"].shape), _resident(w["dkv"].shape), _resident((1, KV_LORA_RANK)),
                  _resident(w["k"].shape), _resident(w["vt"].shape),
                  _resident((QK_SLOT, ROW_TILE)), _resident((1, QK_SLOT)),
                  pl.BlockSpec((ROW_TILE, LANES_V7X), lambda i: (i % nblk, 0)),
                  pl.BlockSpec((LANES_V7X, ROW_TILE), lambda i: (0, i % nblk))],
        out_specs=[pl.BlockSpec((None, slots, ROW_TILE), lambda i: (i // nblk, 0, i % nblk)),
                   row(slots),
                   pl.BlockSpec((None, dv, ROW_TILE), lambda i: (i // nblk, 0, i % nblk))],
        out_shape=[jax.ShapeDtypeStruct((batch, slots, length), BF16), jax.ShapeDtypeStruct((n, slots), BF16),
                   jax.ShapeDtypeStruct((batch, dv, length), BF16)],
        compiler_params=_params("parallel"),
        name="mla_proj",
    )(h, g, w["dq"], w["qa"], w["uqt"], w["dkv"], w["kva"], w["k"], w["vt"], w["qgt"], w["kg"], cs, cs.T)


def _attn_body(qt_ref, k_ref, vt_ref, o_ref, s_ref, smax_ref, m_ref, acc_ref):
    length = k_ref.shape[0]
    n_chunks = length // ATTN_TK
    total = (length // ATTN_TQ) * n_chunks
    trips_per_q = n_chunks // ATTN_BUFFERS

    def scores(g, slot):
        q_off = pl.multiple_of((g // n_chunks) * ATTN_TQ, ATTN_TQ)
        k_off = pl.multiple_of(lax.rem(g, n_chunks) * ATTN_TK, ATTN_TK)
        st = _mm(k_ref[pl.ds(k_off, ATTN_TK), :], qt_ref[:, pl.ds(q_off, ATTN_TQ)])
        s_ref[slot] = st
        smax_ref[slot] = jnp.max(st, axis=0, keepdims=True)

    def update(j, slot):
        k_off = pl.multiple_of(j * ATTN_TK, ATTN_TK)
        m_old = m_ref[...]
        m_new = jnp.maximum(m_old, smax_ref[slot])
        p = jnp.exp2(s_ref[slot] - m_new).astype(BF16)
        alpha = jnp.exp2(m_old - m_new)
        acc_ref[...] = alpha * acc_ref[...] + _mm(vt_ref[:, pl.ds(k_off, ATTN_TK)], p)
        m_ref[...] = m_new

    for g in range(ATTN_LOOKAHEAD):
        scores(g, g)

    def trip(t, carry):
        jt = lax.rem(t, trips_per_q)

        @pl.when(jt == 0)
        def _start_q_tile():
            m_ref[...] = jnp.full(m_ref.shape, -jnp.inf, F32)
            acc_ref[...] = jnp.zeros(acc_ref.shape, F32)

        for i in range(ATTN_BUFFERS):
            ahead = lax.rem(t * ATTN_BUFFERS + i + ATTN_LOOKAHEAD, total)
            scores(ahead, (i + ATTN_LOOKAHEAD) % ATTN_BUFFERS)
            update(jt * ATTN_BUFFERS + i, i)

        @pl.when(jt == trips_per_q - 1)
        def _finish_q_tile():
            q_off = pl.multiple_of((t // trips_per_q) * ATTN_TQ, ATTN_TQ)
            o = acc_ref[:V_HEAD_DIM, :] / acc_ref[V_HEAD_DIM:V_HEAD_DIM + 1, :]
            o_ref[pl.ds(q_off, ATTN_TQ), :] = o.T.astype(BF16)

        return carry

    lax.fori_loop(0, total // ATTN_BUFFERS, trip, 0)


def _attention(qt, k, vt, batch, length):
    k = k.reshape(batch, length, MLA_HEADS * QK_SLOT)
    out = pl.pallas_call(
        _attn_body,
        grid=(batch, MLA_HEADS),
        in_specs=[pl.BlockSpec((None, QK_SLOT, length), lambda b, h: (b, h, 0)),
                  pl.BlockSpec((None, length, QK_SLOT), lambda b, h: (b, 0, h)),
                  pl.BlockSpec((None, V_SLOT, length), lambda b, h: (b, h, 0))],
        out_specs=pl.BlockSpec((None, length, V_HEAD_DIM), lambda b, h: (b, 0, h)),
        out_shape=jax.ShapeDtypeStruct((batch, length, MLA_HEADS * V_HEAD_DIM), BF16),
        scratch_shapes=[pltpu.VMEM((ATTN_BUFFERS, ATTN_TK, ATTN_TQ), F32),
                        pltpu.VMEM((ATTN_BUFFERS, 1, ATTN_TQ), F32), pltpu.VMEM((1, ATTN_TQ), F32),
                        pltpu.VMEM((V_SLOT, ATTN_TQ), F32)],
        compiler_params=_params("parallel", "parallel"),
        name="mla_attn",
    )(qt, k, vt)
    return out.reshape(batch * length, MLA_HEADS * V_HEAD_DIM)


def _ret_proj_body(h_ref, g_ref, w_ref, cos_ref, sin_ref, q_ref, k_ref, v_ref, gate_ref):
    hn = _rms(h_ref[...], g_ref[...]).astype(BF16)
    cos = cos_ref[...]
    sin = sin_ref[...]
    hq = RET_HEADS * RET_DK
    hv = RET_HEADS * RET_DV
    half = RET_DK // 2

    def rope_store(dst, col0, mult):
        for hd in range(RET_HEADS):
            x = _mm(hn, w_ref[:, col0 + hd * RET_DK:col0 + (hd + 1) * RET_DK])
            x1, x2 = x[:, :half], x[:, half:]
            dst[:, hd * RET_DK:hd * RET_DK + half] = ((x1 * cos - x2 * sin) * mult).astype(BF16)
            dst[:, hd * RET_DK + half:(hd + 1) * RET_DK] = ((x2 * cos + x1 * sin) * mult).astype(BF16)

    rope_store(q_ref, 0, 1.0)
    rope_store(k_ref, hq, RET_DK ** -0.5)
    v_ref[...] = _mm(hn, w_ref[:, 2 * hq:2 * hq + hv]).astype(BF16)
    gate_ref[...] = _mm(hn, w_ref[:, 2 * hq + hv:]).astype(BF16)


def _ret_proj(h, g, w_in, cos, sin, length):
    n, d = h.shape
    nblk = length // ROW_TILE
    hq = RET_HEADS * RET_DK
    hv = RET_HEADS * RET_DV
    row = lambda width: pl.BlockSpec((ROW_TILE, width), lambda i: (i, 0))
    pos = pl.BlockSpec((ROW_TILE, RET_DK // 2), lambda i: (i % nblk, 0))
    return pl.pallas_call(
        _ret_proj_body,
        grid=(n // ROW_TILE,),
        in_specs=[row(d), _resident((1, d)), _resident(w_in.shape), pos, pos],
        out_specs=[row(hq), row(hq), row(hv), row(hv)],
        out_shape=[jax.ShapeDtypeStruct((n, hq), BF16), jax.ShapeDtypeStruct((n, hq), BF16),
                   jax.ShapeDtypeStruct((n, hv), BF16), jax.ShapeDtypeStruct((n, hv), BF16)],
        compiler_params=_params("parallel"),
        name="ret_proj",
    )(h, g, w_in, cos, sin)


def _ret_scan_body(lg_ref, q_ref, k_ref, v_ref, gate_ref, gn_ref, o_ref,
                   state_ref, ob_ref, dmat_ref, xi_ref, zeta_ref, gc_ref):
    hd = pl.program_id(1)
    direction = pl.program_id(2)
    blk = pl.program_id(3)
    nblk = pl.num_programs(3)
    c = RET_CHUNK
    n_chunks = RET_BLOCK // c

    @pl.when(blk == 0)
    def _init():
        lg = lg_ref[direction, hd]
        sign = 2 * direction - 1
        state_ref[...] = jnp.zeros(state_ref.shape, F32)
        row = lax.broadcasted_iota(jnp.int32, (c, c), 0)
        col = lax.broadcasted_iota(jnp.int32, (c, c), 1)
        diff = (row - col) * sign
        keep = diff >= 1 - direction
        dmat_ref[...] = jnp.where(keep, jnp.exp(lg * jnp.maximum(diff, 0).astype(F32)), 0.0)
        qi = lax.broadcasted_iota(jnp.int32, xi_ref.shape, 0)
        xi_ref[...] = jnp.exp(lg * (direction * (qi + 1) + (1 - direction) * (c - qi)).astype(F32))
        kj = lax.broadcasted_iota(jnp.int32, zeta_ref.shape, 0)
        zeta_ref[...] = jnp.exp(lg * (direction * (c - 1 - kj) + (1 - direction) * kj).astype(F32))
        gc_ref[...] = jnp.exp(jnp.zeros(gc_ref.shape, F32) + lg * c)

    def chunk(ci):
        rows = pl.ds(ci * c, c)
        q = q_ref[rows, :]
        k = k_ref[rows, :]
        v = v_ref[rows, :]
        s = lax.dot_general(q, k, _NT, preferred_element_type=F32) * dmat_ref[...]
        o = _mm(s.astype(BF16), v) + _mm(q, state_ref[...].astype(BF16)) * xi_ref[...]
        kz_t = (k.astype(F32) * zeta_ref[...]).T.astype(BF16)
        state_ref[...] = state_ref[...] * gc_ref[...] + _mm(kz_t, v)
        return o

    @pl.when(direction == 0)
    def _backward():
        base = (nblk - 1 - blk) * RET_BLOCK
        for ci in reversed(range(n_chunks)):
            ob_ref[pl.ds(pl.multiple_of(base + ci * c, c), c), :] = chunk(ci)

    @pl.when(direction == 1)
    def _forward():
        base = blk * RET_BLOCK
        for ci in range(n_chunks):
            o = chunk(ci) + ob_ref[pl.ds(pl.multiple_of(base + ci * c, c), c), :]
            mu = jnp.mean(o, axis=-1, keepdims=True)
            dev = o - mu
            var = jnp.mean(dev * dev, axis=-1, keepdims=True)
            on = dev * lax.rsqrt(var + EPS) * gn_ref[...]
            gate = gate_ref[pl.ds(ci * c, c), :].astype(F32)
            o_ref[pl.ds(ci * c, c), :] = (gate * jax.nn.sigmoid(gate) * on).astype(BF16)


def _ret_scan(q, k, v, gate, lg, gn_g, batch, length):
    hq = RET_HEADS * RET_DK
    hv = RET_HEADS * RET_DV
    nblk = length // RET_BLOCK
    q = q.reshape(batch, length, hq)
    k = k.reshape(batch, length, hq)
    v = v.reshape(batch, length, hv)
    gate = gate.reshape(batch, length, hv)

    def sweep(b, h, d, i):
        return (b, d * i + (1 - d) * (nblk - 1 - i), h)

    def fwd_only(b, h, d, i):
        return (b, d * i, h)

    out = pl.pallas_call(
        _ret_scan_body,
        grid=(batch, RET_HEADS, 2, nblk),
        in_specs=[pl.BlockSpec(memory_space=pltpu.SMEM),
                  pl.BlockSpec((None, RET_BLOCK, RET_DK), sweep),
                  pl.BlockSpec((None, RET_BLOCK, RET_DK), sweep),
                  pl.BlockSpec((None, RET_BLOCK, RET_DV), sweep),
                  pl.BlockSpec((None, RET_BLOCK, RET_DV), fwd_only),
                  pl.BlockSpec((1, RET_DV), lambda b, h, d, i: (0, h))],
        out_specs=pl.BlockSpec((None, RET_BLOCK, RET_DV), fwd_only),
        out_shape=jax.ShapeDtypeStruct((batch, length, hv), BF16),
        scratch_shapes=[pltpu.VMEM((RET_DK, RET_DV), F32),
                        pltpu.VMEM((length, RET_DV), F32),
                        pltpu.VMEM((RET_CHUNK, RET_CHUNK), F32),
                        pltpu.VMEM((RET_CHUNK, RET_DV), F32),
                        pltpu.VMEM((RET_CHUNK, RET_DK), F32),
                        pltpu.VMEM((RET_DK, RET_DV), F32)],
        compiler_params=_params("parallel", "parallel", "arbitrary", "arbitrary"),
        name="ret_scan",
    )(lg, q, k, v, gate, gn_g)
    return out.reshape(batch * length, hv)


def _swap_halves(a):
    half = a.shape[-1] // 2
    return jnp.concatenate([a[..., half:], a[..., :half]], axis=-1)


def _rope_slot_layout(a):
    pe = a[..., QK_NOPE_DIM:]
    return jnp.concatenate([a[..., :QK_NOPE_DIM], pe, _swap_halves(pe)], axis=-1)


def _mla_weights(w_dq, q_a_norm, w_uq, w_dkv, kv_a_norm, w_ukv, q_norm, k_norm):
    uq = w_uq.reshape(Q_LORA_RANK, MLA_HEADS, QK_HEAD_DIM)
    uq = _rope_slot_layout(uq).reshape(Q_LORA_RANK, MLA_HEADS * QK_SLOT)
    pe = w_dkv[:, KV_LORA_RANK:]
    dkv = jnp.concatenate([w_dkv[:, :KV_LORA_RANK], pe, _swap_halves(pe)], axis=-1)
    ukv = w_ukv.reshape(KV_LORA_RANK, MLA_HEADS, QK_NOPE_DIM + V_HEAD_DIM)
    wk = ukv[..., :QK_NOPE_DIM].reshape(KV_LORA_RANK, MLA_HEADS * QK_NOPE_DIM)
    wvt = ukv[..., QK_NOPE_DIM:].reshape(KV_LORA_RANK, MLA_HEADS * V_HEAD_DIM).T
    return {
        "dq": w_dq.astype(BF16), "qa": q_a_norm[None, :], "uqt": uq.T.astype(BF16),
        "dkv": dkv.astype(BF16), "kva": kv_a_norm[None, :],
        "k": wk.astype(BF16), "vt": wvt.astype(BF16),
        "qgt": jnp.broadcast_to(_rope_slot_layout(q_norm)[:, None], (QK_SLOT, ROW_TILE)),
        "kg": _rope_slot_layout(k_norm)[None, :],
    }


def _rope_tables(length, dim):
    inv = ROPE_BASE ** (-jnp.arange(0, dim, 2, dtype=F32) / dim)
    ang = jnp.arange(length, dtype=F32)[:, None] * inv[None, :]
    return jnp.cos(ang), jnp.sin(ang)


def _trunk(x, p):
    batch, length, d = x.shape
    x = x.reshape(batch * length, d)
    cos_m, sin_m = _rope_tables(length, QK_ROPE_DIM)
    cs_mla = jnp.concatenate([cos_m, cos_m, -sin_m, sin_m], axis=-1)
    cos_r, sin_r = _rope_tables(length, RET_DK)
    for i in range(DEPTH):
        j = i // 2
        ng = p["norm_g"][i]
        f = p["ffn"][i]
        h = _ffn(x, ng[0:1], f[0]["g"], f[0]["u"], f[0]["d"], ng[0:1], False)
        if i % 2 == 0:
            w = p["mla"][j]
            qt, k, vt = _mla_proj(h, ng[1:2], w, cs_mla, batch, length)
            o = _attention(qt, k, vt, batch, length)
            h = _proj_res(h, o, w["o"])
        else:
            w = p["ret"][j]
            q, k, v, gate = _ret_proj(h, ng[1:2], w["in"], cos_r, sin_r, length)
            o = _ret_scan(q, k, v, gate, w["lg"], w["gn"], batch, length)
            h = _proj_res(h, o, w["o"])
        x = _ffn(h, ng[2:3], f[1]["g"], f[1]["u"], f[1]["d"], ng[3:4], True)
    return x.reshape(batch, length, d)


def kernel(x_prompt, x_sample, norm_g, ffn_w_gate, ffn_w_up, ffn_w_down, mla_w_dq, mla_q_a_norm, mla_w_uq, mla_w_dkv, mla_kv_a_norm, mla_w_ukv, mla_q_norm, mla_k_norm, mla_w_o, ret_w_in, ret_decay_fwd, ret_decay_bwd, ret_gn_g, ret_w_o):
    p = {
        "norm_g": norm_g,
        "ffn": [[{"g": ffn_w_gate[i, s].astype(BF16), "u": ffn_w_up[i, s].astype(BF16),
                  "d": ffn_w_down[i, s].astype(BF16)} for s in range(2)] for i in range(DEPTH)],
        "mla": [], "ret": [],
    }
    for j in range(mla_w_dq.shape[0]):
        w = _mla_weights(mla_w_dq[j], mla_q_a_norm[j], mla_w_uq[j], mla_w_dkv[j], mla_kv_a_norm[j],
                         mla_w_ukv[j], mla_q_norm[j], mla_k_norm[j])
        w["o"] = mla_w_o[j].astype(BF16)
        p["mla"].append(w)
    for j in range(ret_w_in.shape[0]):
        lg = jnp.stack([jnp.log1p(-jnp.exp(ret_decay_bwd[j].astype(F32))),
                        jnp.log1p(-jnp.exp(ret_decay_fwd[j].astype(F32)))])
        p["ret"].append({"in": ret_w_in[j].astype(BF16), "lg": lg, "gn": ret_gn_g[j][None, :],
                         "o": ret_w_o[j].astype(BF16)})
    return _trunk(x_prompt, p), _trunk(x_sample, p)
```

```python
import math

import jax
import jax.numpy as jnp
from jax import lax
from jax.experimental import pallas as pl
from jax.experimental.pallas import tpu as pltpu

F32 = jnp.float32
BF16 = jnp.bfloat16

D_MODEL = 1024
DEPTH = 4
D_FF = 2816
MLA_HEADS = 8
QK_NOPE_DIM = 128
QK_ROPE_DIM = 64
QK_HEAD_DIM = QK_NOPE_DIM + QK_ROPE_DIM
V_HEAD_DIM = 128
Q_LORA_RANK = 384
KV_LORA_RANK = 256
ROPE_BASE = 10000.0
RET_HEADS = 4
RET_DK = D_MODEL // RET_HEADS
RET_DV = 2 * D_MODEL // RET_HEADS
EPS = 1e-6

LANES_V7X = 128
MXU_DIM_V7X = 256
VMEM_LIMIT_BYTES_V7X = 56 * 1024 * 1024

QK_SLOT = QK_NOPE_DIM + 2 * QK_ROPE_DIM
assert QK_SLOT == MXU_DIM_V7X
BF16_SUBLANES_V7X = 16
V_SLOT = V_HEAD_DIM + BF16_SUBLANES_V7X

ROW_TILE = 512
FFN_ROW_TILE = 512
ATTN_TQ = 512
ATTN_TK = 512
ATTN_LOOKAHEAD = 2
ATTN_BUFFERS = 4
RET_CHUNK = MXU_DIM_V7X
RET_BLOCK = 1024
RET_NORM_ROWS = 64

_NT = (((1,), (1,)), ((), ()))


def _params(*sem):
    return pltpu.CompilerParams(dimension_semantics=sem, vmem_limit_bytes=VMEM_LIMIT_BYTES_V7X)


def _resident(shape):
    return pl.BlockSpec(shape, lambda *_: (0,) * len(shape), pipeline_mode=pl.Buffered(1))


def _rms(x, g):
    return x * lax.rsqrt(jnp.mean(x * x, axis=-1, keepdims=True) + EPS) * g


def _mm(a, b):
    return jnp.dot(a, b, preferred_element_type=F32)


def _swiglu_half(x, g_ref, wg_ref, wu_ref, wd_ref):
    xn = _rms(x, g_ref[...]).astype(BF16)
    gate = _mm(xn, wg_ref[...])
    up = _mm(xn, wu_ref[...])
    h = (gate * jax.nn.sigmoid(gate) * up).astype(BF16)
    return x + 0.5 * _mm(h, wd_ref[...])


def _ffn_first_body(x_ref, g_ref, wg_ref, wu_ref, wd_ref, o_ref):
    o_ref[...] = _swiglu_half(x_ref[...], g_ref, wg_ref, wu_ref, wd_ref)


def _ffn_second_body(x_ref, a_ref, wo_ref, g_ref, wg_ref, wu_ref, wd_ref, gp_ref, o_ref):
    h = x_ref[...] + _mm(a_ref[...], wo_ref[...])
    o_ref[...] = _rms(_swiglu_half(h, g_ref, wg_ref, wu_ref, wd_ref), gp_ref[...])


def _ffn_first(x, g, f):
    n, d = x.shape
    row = pl.BlockSpec((FFN_ROW_TILE, d), lambda i: (i, 0))
    return pl.pallas_call(
        _ffn_first_body,
        grid=(n // FFN_ROW_TILE,),
        in_specs=[row, _resident((1, d)), _resident(f["g"].shape), _resident(f["u"].shape),
                  _resident(f["d"].shape)],
        out_specs=row,
        out_shape=jax.ShapeDtypeStruct((n, d), F32),
        compiler_params=_params("parallel"),
        name="ffn_first",
    )(x, g, f["g"], f["u"], f["d"])


def _ffn_second(x, a, w_o, g, f, g_post):
    n, d = x.shape
    row = pl.BlockSpec((FFN_ROW_TILE, d), lambda i: (i, 0))
    return pl.pallas_call(
        _ffn_second_body,
        grid=(n // FFN_ROW_TILE,),
        in_specs=[row, pl.BlockSpec((FFN_ROW_TILE, a.shape[1]), lambda i: (i, 0)), _resident(w_o.shape),
                  _resident((1, d)), _resident(f["g"].shape), _resident(f["u"].shape),
                  _resident(f["d"].shape), _resident((1, d))],
        out_specs=row,
        out_shape=jax.ShapeDtypeStruct((n, d), F32),
        compiler_params=_params("parallel"),
        name="ffn_second",
    )(x, a, w_o, g, f["g"], f["u"], f["d"], g_post)


def _rope_slot(t, gain, cs):
    a = t * gain * cs
    lane = lax.broadcasted_iota(jnp.int32, a.shape, 1)
    return jnp.where(lane < QK_ROPE_DIM, a + pltpu.roll(a, QK_ROPE_DIM, axis=1), 0.0)


def _sumsq_rope(t):
    lane = lax.broadcasted_iota(jnp.int32, t.shape, 1)
    return jnp.sum(jnp.where(lane < QK_ROPE_DIM, t * t, 0.0), axis=-1, keepdims=True)


def _mla_proj_body(h_ref, g_ref, wdq_ref, qa_ref, wuqt_ref, wdkv_ref, kva_ref, wk_ref, wvt_ref,
                   qgt_ref, kg_ref, cs_ref, cst_ref, qt_ref, k_ref, vt_ref):
    hn = _rms(h_ref[...], g_ref[...]).astype(BF16)
    cs = cs_ref[...]
    kg = kg_ref[...]
    inv_dim = 1.0 / QK_HEAD_DIM

    ckv_full = _mm(hn, wdkv_ref[...])
    ckv = _rms(ckv_full[:, :KV_LORA_RANK], kva_ref[...]).astype(BF16)
    vt = lax.dot_general(wvt_ref[...], ckv, _NT, preferred_element_type=F32).astype(BF16)
    for hd in range(MLA_HEADS):
        vt_ref[hd * V_SLOT:hd * V_SLOT + V_HEAD_DIM, :] = vt[hd * V_HEAD_DIM:(hd + 1) * V_HEAD_DIM, :]
        vt_ref[hd * V_SLOT + V_HEAD_DIM:(hd + 1) * V_SLOT, :] = jnp.ones((BF16_SUBLANES_V7X, vt.shape[1]), BF16)
    k_pe_raw = ckv_full[:, KV_LORA_RANK:]
    k_pe_ss = _sumsq_rope(k_pe_raw)
    k_pe = _rope_slot(k_pe_raw, kg[:, QK_NOPE_DIM:], cs)
    k_nope = _mm(ckv, wk_ref[...])
    for hd in range(MLA_HEADS):
        kn = k_nope[:, hd * QK_NOPE_DIM:(hd + 1) * QK_NOPE_DIM]
        r = lax.rsqrt((jnp.sum(kn * kn, axis=-1, keepdims=True) + k_pe_ss) * inv_dim + EPS)
        k_ref[:, hd * QK_SLOT:hd * QK_SLOT + QK_NOPE_DIM] = (kn * r * kg[:, :QK_NOPE_DIM]).astype(BF16)
        k_ref[:, hd * QK_SLOT + QK_NOPE_DIM:(hd + 1) * QK_SLOT] = (k_pe * r).astype(BF16)

    cq = _rms(_mm(hn, wdq_ref[...]), qa_ref[...]).astype(BF16)
    scale = QK_HEAD_DIM ** -0.5 * math.log2(math.e)
    cst = cst_ref[...]
    qgt = qgt_ref[...]
    for hd in range(MLA_HEADS):
        base = hd * QK_SLOT
        qt = lax.dot_general(wuqt_ref[base:base + QK_SLOT, :], cq, _NT, preferred_element_type=F32)
        qn, qp = qt[:QK_NOPE_DIM], qt[QK_NOPE_DIM:]
        pe = qp[:QK_ROPE_DIM]
        ss = jnp.sum(qn * qn, axis=0, keepdims=True) + jnp.sum(pe * pe, axis=0, keepdims=True)
        r = lax.rsqrt(ss * inv_dim + EPS) * scale
        a = qp * qgt[QK_NOPE_DIM:] * cst
        qt_ref[base:base + QK_NOPE_DIM, :] = (qn * r * qgt[:QK_NOPE_DIM]).astype(BF16)
        qt_ref[base + QK_NOPE_DIM:base + QK_HEAD_DIM, :] = (
            (a[:QK_ROPE_DIM] + a[QK_ROPE_DIM:]) * r).astype(BF16)
        qt_ref[base + QK_HEAD_DIM:base + QK_SLOT, :] = jnp.zeros((QK_SLOT - QK_HEAD_DIM, qt.shape[1]), BF16)


def _mla_proj(h, g, w, cs, batch, length):
    n, d = h.shape
    nblk = length // ROW_TILE
    slots = MLA_HEADS * QK_SLOT
    dv = MLA_HEADS * V_SLOT
    row = lambda width: pl.BlockSpec((ROW_TILE, width), lambda i: (i, 0))
    return pl.pallas_call(
        _mla_proj_body,
        grid=(n // ROW_TILE,),
        in_specs=[row(d), _resident((1, d)), _resident(w["dq"].shape), _resident((1, Q_LORA_RANK)),
                  _resident(w["uqt"].shape), _resident(w["dkv"].shape), _resident((1, KV_LORA_RANK)),
                  _resident(w["k"].shape), _resident(w["vt"].shape),
                  _resident((QK_SLOT, ROW_TILE)), _resident((1, QK_SLOT)),
                  pl.BlockSpec((ROW_TILE, LANES_V7X), lambda i: (i % nblk, 0)),
                  pl.BlockSpec((LANES_V7X, ROW_TILE), lambda i: (0, i % nblk))],
        out_specs=[pl.BlockSpec((None, slots, ROW_TILE), lambda i: (i // nblk, 0, i % nblk)),
                   row(slots),
                   pl.BlockSpec((None, dv, ROW_TILE), lambda i: (i // nblk, 0, i % nblk))],
        out_shape=[jax.ShapeDtypeStruct((batch, slots, length), BF16), jax.ShapeDtypeStruct((n, slots), BF16),
                   jax.ShapeDtypeStruct((batch, dv, length), BF16)],
        compiler_params=_params("parallel"),
        name="mla_proj",
    )(h, g, w["dq"], w["qa"], w["uqt"], w["dkv"], w["kva"], w["k"], w["vt"], w["qgt"], w["kg"], cs, cs.T)


def _attn_body(qt_ref, k_ref, vt_ref, o_ref, s_ref, smax_ref, m_ref, acc_ref):
    length = k_ref.shape[0]
    n_chunks = length // ATTN_TK
    total = (length // ATTN_TQ) * n_chunks
    trips_per_q = n_chunks // ATTN_BUFFERS

    def scores(g, slot):
        q_off = pl.multiple_of((g // n_chunks) * ATTN_TQ, ATTN_TQ)
        k_off = pl.multiple_of(lax.rem(g, n_chunks) * ATTN_TK, ATTN_TK)
        st = _mm(k_ref[pl.ds(k_off, ATTN_TK), :], qt_ref[:, pl.ds(q_off, ATTN_TQ)])
        s_ref[slot] = st
        smax_ref[slot] = jnp.max(st, axis=0, keepdims=True)

    def update(j, slot):
        k_off = pl.multiple_of(j * ATTN_TK, ATTN_TK)
        m_old = m_ref[...]
        m_new = jnp.maximum(m_old, smax_ref[slot])
        p = jnp.exp2(s_ref[slot] - m_new).astype(BF16)
        alpha = jnp.exp2(m_old - m_new)
        acc_ref[...] = alpha * acc_ref[...] + _mm(vt_ref[:, pl.ds(k_off, ATTN_TK)], p)
        m_ref[...] = m_new

    for g in range(ATTN_LOOKAHEAD):
        scores(g, g)

    def trip(t, carry):
        jt = lax.rem(t, trips_per_q)

        @pl.when(jt == 0)
        def _start_q_tile():
            m_ref[...] = jnp.full(m_ref.shape, -jnp.inf, F32)
            acc_ref[...] = jnp.zeros(acc_ref.shape, F32)

        for i in range(ATTN_BUFFERS):
            ahead = lax.rem(t * ATTN_BUFFERS + i + ATTN_LOOKAHEAD, total)
            scores(ahead, (i + ATTN_LOOKAHEAD) % ATTN_BUFFERS)
            update(jt * ATTN_BUFFERS + i, i)

        @pl.when(jt == trips_per_q - 1)
        def _finish_q_tile():
            q_off = pl.multiple_of((t // trips_per_q) * ATTN_TQ, ATTN_TQ)
            o = acc_ref[:V_HEAD_DIM, :] / acc_ref[V_HEAD_DIM:V_HEAD_DIM + 1, :]
            o_ref[pl.ds(q_off, ATTN_TQ), :] = o.T.astype(BF16)

        return carry

    lax.fori_loop(0, total // ATTN_BUFFERS, trip, 0)


def _attention(qt, k, vt, batch, length):
    assert length % ATTN_TQ == 0 and length % (ATTN_TK * ATTN_BUFFERS) == 0
    k = k.reshape(batch, length, MLA_HEADS * QK_SLOT)
    out = pl.pallas_call(
        _attn_body,
        grid=(batch, MLA_HEADS),
        in_specs=[pl.BlockSpec((None, QK_SLOT, length), lambda b, h: (b, h, 0)),
                  pl.BlockSpec((None, length, QK_SLOT), lambda b, h: (b, 0, h)),
                  pl.BlockSpec((None, V_SLOT, length), lambda b, h: (b, h, 0))],
        out_specs=pl.BlockSpec((None, length, V_HEAD_DIM), lambda b, h: (b, 0, h)),
        out_shape=jax.ShapeDtypeStruct((batch, length, MLA_HEADS * V_HEAD_DIM), BF16),
        scratch_shapes=[pltpu.VMEM((ATTN_BUFFERS, ATTN_TK, ATTN_TQ), F32),
                        pltpu.VMEM((ATTN_BUFFERS, 1, ATTN_TQ), F32), pltpu.VMEM((1, ATTN_TQ), F32),
                        pltpu.VMEM((V_SLOT, ATTN_TQ), F32)],
        compiler_params=_params("parallel", "parallel"),
        name="mla_attn",
    )(qt, k, vt)
    return out.reshape(batch * length, MLA_HEADS * V_HEAD_DIM)


def _ret_proj_body(h_ref, g_ref, w_ref, gn_ref, cos_ref, sin_ref, q_ref, k_ref, v_ref, gate_ref):
    hn = _rms(h_ref[...], g_ref[...]).astype(BF16)
    cos = cos_ref[...]
    sin = sin_ref[...]
    hq = RET_HEADS * RET_DK
    hv = RET_HEADS * RET_DV
    half = RET_DK // 2

    def rope_store(dst, col0, mult):
        for hd in range(RET_HEADS):
            x = _mm(hn, w_ref[:, col0 + hd * RET_DK:col0 + (hd + 1) * RET_DK])
            x1, x2 = x[:, :half], x[:, half:]
            dst[:, hd * RET_DK:hd * RET_DK + half] = ((x1 * cos - x2 * sin) * mult).astype(BF16)
            dst[:, hd * RET_DK + half:(hd + 1) * RET_DK] = ((x2 * cos + x1 * sin) * mult).astype(BF16)

    rope_store(q_ref, 0, 1.0)
    rope_store(k_ref, hq, RET_DK ** -0.5)
    v_ref[...] = _mm(hn, w_ref[:, 2 * hq:2 * hq + hv]).astype(BF16)
    gate = _mm(hn, w_ref[:, 2 * hq + hv:])
    gate_ref[...] = (gate * jax.nn.sigmoid(gate) * gn_ref[...]).astype(BF16)


def _ret_proj(h, g, w_in, gn_g, cos, sin, length):
    n, d = h.shape
    nblk = length // ROW_TILE
    hq = RET_HEADS * RET_DK
    hv = RET_HEADS * RET_DV
    row = lambda width: pl.BlockSpec((ROW_TILE, width), lambda i: (i, 0))
    pos = pl.BlockSpec((ROW_TILE, RET_DK // 2), lambda i: (i % nblk, 0))
    return pl.pallas_call(
        _ret_proj_body,
        grid=(n // ROW_TILE,),
        in_specs=[row(d), _resident((1, d)), _resident(w_in.shape), _resident((1, hv)), pos, pos],
        out_specs=[row(hq), row(hq), row(hv), row(hv)],
        out_shape=[jax.ShapeDtypeStruct((n, hq), BF16), jax.ShapeDtypeStruct((n, hq), BF16),
                   jax.ShapeDtypeStruct((n, hv), BF16), jax.ShapeDtypeStruct((n, hv), BF16)],
        compiler_params=_params("parallel"),
        name="ret_proj",
    )(h, g, w_in, gn_g, cos, sin)


_BWD, _FWD = 0, 1


def _ret_scan_body(lg_ref, q_ref, k_ref, v_ref, gate_ref, o_ref,
                   state_ref, sb_ref, o_scr, dmat_ref, xi_ref, zeta_ref, gc_ref):
    hd = pl.program_id(1)
    direction = pl.program_id(2)
    blk = pl.program_id(3)
    nblk = pl.num_programs(3)
    c = RET_CHUNK
    n_chunks = RET_BLOCK // c

    @pl.when(blk == 0)
    def _reset_state():
        state_ref[...] = jnp.zeros(state_ref.shape, F32)

    @pl.when((blk == 0) & (direction == _BWD))
    def _decay_tables():
        lg_b = lg_ref[_BWD, hd]
        lg_f = lg_ref[_FWD, hd]
        row = lax.broadcasted_iota(jnp.int32, (c, c), 0)
        col = lax.broadcasted_iota(jnp.int32, (c, c), 1)
        dmat_ref[...] = jnp.where(row >= col,
                                  jnp.exp(lg_f * jnp.maximum(row - col, 0).astype(F32)),
                                  jnp.exp(lg_b * jnp.maximum(col - row, 0).astype(F32)))
        qi = lax.broadcasted_iota(jnp.int32, xi_ref.shape[1:], 0)
        xi_ref[_FWD] = jnp.exp(lg_f * (qi + 1).astype(F32))
        xi_ref[_BWD] = jnp.exp(lg_b * (c - qi).astype(F32))
        kj = lax.broadcasted_iota(jnp.int32, zeta_ref.shape[1:], 0)
        zeta_ref[_FWD] = jnp.exp(lg_f * (c - 1 - kj).astype(F32))
        zeta_ref[_BWD] = jnp.exp(lg_b * kj.astype(F32))
        gc_ref[_FWD] = jnp.exp(jnp.zeros(gc_ref.shape[1:], F32) + lg_f * c)
        gc_ref[_BWD] = jnp.exp(jnp.zeros(gc_ref.shape[1:], F32) + lg_b * c)

    def absorb(ci, d):
        rows = pl.ds(ci * c, c)
        kz_t = (k_ref[rows, :].astype(F32) * zeta_ref[d]).T.astype(BF16)
        state_ref[...] = state_ref[...] * gc_ref[d] + _mm(kz_t, v_ref[rows, :])

    @pl.when(direction == _BWD)
    def _backward_states():
        first = (nblk - 1 - blk) * n_chunks
        for ci in reversed(range(n_chunks)):
            sb_ref[first + ci] = state_ref[...].astype(BF16)
            absorb(ci, _BWD)

    @pl.when(direction == _FWD)
    def _forward():
        first = blk * n_chunks
        for ci in range(n_chunks):
            rows = pl.ds(ci * c, c)
            q = q_ref[rows, :]
            s = lax.dot_general(q, k_ref[rows, :], _NT, preferred_element_type=F32) * dmat_ref[...]
            o_scr[rows, :] = (_mm(s.astype(BF16), v_ref[rows, :])
                              + _mm(q, state_ref[...].astype(BF16)) * xi_ref[_FWD]
                              + _mm(q, sb_ref[first + ci]) * xi_ref[_BWD])
            absorb(ci, _FWD)
            for r in range(ci * c, (ci + 1) * c, RET_NORM_ROWS):
                o = o_scr[pl.ds(r, RET_NORM_ROWS), :]
                dev = o - jnp.mean(o, axis=-1, keepdims=True)
                on = dev * lax.rsqrt(jnp.mean(dev * dev, axis=-1, keepdims=True) + EPS)
                gate = gate_ref[pl.ds(r, RET_NORM_ROWS), :].astype(F32)
                o_ref[pl.ds(r, RET_NORM_ROWS), :] = (gate * on).astype(BF16)


def _ret_scan(q, k, v, gate, lg, batch, length):
    hq = RET_HEADS * RET_DK
    hv = RET_HEADS * RET_DV
    assert length % RET_BLOCK == 0
    nblk = length // RET_BLOCK
    q = q.reshape(batch, length, hq)
    k = k.reshape(batch, length, hq)
    v = v.reshape(batch, length, hv)
    gate = gate.reshape(batch, length, hv)

    def sweep(b, h, d, i):
        return (b, d * i + (1 - d) * (nblk - 1 - i), h)

    def fwd_only(b, h, d, i):
        return (b, d * i, h)

    out = pl.pallas_call(
        _ret_scan_body,
        grid=(batch, RET_HEADS, 2, nblk),
        in_specs=[pl.BlockSpec(memory_space=pltpu.SMEM),
                  pl.BlockSpec((None, RET_BLOCK, RET_DK), fwd_only),
                  pl.BlockSpec((None, RET_BLOCK, RET_DK), sweep),
                  pl.BlockSpec((None, RET_BLOCK, RET_DV), sweep),
                  pl.BlockSpec((None, RET_BLOCK, RET_DV), fwd_only)],
        out_specs=pl.BlockSpec((None, RET_BLOCK, RET_DV), fwd_only),
        out_shape=jax.ShapeDtypeStruct((batch, length, hv), BF16),
        scratch_shapes=[pltpu.VMEM((RET_DK, RET_DV), F32),
                        pltpu.VMEM((length // RET_CHUNK, RET_DK, RET_DV), BF16),
                        pltpu.VMEM((RET_BLOCK, RET_DV), F32),
                        pltpu.VMEM((RET_CHUNK, RET_CHUNK), F32),
                        pltpu.VMEM((2, RET_CHUNK, RET_DV), F32),
                        pltpu.VMEM((2, RET_CHUNK, RET_DK), F32),
                        pltpu.VMEM((2, 1, RET_DV), F32)],
        compiler_params=_params("parallel", "parallel", "arbitrary", "arbitrary"),
        name="ret_scan",
    )(lg, q, k, v, gate)
    return out.reshape(batch * length, hv)


def _swap_halves(a):
    half = a.shape[-1] // 2
    return jnp.concatenate([a[..., half:], a[..., :half]], axis=-1)


def _rope_slot_layout(a):
    pe = a[..., QK_NOPE_DIM:]
    return jnp.concatenate([a[..., :QK_NOPE_DIM], pe, _swap_halves(pe)], axis=-1)


def _mla_weights(w_dq, q_a_norm, w_uq, w_dkv, kv_a_norm, w_ukv, q_norm, k_norm):
    uq = w_uq.reshape(Q_LORA_RANK, MLA_HEADS, QK_HEAD_DIM)
    uq = _rope_slot_layout(uq).reshape(Q_LORA_RANK, MLA_HEADS * QK_SLOT)
    pe = w_dkv[:, KV_LORA_RANK:]
    dkv = jnp.concatenate([w_dkv[:, :KV_LORA_RANK], pe, _swap_halves(pe)], axis=-1)
    ukv = w_ukv.reshape(KV_LORA_RANK, MLA_HEADS, QK_NOPE_DIM + V_HEAD_DIM)
    wk = ukv[..., :QK_NOPE_DIM].reshape(KV_LORA_RANK, MLA_HEADS * QK_NOPE_DIM)
    wvt = ukv[..., QK_NOPE_DIM:].reshape(KV_LORA_RANK, MLA_HEADS * V_HEAD_DIM).T
    return {
        "dq": w_dq.astype(BF16), "qa": q_a_norm[None, :], "uqt": uq.T.astype(BF16),
        "dkv": dkv.astype(BF16), "kva": kv_a_norm[None, :],
        "k": wk.astype(BF16), "vt": wvt.astype(BF16),
        "qgt": jnp.broadcast_to(_rope_slot_layout(q_norm)[:, None], (QK_SLOT, ROW_TILE)),
        "kg": _rope_slot_layout(k_norm)[None, :],
    }


def _rope_tables(length, dim):
    inv = ROPE_BASE ** (-jnp.arange(0, dim, 2, dtype=F32) / dim)
    ang = jnp.arange(length, dtype=F32)[:, None] * inv[None, :]
    return jnp.cos(ang), jnp.sin(ang)


def _trunk(x, p):
    batch, length, d = x.shape
    assert d == D_MODEL and length % ROW_TILE == 0 and length % FFN_ROW_TILE == 0
    x = x.reshape(batch * length, d)
    cos_m, sin_m = _rope_tables(length, QK_ROPE_DIM)
    cs_mla = jnp.concatenate([cos_m, cos_m, -sin_m, sin_m], axis=-1)
    cos_r, sin_r = _rope_tables(length, RET_DK)
    for i in range(DEPTH):
        j = i // 2
        ng = p["norm_g"][i]
        f = p["ffn"][i]
        h = _ffn_first(x, ng[0:1], f[0])
        if i % 2 == 0:
            w = p["mla"][j]
            qt, k, vt = _mla_proj(h, ng[1:2], w, cs_mla, batch, length)
            o = _attention(qt, k, vt, batch, length)
        else:
            w = p["ret"][j]
            q, k, v, gate = _ret_proj(h, ng[1:2], w["in"], w["gn"], cos_r, sin_r, length)
            o = _ret_scan(q, k, v, gate, w["lg"], batch, length)
        x = _ffn_second(h, o, w["o"], ng[2:3], f[1], ng[3:4])
    return x.reshape(batch, length, d)


def kernel(x_prompt, x_sample, norm_g, ffn_w_gate, ffn_w_up, ffn_w_down, mla_w_dq, mla_q_a_norm, mla_w_uq, mla_w_dkv, mla_kv_a_norm, mla_w_ukv, mla_q_norm, mla_k_norm, mla_w_o, ret_w_in, ret_decay_fwd, ret_decay_bwd, ret_gn_g, ret_w_o):
    p = {
        "norm_g": norm_g,
        "ffn": [[{"g": ffn_w_gate[i, s].astype(BF16), "u": ffn_w_up[i, s].astype(BF16),
                  "d": ffn_w_down[i, s].astype(BF16)} for s in range(2)] for i in range(DEPTH)],
        "mla": [], "ret": [],
    }
    for j in range(mla_w_dq.shape[0]):
        w = _mla_weights(mla_w_dq[j], mla_q_a_norm[j], mla_w_uq[j], mla_w_dkv[j], mla_kv_a_norm[j],
                         mla_w_ukv[j], mla_q_norm[j], mla_k_norm[j])
        w["o"] = mla_w_o[j].astype(BF16)
        p["mla"].append(w)
    for j in range(ret_w_in.shape[0]):
        lg = jnp.stack([jnp.log1p(-jnp.exp(ret_decay_bwd[j].astype(F32))),
                        jnp.log1p(-jnp.exp(ret_decay_fwd[j].astype(F32)))])
        p["ret"].append({"in": ret_w_in[j].astype(BF16), "lg": lg, "gn": ret_gn_g[j][None, :],
                         "o": ret_w_o[j].astype(BF16)})
    return _trunk(x_prompt, p), _trunk(x_sample, p)
```

```python
import math

import jax
import jax.numpy as jnp
from jax import lax
from jax.experimental import pallas as pl
from jax.experimental.pallas import tpu as pltpu

F32 = jnp.float32
BF16 = jnp.bfloat16

D_MODEL = 1024
DEPTH = 4
D_FF = 2816
MLA_HEADS = 8
QK_NOPE_DIM = 128
QK_ROPE_DIM = 64
QK_HEAD_DIM = QK_NOPE_DIM + QK_ROPE_DIM
V_HEAD_DIM = 128
Q_LORA_RANK = 384
KV_LORA_RANK = 256
ROPE_BASE = 10000.0
RET_HEADS = 4
RET_DK = D_MODEL // RET_HEADS
RET_DV = 2 * D_MODEL // RET_HEADS
EPS = 1e-6

LANES_V7X = 128
MXU_DIM_V7X = 256
VMEM_LIMIT_BYTES_V7X = 56 * 1024 * 1024

QK_SLOT = QK_NOPE_DIM + 2 * QK_ROPE_DIM
assert QK_SLOT == MXU_DIM_V7X
BF16_SUBLANES_V7X = 16
V_SLOT = V_HEAD_DIM + BF16_SUBLANES_V7X

ROW_TILE = 512
FFN_ROW_TILE = 512
ATTN_TQ = 512
ATTN_TK = 512
ATTN_LOOKAHEAD = 2
ATTN_BUFFERS = 4
RET_CHUNK = MXU_DIM_V7X
RET_BLOCK = 1024
RET_NORM_ROWS = 64

_NT = (((1,), (1,)), ((), ()))


def _params(*sem):
    return pltpu.CompilerParams(dimension_semantics=sem, vmem_limit_bytes=VMEM_LIMIT_BYTES_V7X)


def _resident(shape):
    return pl.BlockSpec(shape, lambda *_: (0,) * len(shape), pipeline_mode=pl.Buffered(1))


def _rms(x, g):
    return x * lax.rsqrt(jnp.mean(x * x, axis=-1, keepdims=True) + EPS) * g


def _mm(a, b):
    return jnp.dot(a, b, preferred_element_type=F32)


def _swiglu_half(x, g_ref, wg_ref, wu_ref, wd_ref):
    xn = _rms(x, g_ref[...]).astype(BF16)
    gate = _mm(xn, wg_ref[...])
    up = _mm(xn, wu_ref[...])
    h = (gate * jax.nn.sigmoid(gate) * up).astype(BF16)
    return x + 0.5 * _mm(h, wd_ref[...])


def _ffn_first_body(x_ref, g_ref, wg_ref, wu_ref, wd_ref, o_ref):
    o_ref[...] = _swiglu_half(x_ref[...], g_ref, wg_ref, wu_ref, wd_ref)


def _ffn_second_body(x_ref, a_ref, wo_ref, g_ref, wg_ref, wu_ref, wd_ref, gp_ref, o_ref):
    h = x_ref[...] + _mm(a_ref[...], wo_ref[...])
    o_ref[...] = _rms(_swiglu_half(h, g_ref, wg_ref, wu_ref, wd_ref), gp_ref[...])


def _ffn_first(x, g, f):
    n, d = x.shape
    row = pl.BlockSpec((FFN_ROW_TILE, d), lambda i: (i, 0))
    return pl.pallas_call(
        _ffn_first_body,
        grid=(n // FFN_ROW_TILE,),
        in_specs=[row, _resident((1, d)), _resident(f["g"].shape), _resident(f["u"].shape),
                  _resident(f["d"].shape)],
        out_specs=row,
        out_shape=jax.ShapeDtypeStruct((n, d), F32),
        compiler_params=_params("parallel"),
        name="ffn_first",
    )(x, g, f["g"], f["u"], f["d"])


def _ffn_second(x, a, w_o, g, f, g_post):
    n, d = x.shape
    row = pl.BlockSpec((FFN_ROW_TILE, d), lambda i: (i, 0))
    return pl.pallas_call(
        _ffn_second_body,
        grid=(n // FFN_ROW_TILE,),
        in_specs=[row, pl.BlockSpec((FFN_ROW_TILE, a.shape[1]), lambda i: (i, 0)), _resident(w_o.shape),
                  _resident((1, d)), _resident(f["g"].shape), _resident(f["u"].shape),
                  _resident(f["d"].shape), _resident((1, d))],
        out_specs=row,
        out_shape=jax.ShapeDtypeStruct((n, d), F32),
        compiler_params=_params("parallel"),
        name="ffn_second",
    )(x, a, w_o, g, f["g"], f["u"], f["d"], g_post)


def _rope_slot(t, gain, cs):
    a = t * gain * cs
    lane = lax.broadcasted_iota(jnp.int32, a.shape, 1)
    return jnp.where(lane < QK_ROPE_DIM, a + pltpu.roll(a, QK_ROPE_DIM, axis=1), 0.0)


def _sumsq_rope(t):
    lane = lax.broadcasted_iota(jnp.int32, t.shape, 1)
    return jnp.sum(jnp.where(lane < QK_ROPE_DIM, t * t, 0.0), axis=-1, keepdims=True)


def _mla_proj_body(h_ref, g_ref, wdq_ref, qa_ref, wuqt_ref, wdkv_ref, kva_ref, wk_ref, wvt_ref,
                   qgt_ref, kg_ref, cs_ref, cst_ref, qt_ref, k_ref, vt_ref):
    hn = _rms(h_ref[...], g_ref[...]).astype(BF16)
    cs = cs_ref[...]
    kg = kg_ref[...]
    inv_dim = 1.0 / QK_HEAD_DIM

    ckv_full = _mm(hn, wdkv_ref[...])
    ckv = _rms(ckv_full[:, :KV_LORA_RANK], kva_ref[...]).astype(BF16)
    vt = lax.dot_general(wvt_ref[...], ckv, _NT, preferred_element_type=F32).astype(BF16)
    for hd in range(MLA_HEADS):
        vt_ref[hd * V_SLOT:hd * V_SLOT + V_HEAD_DIM, :] = vt[hd * V_HEAD_DIM:(hd + 1) * V_HEAD_DIM, :]
        vt_ref[hd * V_SLOT + V_HEAD_DIM:(hd + 1) * V_SLOT, :] = jnp.ones((BF16_SUBLANES_V7X, vt.shape[1]), BF16)
    k_pe_raw = ckv_full[:, KV_LORA_RANK:]
    k_pe_ss = _sumsq_rope(k_pe_raw)
    k_pe = _rope_slot(k_pe_raw, kg[:, QK_NOPE_DIM:], cs)
    k_nope = _mm(ckv, wk_ref[...])
    for hd in range(MLA_HEADS):
        kn = k_nope[:, hd * QK_NOPE_DIM:(hd + 1) * QK_NOPE_DIM]
        r = lax.rsqrt((jnp.sum(kn * kn, axis=-1, keepdims=True) + k_pe_ss) * inv_dim + EPS)
        k_ref[:, hd * QK_SLOT:hd * QK_SLOT + QK_NOPE_DIM] = (kn * r * kg[:, :QK_NOPE_DIM]).astype(BF16)
        k_ref[:, hd * QK_SLOT + QK_NOPE_DIM:(hd + 1) * QK_SLOT] = (k_pe * r).astype(BF16)

    cq = _rms(_mm(hn, wdq_ref[...]), qa_ref[...]).astype(BF16)
    scale = QK_HEAD_DIM ** -0.5 * math.log2(math.e)
    cst = cst_ref[...]
    qgt = qgt_ref[...]
    for hd in range(MLA_HEADS):
        base = hd * QK_SLOT
        qt = lax.dot_general(wuqt_ref[base:base + QK_SLOT, :], cq, _NT, preferred_element_type=F32)
        qn, qp = qt[:QK_NOPE_DIM], qt[QK_NOPE_DIM:]
        pe = qp[:QK_ROPE_DIM]
        ss = jnp.sum(qn * qn, axis=0, keepdims=True) + jnp.sum(pe * pe, axis=0, keepdims=True)
        r = lax.rsqrt(ss * inv_dim + EPS) * scale
        a = qp * qgt[QK_NOPE_DIM:] * cst
        qt_ref[base:base + QK_NOPE_DIM, :] = (qn * r * qgt[:QK_NOPE_DIM]).astype(BF16)
        qt_ref[base + QK_NOPE_DIM:base + QK_HEAD_DIM, :] = (
            (a[:QK_ROPE_DIM] + a[QK_ROPE_DIM:]) * r).astype(BF16)
        qt_ref[base + QK_HEAD_DIM:base + QK_SLOT, :] = jnp.zeros((QK_SLOT - QK_HEAD_DIM, qt.shape[1]), BF16)


def _mla_proj(h, g, w, cs, batch, length):
    n, d = h.shape
    nblk = length // ROW_TILE
    slots = MLA_HEADS * QK_SLOT
    dv = MLA_HEADS * V_SLOT
    row = lambda width: pl.BlockSpec((ROW_TILE, width), lambda i: (i, 0))
    return pl.pallas_call(
        _mla_proj_body,
        grid=(n // ROW_TILE,),
        in_specs=[row(d), _resident((1, d)), _resident(w["dq"].shape), _resident((1, Q_LORA_RANK)),
                  _resident(w["uqt"].shape), _resident(w["dkv"].shape), _resident((1, KV_LORA_RANK)),
                  _resident(w["k"].shape), _resident(w["vt"].shape),
                  _resident((QK_SLOT, ROW_TILE)), _resident((1, QK_SLOT)),
                  pl.BlockSpec((ROW_TILE, LANES_V7X), lambda i: (i % nblk, 0)),
                  pl.BlockSpec((LANES_V7X, ROW_TILE), lambda i: (0, i % nblk))],
        out_specs=[pl.BlockSpec((None, slots, ROW_TILE), lambda i: (i // nblk, 0, i % nblk)),
                   row(slots),
                   pl.BlockSpec((None, dv, ROW_TILE), lambda i: (i // nblk, 0, i % nblk))],
        out_shape=[jax.ShapeDtypeStruct((batch, slots, length), BF16), jax.ShapeDtypeStruct((n, slots), BF16),
                   jax.ShapeDtypeStruct((batch, dv, length), BF16)],
        compiler_params=_params("parallel"),
        name="mla_proj",
    )(h, g, w["dq"], w["qa"], w["uqt"], w["dkv"], w["kva"], w["k"], w["vt"], w["qgt"], w["kg"], cs, cs.T)


def _attn_body(qt_ref, k_ref, vt_ref, o_ref, s_ref, smax_ref, m_ref, acc_ref):
    length = k_ref.shape[0]
    n_chunks = length // ATTN_TK
    n_q = length // ATTN_TQ

    def scores(q_tile, j, slot):
        q_off = pl.multiple_of(q_tile * ATTN_TQ, ATTN_TQ)
        st = _mm(k_ref[j * ATTN_TK:(j + 1) * ATTN_TK, :], qt_ref[:, pl.ds(q_off, ATTN_TQ)])
        s_ref[slot] = st
        smax_ref[slot] = jnp.max(st, axis=0, keepdims=True)

    def update(j, slot):
        m_old = m_ref[...]
        m_new = jnp.maximum(m_old, smax_ref[slot])
        p = jnp.exp2(s_ref[slot] - m_new).astype(BF16)
        alpha = jnp.exp2(m_old - m_new)
        acc_ref[...] = alpha * acc_ref[...] + _mm(vt_ref[:, j * ATTN_TK:(j + 1) * ATTN_TK], p)
        m_ref[...] = m_new

    for j in range(ATTN_LOOKAHEAD):
        scores(0, j, j)

    def q_tile_trip(t, carry):
        m_ref[...] = jnp.full(m_ref.shape, -jnp.inf, F32)
        acc_ref[...] = jnp.zeros(acc_ref.shape, F32)
        next_t = lax.rem(t + 1, n_q)
        for j in range(n_chunks):
            ahead = j + ATTN_LOOKAHEAD
            scores(t if ahead < n_chunks else next_t, ahead % n_chunks, ahead % ATTN_BUFFERS)
            update(j, j % ATTN_BUFFERS)
        o = acc_ref[:V_HEAD_DIM, :] / acc_ref[V_HEAD_DIM:V_HEAD_DIM + 1, :]
        o_ref[pl.ds(pl.multiple_of(t * ATTN_TQ, ATTN_TQ), ATTN_TQ), :] = o.T.astype(BF16)
        return carry

    lax.fori_loop(0, n_q, q_tile_trip, 0)


def _attention(qt, k, vt, batch, length):
    assert length % ATTN_TQ == 0 and length % (ATTN_TK * ATTN_BUFFERS) == 0
    k = k.reshape(batch, length, MLA_HEADS * QK_SLOT)
    out = pl.pallas_call(
        _attn_body,
        grid=(batch, MLA_HEADS),
        in_specs=[pl.BlockSpec((None, QK_SLOT, length), lambda b, h: (b, h, 0)),
                  pl.BlockSpec((None, length, QK_SLOT), lambda b, h: (b, 0, h)),
                  pl.BlockSpec((None, V_SLOT, length), lambda b, h: (b, h, 0))],
        out_specs=pl.BlockSpec((None, length, V_HEAD_DIM), lambda b, h: (b, 0, h)),
        out_shape=jax.ShapeDtypeStruct((batch, length, MLA_HEADS * V_HEAD_DIM), BF16),
        scratch_shapes=[pltpu.VMEM((ATTN_BUFFERS, ATTN_TK, ATTN_TQ), F32),
                        pltpu.VMEM((ATTN_BUFFERS, 1, ATTN_TQ), F32), pltpu.VMEM((1, ATTN_TQ), F32),
                        pltpu.VMEM((V_SLOT, ATTN_TQ), F32)],
        compiler_params=_params("parallel", "parallel"),
        name="mla_attn",
    )(qt, k, vt)
    return out.reshape(batch * length, MLA_HEADS * V_HEAD_DIM)


def _ret_proj_body(h_ref, g_ref, w_ref, gn_ref, cos_ref, sin_ref, q_ref, k_ref, v_ref, gate_ref):
    hn = _rms(h_ref[...], g_ref[...]).astype(BF16)
    cos = cos_ref[...]
    sin = sin_ref[...]
    hq = RET_HEADS * RET_DK
    hv = RET_HEADS * RET_DV
    half = RET_DK // 2

    def rope_store(dst, col0, mult):
        for hd in range(RET_HEADS):
            x = _mm(hn, w_ref[:, col0 + hd * RET_DK:col0 + (hd + 1) * RET_DK])
            x1, x2 = x[:, :half], x[:, half:]
            dst[:, hd * RET_DK:hd * RET_DK + half] = ((x1 * cos - x2 * sin) * mult).astype(BF16)
            dst[:, hd * RET_DK + half:(hd + 1) * RET_DK] = ((x2 * cos + x1 * sin) * mult).astype(BF16)

    rope_store(q_ref, 0, 1.0)
    rope_store(k_ref, hq, RET_DK ** -0.5)
    v_ref[...] = _mm(hn, w_ref[:, 2 * hq:2 * hq + hv]).astype(BF16)
    gate = _mm(hn, w_ref[:, 2 * hq + hv:])
    gate_ref[...] = (gate * jax.nn.sigmoid(gate) * gn_ref[...]).astype(BF16)


def _ret_proj(h, g, w_in, gn_g, cos, sin, length):
    n, d = h.shape
    nblk = length // ROW_TILE
    hq = RET_HEADS * RET_DK
    hv = RET_HEADS * RET_DV
    row = lambda width: pl.BlockSpec((ROW_TILE, width), lambda i: (i, 0))
    pos = pl.BlockSpec((ROW_TILE, RET_DK // 2), lambda i: (i % nblk, 0))
    return pl.pallas_call(
        _ret_proj_body,
        grid=(n // ROW_TILE,),
        in_specs=[row(d), _resident((1, d)), _resident(w_in.shape), _resident((1, hv)), pos, pos],
        out_specs=[row(hq), row(hq), row(hv), row(hv)],
        out_shape=[jax.ShapeDtypeStruct((n, hq), BF16), jax.ShapeDtypeStruct((n, hq), BF16),
                   jax.ShapeDtypeStruct((n, hv), BF16), jax.ShapeDtypeStruct((n, hv), BF16)],
        compiler_params=_params("parallel"),
        name="ret_proj",
    )(h, g, w_in, gn_g, cos, sin)


_BWD, _FWD = 0, 1


def _ret_scan_body(lg_ref, q_ref, k_ref, v_ref, gate_ref, o_ref,
                   state_ref, sb_ref, o_scr, dmat_ref, xi_ref, zeta_ref, gc_ref):
    hd = pl.program_id(1)
    direction = pl.program_id(2)
    blk = pl.program_id(3)
    nblk = pl.num_programs(3)
    c = RET_CHUNK
    n_chunks = RET_BLOCK // c

    @pl.when(blk == 0)
    def _reset_state():
        state_ref[...] = jnp.zeros(state_ref.shape, F32)

    @pl.when((blk == 0) & (direction == _BWD))
    def _decay_tables():
        lg_b = lg_ref[_BWD, hd]
        lg_f = lg_ref[_FWD, hd]
        row = lax.broadcasted_iota(jnp.int32, (c, c), 0)
        col = lax.broadcasted_iota(jnp.int32, (c, c), 1)
        dmat_ref[...] = jnp.where(row >= col,
                                  jnp.exp(lg_f * jnp.maximum(row - col, 0).astype(F32)),
                                  jnp.exp(lg_b * jnp.maximum(col - row, 0).astype(F32)))
        qi = lax.broadcasted_iota(jnp.int32, xi_ref.shape[1:], 0)
        xi_ref[_FWD] = jnp.exp(lg_f * (qi + 1).astype(F32))
        xi_ref[_BWD] = jnp.exp(lg_b * (c - qi).astype(F32))
        kj = lax.broadcasted_iota(jnp.int32, zeta_ref.shape[1:], 0)
        zeta_ref[_FWD] = jnp.exp(lg_f * (c - 1 - kj).astype(F32))
        zeta_ref[_BWD] = jnp.exp(lg_b * kj.astype(F32))
        gc_ref[_FWD] = jnp.exp(jnp.zeros(gc_ref.shape[1:], F32) + lg_f * c)
        gc_ref[_BWD] = jnp.exp(jnp.zeros(gc_ref.shape[1:], F32) + lg_b * c)

    def absorb(ci, d):
        rows = pl.ds(ci * c, c)
        kz_t = (k_ref[rows, :].astype(F32) * zeta_ref[d]).T.astype(BF16)
        state_ref[...] = state_ref[...] * gc_ref[d] + _mm(kz_t, v_ref[rows, :])

    @pl.when(direction == _BWD)
    def _backward_states():
        first = (nblk - 1 - blk) * n_chunks
        for ci in reversed(range(n_chunks)):
            sb_ref[first + ci] = state_ref[...].astype(BF16)
            absorb(ci, _BWD)

    @pl.when(direction == _FWD)
    def _forward():
        first = blk * n_chunks
        for ci in range(n_chunks):
            rows = pl.ds(ci * c, c)
            q = q_ref[rows, :]
            s = lax.dot_general(q, k_ref[rows, :], _NT, preferred_element_type=F32) * dmat_ref[...]
            o_scr[rows, :] = (_mm(s.astype(BF16), v_ref[rows, :])
                              + _mm(q, state_ref[...].astype(BF16)) * xi_ref[_FWD]
                              + _mm(q, sb_ref[first + ci]) * xi_ref[_BWD])
            absorb(ci, _FWD)
            for r in range(ci * c, (ci + 1) * c, RET_NORM_ROWS):
                o = o_scr[pl.ds(r, RET_NORM_ROWS), :]
                dev = o - jnp.mean(o, axis=-1, keepdims=True)
                on = dev * lax.rsqrt(jnp.mean(dev * dev, axis=-1, keepdims=True) + EPS)
                gate = gate_ref[pl.ds(r, RET_NORM_ROWS), :].astype(F32)
                o_ref[pl.ds(r, RET_NORM_ROWS), :] = (gate * on).astype(BF16)


def _ret_scan(q, k, v, gate, lg, batch, length):
    hq = RET_HEADS * RET_DK
    hv = RET_HEADS * RET_DV
    assert length % RET_BLOCK == 0
    nblk = length // RET_BLOCK
    q = q.reshape(batch, length, hq)
    k = k.reshape(batch, length, hq)
    v = v.reshape(batch, length, hv)
    gate = gate.reshape(batch, length, hv)

    def sweep(b, h, d, i):
        return (b, d * i + (1 - d) * (nblk - 1 - i), h)

    def fwd_only(b, h, d, i):
        return (b, d * i, h)

    out = pl.pallas_call(
        _ret_scan_body,
        grid=(batch, RET_HEADS, 2, nblk),
        in_specs=[pl.BlockSpec(memory_space=pltpu.SMEM),
                  pl.BlockSpec((None, RET_BLOCK, RET_DK), fwd_only),
                  pl.BlockSpec((None, RET_BLOCK, RET_DK), sweep),
                  pl.BlockSpec((None, RET_BLOCK, RET_DV), sweep),
                  pl.BlockSpec((None, RET_BLOCK, RET_DV), fwd_only)],
        out_specs=pl.BlockSpec((None, RET_BLOCK, RET_DV), fwd_only),
        out_shape=jax.ShapeDtypeStruct((batch, length, hv), BF16),
        scratch_shapes=[pltpu.VMEM((RET_DK, RET_DV), F32),
                        pltpu.VMEM((length // RET_CHUNK, RET_DK, RET_DV), BF16),
                        pltpu.VMEM((RET_BLOCK, RET_DV), F32),
                        pltpu.VMEM((RET_CHUNK, RET_CHUNK), F32),
                        pltpu.VMEM((2, RET_CHUNK, RET_DV), F32),
                        pltpu.VMEM((2, RET_CHUNK, RET_DK), F32),
                        pltpu.VMEM((2, 1, RET_DV), F32)],
        compiler_params=_params("parallel", "parallel", "arbitrary", "arbitrary"),
        name="ret_scan",
    )(lg, q, k, v, gate)
    return out.reshape(batch * length, hv)


def _swap_halves(a):
    half = a.shape[-1] // 2
    return jnp.concatenate([a[..., half:], a[..., :half]], axis=-1)


def _rope_slot_layout(a):
    pe = a[..., QK_NOPE_DIM:]
    return jnp.concatenate([a[..., :QK_NOPE_DIM], pe, _swap_halves(pe)], axis=-1)


def _mla_weights(w_dq, q_a_norm, w_uq, w_dkv, kv_a_norm, w_ukv, q_norm, k_norm):
    uq = w_uq.reshape(Q_LORA_RANK, MLA_HEADS, QK_HEAD_DIM)
    uq = _rope_slot_layout(uq).reshape(Q_LORA_RANK, MLA_HEADS * QK_SLOT)
    pe = w_dkv[:, KV_LORA_RANK:]
    dkv = jnp.concatenate([w_dkv[:, :KV_LORA_RANK], pe, _swap_halves(pe)], axis=-1)
    ukv = w_ukv.reshape(KV_LORA_RANK, MLA_HEADS, QK_NOPE_DIM + V_HEAD_DIM)
    wk = ukv[..., :QK_NOPE_DIM].reshape(KV_LORA_RANK, MLA_HEADS * QK_NOPE_DIM)
    wvt = ukv[..., QK_NOPE_DIM:].reshape(KV_LORA_RANK, MLA_HEADS * V_HEAD_DIM).T
    return {
        "dq": w_dq.astype(BF16), "qa": q_a_norm[None, :], "uqt": uq.T.astype(BF16),
        "dkv": dkv.astype(BF16), "kva": kv_a_norm[None, :],
        "k": wk.astype(BF16), "vt": wvt.astype(BF16),
        "qgt": jnp.broadcast_to(_rope_slot_layout(q_norm)[:, None], (QK_SLOT, ROW_TILE)),
        "kg": _rope_slot_layout(k_norm)[None, :],
    }


def _rope_tables(length, dim):
    inv = ROPE_BASE ** (-jnp.arange(0, dim, 2, dtype=F32) / dim)
    ang = jnp.arange(length, dtype=F32)[:, None] * inv[None, :]
    return jnp.cos(ang), jnp.sin(ang)


def _trunk(x, p):
    batch, length, d = x.shape
    assert d == D_MODEL and length % ROW_TILE == 0 and length % FFN_ROW_TILE == 0
    x = x.reshape(batch * length, d)
    cos_m, sin_m = _rope_tables(length, QK_ROPE_DIM)
    cs_mla = jnp.concatenate([cos_m, cos_m, -sin_m, sin_m], axis=-1)
    cos_r, sin_r = _rope_tables(length, RET_DK)
    for i in range(DEPTH):
        j = i // 2
        ng = p["norm_g"][i]
        f = p["ffn"][i]
        h = _ffn_first(x, ng[0:1], f[0])
        if i % 2 == 0:
            w = p["mla"][j]
            qt, k, vt = _mla_proj(h, ng[1:2], w, cs_mla, batch, length)
            o = _attention(qt, k, vt, batch, length)
        else:
            w = p["ret"][j]
            q, k, v, gate = _ret_proj(h, ng[1:2], w["in"], w["gn"], cos_r, sin_r, length)
            o = _ret_scan(q, k, v, gate, w["lg"], batch, length)
        x = _ffn_second(h, o, w["o"], ng[2:3], f[1], ng[3:4])
    return x.reshape(batch, length, d)


def kernel(x_prompt, x_sample, norm_g, ffn_w_gate, ffn_w_up, ffn_w_down, mla_w_dq, mla_q_a_norm, mla_w_uq, mla_w_dkv, mla_kv_a_norm, mla_w_ukv, mla_q_norm, mla_k_norm, mla_w_o, ret_w_in, ret_decay_fwd, ret_decay_bwd, ret_gn_g, ret_w_o):
    p = {
        "norm_g": norm_g,
        "ffn": [[{"g": ffn_w_gate[i, s].astype(BF16), "u": ffn_w_up[i, s].astype(BF16),
                  "d": ffn_w_down[i, s].astype(BF16)} for s in range(2)] for i in range(DEPTH)],
        "mla": [], "ret": [],
    }
    for j in range(mla_w_dq.shape[0]):
        w = _mla_weights(mla_w_dq[j], mla_q_a_norm[j], mla_w_uq[j], mla_w_dkv[j], mla_kv_a_norm[j],
                         mla_w_ukv[j], mla_q_norm[j], mla_k_norm[j])
        w["o"] = mla_w_o[j].astype(BF16)
        p["mla"].append(w)
    for j in range(ret_w_in.shape[0]):
        lg = jnp.stack([jnp.log1p(-jnp.exp(ret_decay_bwd[j].astype(F32))),
                        jnp.log1p(-jnp.exp(ret_decay_fwd[j].astype(F32)))])
        p["ret"].append({"in": ret_w_in[j].astype(BF16), "lg": lg, "gn": ret_gn_g[j][None, :],
                         "o": ret_w_o[j].astype(BF16)})
    return _trunk(x_prompt, p), _trunk(x_sample, p)
```

```python
import math

import jax
import jax.numpy as jnp
from jax import lax
from jax.experimental import pallas as pl
from jax.experimental.pallas import tpu as pltpu

F32 = jnp.float32
BF16 = jnp.bfloat16

D_MODEL = 1024
DEPTH = 4
D_FF = 2816
MLA_HEADS = 8
QK_NOPE_DIM = 128
QK_ROPE_DIM = 64
QK_HEAD_DIM = QK_NOPE_DIM + QK_ROPE_DIM
V_HEAD_DIM = 128
Q_LORA_RANK = 384
KV_LORA_RANK = 256
ROPE_BASE = 10000.0
RET_HEADS = 4
RET_DK = D_MODEL // RET_HEADS
RET_DV = 2 * D_MODEL // RET_HEADS
EPS = 1e-6

LANES_V7X = 128
MXU_DIM_V7X = 256
VMEM_LIMIT_BYTES_V7X = 56 * 1024 * 1024

QK_SLOT = QK_NOPE_DIM + 2 * QK_ROPE_DIM
assert QK_SLOT == MXU_DIM_V7X
BF16_SUBLANES_V7X = 16
V_SLOT = V_HEAD_DIM + BF16_SUBLANES_V7X

ROW_TILE = 512
FFN_FIRST_ROW_TILE = 1024
FFN_SECOND_ROW_TILE = 512
ATTN_TQ = 512
ATTN_TK = 512
ATTN_LOOKAHEAD = 2
ATTN_BUFFERS = 4
ATTN_Q_PER_TRIP = 2
RET_CHUNK = MXU_DIM_V7X
RET_BLOCK = 2048
RET_NORM_ROWS = 64

_NT = (((1,), (1,)), ((), ()))


def _params(*sem):
    return pltpu.CompilerParams(dimension_semantics=sem, vmem_limit_bytes=VMEM_LIMIT_BYTES_V7X)


def _resident(shape):
    return pl.BlockSpec(shape, lambda *_: (0,) * len(shape), pipeline_mode=pl.Buffered(1))


def _rms(x, g):
    return x * lax.rsqrt(jnp.mean(x * x, axis=-1, keepdims=True) + EPS) * g


def _mm(a, b):
    return jnp.dot(a, b, preferred_element_type=F32)


def _swiglu_half(x, g_ref, wg_ref, wu_ref, wd_ref):
    xn = _rms(x, g_ref[...]).astype(BF16)
    gate = _mm(xn, wg_ref[...])
    up = _mm(xn, wu_ref[...])
    h = (gate * jax.nn.sigmoid(gate) * up).astype(BF16)
    return x + 0.5 * _mm(h, wd_ref[...])


def _ffn_first_body(x_ref, g_ref, wg_ref, wu_ref, wd_ref, o_ref):
    o_ref[...] = _swiglu_half(x_ref[...], g_ref, wg_ref, wu_ref, wd_ref)


def _ffn_second_body(x_ref, a_ref, wo_ref, g_ref, wg_ref, wu_ref, wd_ref, gp_ref, o_ref):
    h = x_ref[...] + _mm(a_ref[...], wo_ref[...])
    o_ref[...] = _rms(_swiglu_half(h, g_ref, wg_ref, wu_ref, wd_ref), gp_ref[...])


def _ffn_first(x, g, f):
    n, d = x.shape
    row = pl.BlockSpec((FFN_FIRST_ROW_TILE, d), lambda i: (i, 0))
    return pl.pallas_call(
        _ffn_first_body,
        grid=(n // FFN_FIRST_ROW_TILE,),
        in_specs=[row, _resident((1, d)), _resident(f["g"].shape), _resident(f["u"].shape),
                  _resident(f["d"].shape)],
        out_specs=row,
        out_shape=jax.ShapeDtypeStruct((n, d), F32),
        compiler_params=_params("parallel"),
        name="ffn_first",
    )(x, g, f["g"], f["u"], f["d"])


def _ffn_second(x, a, w_o, g, f, g_post):
    n, d = x.shape
    row = pl.BlockSpec((FFN_SECOND_ROW_TILE, d), lambda i: (i, 0))
    return pl.pallas_call(
        _ffn_second_body,
        grid=(n // FFN_SECOND_ROW_TILE,),
        in_specs=[row, pl.BlockSpec((FFN_SECOND_ROW_TILE, a.shape[1]), lambda i: (i, 0)), _resident(w_o.shape),
                  _resident((1, d)), _resident(f["g"].shape), _resident(f["u"].shape),
                  _resident(f["d"].shape), _resident((1, d))],
        out_specs=row,
        out_shape=jax.ShapeDtypeStruct((n, d), F32),
        compiler_params=_params("parallel"),
        name="ffn_second",
    )(x, a, w_o, g, f["g"], f["u"], f["d"], g_post)


def _rope_slot(t, gain, cs):
    a = t * gain * cs
    lane = lax.broadcasted_iota(jnp.int32, a.shape, 1)
    return jnp.where(lane < QK_ROPE_DIM, a + pltpu.roll(a, QK_ROPE_DIM, axis=1), 0.0)


def _sumsq_rope(t):
    lane = lax.broadcasted_iota(jnp.int32, t.shape, 1)
    return jnp.sum(jnp.where(lane < QK_ROPE_DIM, t * t, 0.0), axis=-1, keepdims=True)


def _mla_proj_body(h_ref, g_ref, wdq_ref, qa_ref, wuqt_ref, wdkv_ref, kva_ref, wk_ref, wvt_ref,
                   qgt_ref, kg_ref, cs_ref, cst_ref, qt_ref, k_ref, vt_ref):
    hn = _rms(h_ref[...], g_ref[...]).astype(BF16)
    cs = cs_ref[...]
    kg = kg_ref[...]
    inv_dim = 1.0 / QK_HEAD_DIM

    ckv_full = _mm(hn, wdkv_ref[...])
    ckv = _rms(ckv_full[:, :KV_LORA_RANK], kva_ref[...]).astype(BF16)
    vt = lax.dot_general(wvt_ref[...], ckv, _NT, preferred_element_type=F32).astype(BF16)
    for hd in range(MLA_HEADS):
        vt_ref[hd * V_SLOT:hd * V_SLOT + V_HEAD_DIM, :] = vt[hd * V_HEAD_DIM:(hd + 1) * V_HEAD_DIM, :]
        vt_ref[hd * V_SLOT + V_HEAD_DIM:(hd + 1) * V_SLOT, :] = jnp.ones((BF16_SUBLANES_V7X, vt.shape[1]), BF16)
    k_pe_raw = ckv_full[:, KV_LORA_RANK:]
    k_pe_ss = _sumsq_rope(k_pe_raw)
    k_pe = _rope_slot(k_pe_raw, kg[:, QK_NOPE_DIM:], cs)
    k_nope = _mm(ckv, wk_ref[...])
    for hd in range(MLA_HEADS):
        kn = k_nope[:, hd * QK_NOPE_DIM:(hd + 1) * QK_NOPE_DIM]
        r = lax.rsqrt((jnp.sum(kn * kn, axis=-1, keepdims=True) + k_pe_ss) * inv_dim + EPS)
        k_ref[:, hd * QK_SLOT:hd * QK_SLOT + QK_NOPE_DIM] = (kn * r * kg[:, :QK_NOPE_DIM]).astype(BF16)
        k_ref[:, hd * QK_SLOT + QK_NOPE_DIM:(hd + 1) * QK_SLOT] = (k_pe * r).astype(BF16)

    cq = _rms(_mm(hn, wdq_ref[...]), qa_ref[...]).astype(BF16)
    scale = QK_HEAD_DIM ** -0.5 * math.log2(math.e)
    cst = cst_ref[...]
    qgt = qgt_ref[...]
    for hd in range(MLA_HEADS):
        base = hd * QK_SLOT
        qt = lax.dot_general(wuqt_ref[base:base + QK_SLOT, :], cq, _NT, preferred_element_type=F32)
        qn, qp = qt[:QK_NOPE_DIM], qt[QK_NOPE_DIM:]
        pe = qp[:QK_ROPE_DIM]
        ss = jnp.sum(qn * qn, axis=0, keepdims=True) + jnp.sum(pe * pe, axis=0, keepdims=True)
        r = lax.rsqrt(ss * inv_dim + EPS) * scale
        a = qp * qgt[QK_NOPE_DIM:] * cst
        qt_ref[base:base + QK_NOPE_DIM, :] = (qn * r * qgt[:QK_NOPE_DIM]).astype(BF16)
        qt_ref[base + QK_NOPE_DIM:base + QK_HEAD_DIM, :] = (
            (a[:QK_ROPE_DIM] + a[QK_ROPE_DIM:]) * r).astype(BF16)
        qt_ref[base + QK_HEAD_DIM:base + QK_SLOT, :] = jnp.zeros((QK_SLOT - QK_HEAD_DIM, qt.shape[1]), BF16)


def _mla_proj(h, g, w, cs, batch, length):
    n, d = h.shape
    nblk = length // ROW_TILE
    slots = MLA_HEADS * QK_SLOT
    dv = MLA_HEADS * V_SLOT
    row = lambda width: pl.BlockSpec((ROW_TILE, width), lambda i: (i, 0))
    return pl.pallas_call(
        _mla_proj_body,
        grid=(n // ROW_TILE,),
        in_specs=[row(d), _resident((1, d)), _resident(w["dq"].shape), _resident((1, Q_LORA_RANK)),
                  _resident(w["uqt"].shape), _resident(w["dkv"].shape), _resident((1, KV_LORA_RANK)),
                  _resident(w["k"].shape), _resident(w["vt"].shape),
                  _resident((QK_SLOT, ROW_TILE)), _resident((1, QK_SLOT)),
                  pl.BlockSpec((ROW_TILE, LANES_V7X), lambda i: (i % nblk, 0)),
                  pl.BlockSpec((LANES_V7X, ROW_TILE), lambda i: (0, i % nblk))],
        out_specs=[pl.BlockSpec((None, slots, ROW_TILE), lambda i: (i // nblk, 0, i % nblk)),
                   row(slots),
                   pl.BlockSpec((None, dv, ROW_TILE), lambda i: (i // nblk, 0, i % nblk))],
        out_shape=[jax.ShapeDtypeStruct((batch, slots, length), BF16), jax.ShapeDtypeStruct((n, slots), BF16),
                   jax.ShapeDtypeStruct((batch, dv, length), BF16)],
        compiler_params=_params("parallel"),
        name="mla_proj",
    )(h, g, w["dq"], w["qa"], w["uqt"], w["dkv"], w["kva"], w["k"], w["vt"], w["qgt"], w["kg"], cs, cs.T)


def _attn_body(qt_ref, k_ref, vt_ref, o_ref, s_ref, smax_ref, m_ref, acc_ref):
    length = k_ref.shape[0]
    n_chunks = length // ATTN_TK
    n_q = length // ATTN_TQ

    def scores(q_tile, j, slot):
        q_off = pl.multiple_of(q_tile * ATTN_TQ, ATTN_TQ)
        st = _mm(k_ref[j * ATTN_TK:(j + 1) * ATTN_TK, :], qt_ref[:, pl.ds(q_off, ATTN_TQ)])
        s_ref[slot] = st
        smax_ref[slot] = jnp.max(st, axis=0, keepdims=True)

    def update(j, slot):
        m_old = m_ref[...]
        m_new = jnp.maximum(m_old, smax_ref[slot])
        p = jnp.exp2(s_ref[slot] - m_new).astype(BF16)
        alpha = jnp.exp2(m_old - m_new)
        acc_ref[...] = alpha * acc_ref[...] + _mm(vt_ref[:, j * ATTN_TK:(j + 1) * ATTN_TK], p)
        m_ref[...] = m_new

    for j in range(ATTN_LOOKAHEAD):
        scores(0, j, j)

    def q_tile(tile):
        m_ref[...] = jnp.full(m_ref.shape, -jnp.inf, F32)
        acc_ref[...] = jnp.zeros(acc_ref.shape, F32)
        next_tile = lax.rem(tile + 1, n_q)
        for j in range(n_chunks):
            ahead = j + ATTN_LOOKAHEAD
            scores(tile if ahead < n_chunks else next_tile, ahead % n_chunks, ahead % ATTN_BUFFERS)
            update(j, j % ATTN_BUFFERS)
        o = acc_ref[:V_HEAD_DIM, :] / acc_ref[V_HEAD_DIM:V_HEAD_DIM + 1, :]
        o_ref[pl.ds(pl.multiple_of(tile * ATTN_TQ, ATTN_TQ), ATTN_TQ), :] = o.T.astype(BF16)

    def trip(t, carry):
        for i in range(ATTN_Q_PER_TRIP):
            q_tile(t * ATTN_Q_PER_TRIP + i)
        return carry

    lax.fori_loop(0, n_q // ATTN_Q_PER_TRIP, trip, 0)


def _attention(qt, k, vt, batch, length):
    assert length % (ATTN_TQ * ATTN_Q_PER_TRIP) == 0 and length % (ATTN_TK * ATTN_BUFFERS) == 0
    k = k.reshape(batch, length, MLA_HEADS * QK_SLOT)
    out = pl.pallas_call(
        _attn_body,
        grid=(batch, MLA_HEADS),
        in_specs=[pl.BlockSpec((None, QK_SLOT, length), lambda b, h: (b, h, 0)),
                  pl.BlockSpec((None, length, QK_SLOT), lambda b, h: (b, 0, h)),
                  pl.BlockSpec((None, V_SLOT, length), lambda b, h: (b, h, 0))],
        out_specs=pl.BlockSpec((None, length, V_HEAD_DIM), lambda b, h: (b, 0, h)),
        out_shape=jax.ShapeDtypeStruct((batch, length, MLA_HEADS * V_HEAD_DIM), BF16),
        scratch_shapes=[pltpu.VMEM((ATTN_BUFFERS, ATTN_TK, ATTN_TQ), F32),
                        pltpu.VMEM((ATTN_BUFFERS, 1, ATTN_TQ), F32), pltpu.VMEM((1, ATTN_TQ), F32),
                        pltpu.VMEM((V_SLOT, ATTN_TQ), F32)],
        compiler_params=_params("parallel", "parallel"),
        name="mla_attn",
    )(qt, k, vt)
    return out.reshape(batch * length, MLA_HEADS * V_HEAD_DIM)


def _ret_proj_body(h_ref, g_ref, w_ref, gn_ref, cos_ref, sin_ref, q_ref, k_ref, v_ref, gate_ref):
    hn = _rms(h_ref[...], g_ref[...]).astype(BF16)
    cos = cos_ref[...]
    sin = sin_ref[...]
    hq = RET_HEADS * RET_DK
    hv = RET_HEADS * RET_DV
    half = RET_DK // 2

    def rope_store(dst, col0, mult):
        for hd in range(RET_HEADS):
            x = _mm(hn, w_ref[:, col0 + hd * RET_DK:col0 + (hd + 1) * RET_DK])
            x1, x2 = x[:, :half], x[:, half:]
            dst[:, hd * RET_DK:hd * RET_DK + half] = ((x1 * cos - x2 * sin) * mult).astype(BF16)
            dst[:, hd * RET_DK + half:(hd + 1) * RET_DK] = ((x2 * cos + x1 * sin) * mult).astype(BF16)

    rope_store(q_ref, 0, 1.0)
    rope_store(k_ref, hq, RET_DK ** -0.5)
    v_ref[...] = _mm(hn, w_ref[:, 2 * hq:2 * hq + hv]).astype(BF16)
    gate = _mm(hn, w_ref[:, 2 * hq + hv:])
    gate_ref[...] = (gate * jax.nn.sigmoid(gate) * gn_ref[...]).astype(BF16)


def _ret_proj(h, g, w_in, gn_g, cos, sin, length):
    n, d = h.shape
    nblk = length // ROW_TILE
    hq = RET_HEADS * RET_DK
    hv = RET_HEADS * RET_DV
    row = lambda width: pl.BlockSpec((ROW_TILE, width), lambda i: (i, 0))
    pos = pl.BlockSpec((ROW_TILE, RET_DK // 2), lambda i: (i % nblk, 0))
    return pl.pallas_call(
        _ret_proj_body,
        grid=(n // ROW_TILE,),
        in_specs=[row(d), _resident((1, d)), _resident(w_in.shape), _resident((1, hv)), pos, pos],
        out_specs=[row(hq), row(hq), row(hv), row(hv)],
        out_shape=[jax.ShapeDtypeStruct((n, hq), BF16), jax.ShapeDtypeStruct((n, hq), BF16),
                   jax.ShapeDtypeStruct((n, hv), BF16), jax.ShapeDtypeStruct((n, hv), BF16)],
        compiler_params=_params("parallel"),
        name="ret_proj",
    )(h, g, w_in, gn_g, cos, sin)


_BWD, _FWD = 0, 1


def _ret_scan_body(lg_ref, q_ref, k_ref, v_ref, gate_ref, o_ref,
                   state_ref, sb_ref, o_scr, dmat_ref, xi_ref, zeta_ref, gc_ref):
    hd = pl.program_id(1)
    direction = pl.program_id(2)
    blk = pl.program_id(3)
    nblk = pl.num_programs(3)
    c = RET_CHUNK
    n_chunks = RET_BLOCK // c

    @pl.when(blk == 0)
    def _reset_state():
        state_ref[...] = jnp.zeros(state_ref.shape, F32)

    @pl.when((blk == 0) & (direction == _BWD))
    def _decay_tables():
        lg_b = lg_ref[_BWD, hd]
        lg_f = lg_ref[_FWD, hd]
        row = lax.broadcasted_iota(jnp.int32, (c, c), 0)
        col = lax.broadcasted_iota(jnp.int32, (c, c), 1)
        dmat_ref[...] = jnp.where(row >= col,
                                  jnp.exp(lg_f * jnp.maximum(row - col, 0).astype(F32)),
                                  jnp.exp(lg_b * jnp.maximum(col - row, 0).astype(F32)))
        qi = lax.broadcasted_iota(jnp.int32, xi_ref.shape[1:], 0)
        xi_ref[_FWD] = jnp.exp(lg_f * (qi + 1).astype(F32))
        xi_ref[_BWD] = jnp.exp(lg_b * (c - qi).astype(F32))
        kj = lax.broadcasted_iota(jnp.int32, zeta_ref.shape[1:], 0)
        zeta_ref[_FWD] = jnp.exp(lg_f * (c - 1 - kj).astype(F32))
        zeta_ref[_BWD] = jnp.exp(lg_b * kj.astype(F32))
        gc_ref[_FWD] = jnp.exp(jnp.zeros(gc_ref.shape[1:], F32) + lg_f * c)
        gc_ref[_BWD] = jnp.exp(jnp.zeros(gc_ref.shape[1:], F32) + lg_b * c)

    def absorb(ci, d):
        rows = pl.ds(ci * c, c)
        kz_t = (k_ref[rows, :].astype(F32) * zeta_ref[d]).T.astype(BF16)
        state_ref[...] = state_ref[...] * gc_ref[d] + _mm(kz_t, v_ref[rows, :])

    @pl.when(direction == _BWD)
    def _backward_states():
        first = (nblk - 1 - blk) * n_chunks
        for ci in reversed(range(n_chunks)):
            sb_ref[first + ci] = state_ref[...].astype(BF16)
            absorb(ci, _BWD)

    @pl.when(direction == _FWD)
    def _forward():
        first = blk * n_chunks
        for ci in range(n_chunks):
            rows = pl.ds(ci * c, c)
            q = q_ref[rows, :]
            s = lax.dot_general(q, k_ref[rows, :], _NT, preferred_element_type=F32) * dmat_ref[...]
            o_scr[rows, :] = (_mm(s.astype(BF16), v_ref[rows, :])
                              + _mm(q, state_ref[...].astype(BF16)) * xi_ref[_FWD]
                              + _mm(q, sb_ref[first + ci]) * xi_ref[_BWD])
            absorb(ci, _FWD)
            for r in range(ci * c, (ci + 1) * c, RET_NORM_ROWS):
                o = o_scr[pl.ds(r, RET_NORM_ROWS), :]
                dev = o - jnp.mean(o, axis=-1, keepdims=True)
                on = dev * lax.rsqrt(jnp.mean(dev * dev, axis=-1, keepdims=True) + EPS)
                gate = gate_ref[pl.ds(r, RET_NORM_ROWS), :].astype(F32)
                o_ref[pl.ds(r, RET_NORM_ROWS), :] = (gate * on).astype(BF16)


def _ret_scan(q, k, v, gate, lg, batch, length):
    hq = RET_HEADS * RET_DK
    hv = RET_HEADS * RET_DV
    assert length % RET_BLOCK == 0
    nblk = length // RET_BLOCK
    q = q.reshape(batch, length, hq)
    k = k.reshape(batch, length, hq)
    v = v.reshape(batch, length, hv)
    gate = gate.reshape(batch, length, hv)

    def sweep(b, h, d, i):
        return (b, d * i + (1 - d) * (nblk - 1 - i), h)

    def fwd_only(b, h, d, i):
        return (b, d * i, h)

    out = pl.pallas_call(
        _ret_scan_body,
        grid=(batch, RET_HEADS, 2, nblk),
        in_specs=[pl.BlockSpec(memory_space=pltpu.SMEM),
                  pl.BlockSpec((None, RET_BLOCK, RET_DK), fwd_only),
                  pl.BlockSpec((None, RET_BLOCK, RET_DK), sweep),
                  pl.BlockSpec((None, RET_BLOCK, RET_DV), sweep),
                  pl.BlockSpec((None, RET_BLOCK, RET_DV), fwd_only)],
        out_specs=pl.BlockSpec((None, RET_BLOCK, RET_DV), fwd_only),
        out_shape=jax.ShapeDtypeStruct((batch, length, hv), BF16),
        scratch_shapes=[pltpu.VMEM((RET_DK, RET_DV), F32),
                        pltpu.VMEM((length // RET_CHUNK, RET_DK, RET_DV), BF16),
                        pltpu.VMEM((RET_BLOCK, RET_DV), F32),
                        pltpu.VMEM((RET_CHUNK, RET_CHUNK), F32),
                        pltpu.VMEM((2, RET_CHUNK, RET_DV), F32),
                        pltpu.VMEM((2, RET_CHUNK, RET_DK), F32),
                        pltpu.VMEM((2, 1, RET_DV), F32)],
        compiler_params=_params("parallel", "parallel", "arbitrary", "arbitrary"),
        name="ret_scan",
    )(lg, q, k, v, gate)
    return out.reshape(batch * length, hv)


def _swap_halves(a):
    half = a.shape[-1] // 2
    return jnp.concatenate([a[..., half:], a[..., :half]], axis=-1)


def _rope_slot_layout(a):
    pe = a[..., QK_NOPE_DIM:]
    return jnp.concatenate([a[..., :QK_NOPE_DIM], pe, _swap_halves(pe)], axis=-1)


def _mla_weights(w_dq, q_a_norm, w_uq, w_dkv, kv_a_norm, w_ukv, q_norm, k_norm):
    uq = w_uq.reshape(Q_LORA_RANK, MLA_HEADS, QK_HEAD_DIM)
    uq = _rope_slot_layout(uq).reshape(Q_LORA_RANK, MLA_HEADS * QK_SLOT)
    pe = w_dkv[:, KV_LORA_RANK:]
    dkv = jnp.concatenate([w_dkv[:, :KV_LORA_RANK], pe, _swap_halves(pe)], axis=-1)
    ukv = w_ukv.reshape(KV_LORA_RANK, MLA_HEADS, QK_NOPE_DIM + V_HEAD_DIM)
    wk = ukv[..., :QK_NOPE_DIM].reshape(KV_LORA_RANK, MLA_HEADS * QK_NOPE_DIM)
    wvt = ukv[..., QK_NOPE_DIM:].reshape(KV_LORA_RANK, MLA_HEADS * V_HEAD_DIM).T
    return {
        "dq": w_dq.astype(BF16), "qa": q_a_norm[None, :], "uqt": uq.T.astype(BF16),
        "dkv": dkv.astype(BF16), "kva": kv_a_norm[None, :],
        "k": wk.astype(BF16), "vt": wvt.astype(BF16),
        "qgt": jnp.broadcast_to(_rope_slot_layout(q_norm)[:, None], (QK_SLOT, ROW_TILE)),
        "kg": _rope_slot_layout(k_norm)[None, :],
    }


def _rope_tables(length, dim):
    inv = ROPE_BASE ** (-jnp.arange(0, dim, 2, dtype=F32) / dim)
    ang = jnp.arange(length, dtype=F32)[:, None] * inv[None, :]
    return jnp.cos(ang), jnp.sin(ang)


def _trunk(x, p):
    batch, length, d = x.shape
    assert d == D_MODEL and length % ROW_TILE == 0 and length % FFN_FIRST_ROW_TILE == 0
    x = x.reshape(batch * length, d)
    cos_m, sin_m = _rope_tables(length, QK_ROPE_DIM)
    cs_mla = jnp.concatenate([cos_m, cos_m, -sin_m, sin_m], axis=-1)
    cos_r, sin_r = _rope_tables(length, RET_DK)
    for i in range(DEPTH):
        j = i // 2
        ng = p["norm_g"][i]
        f = p["ffn"][i]
        h = _ffn_first(x, ng[0:1], f[0])
        if i % 2 == 0:
            w = p["mla"][j]
            qt, k, vt = _mla_proj(h, ng[1:2], w, cs_mla, batch, length)
            o = _attention(qt, k, vt, batch, length)
        else:
            w = p["ret"][j]
            q, k, v, gate = _ret_proj(h, ng[1:2], w["in"], w["gn"], cos_r, sin_r, length)
            o = _ret_scan(q, k, v, gate, w["lg"], batch, length)
        x = _ffn_second(h, o, w["o"], ng[2:3], f[1], ng[3:4])
    return x.reshape(batch, length, d)


def kernel(x_prompt, x_sample, norm_g, ffn_w_gate, ffn_w_up, ffn_w_down, mla_w_dq, mla_q_a_norm, mla_w_uq, mla_w_dkv, mla_kv_a_norm, mla_w_ukv, mla_q_norm, mla_k_norm, mla_w_o, ret_w_in, ret_decay_fwd, ret_decay_bwd, ret_gn_g, ret_w_o):
    p = {
        "norm_g": norm_g,
        "ffn": [[{"g": ffn_w_gate[i, s].astype(BF16), "u": ffn_w_up[i, s].astype(BF16),
                  "d": ffn_w_down[i, s].astype(BF16)} for s in range(2)] for i in range(DEPTH)],
        "mla": [], "ret": [],
    }
    for j in range(mla_w_dq.shape[0]):
        w = _mla_weights(mla_w_dq[j], mla_q_a_norm[j], mla_w_uq[j], mla_w_dkv[j], mla_kv_a_norm[j],
                         mla_w_ukv[j], mla_q_norm[j], mla_k_norm[j])
        w["o"] = mla_w_o[j].astype(BF16)
        p["mla"].append(w)
    for j in range(ret_w_in.shape[0]):
        lg = jnp.stack([jnp.log1p(-jnp.exp(ret_decay_bwd[j].astype(F32))),
                        jnp.log1p(-jnp.exp(ret_decay_fwd[j].astype(F32)))])
        p["ret"].append({"in": ret_w_in[j].astype(BF16), "lg": lg, "gn": ret_gn_g[j][None, :],
                         "o": ret_w_o[j].astype(BF16)})
    return _trunk(x_prompt, p), _trunk(x_sample, p)
```

```python
import math

import jax
import jax.numpy as jnp
from jax import lax
from jax.experimental import pallas as pl
from jax.experimental.pallas import tpu as pltpu

F32 = jnp.float32
BF16 = jnp.bfloat16

D_MODEL = 1024
DEPTH = 4
D_FF = 2816
MLA_HEADS = 8
QK_NOPE_DIM = 128
QK_ROPE_DIM = 64
QK_HEAD_DIM = QK_NOPE_DIM + QK_ROPE_DIM
V_HEAD_DIM = 128
Q_LORA_RANK = 384
KV_LORA_RANK = 256
ROPE_BASE = 10000.0
RET_HEADS = 4
RET_DK = D_MODEL // RET_HEADS
RET_DV = 2 * D_MODEL // RET_HEADS
EPS = 1e-6

LANES_V7X = 128
MXU_DIM_V7X = 256
VMEM_LIMIT_BYTES_V7X = 56 * 1024 * 1024

QK_SLOT = QK_NOPE_DIM + 2 * QK_ROPE_DIM
assert QK_SLOT == MXU_DIM_V7X
BF16_SUBLANES_V7X = 16
V_SLOT = V_HEAD_DIM + BF16_SUBLANES_V7X

ROW_TILE = 512
FFN_FIRST_ROW_TILE = 1024
FFN_SECOND_ROW_TILE = 512
ATTN_TQ = 512
ATTN_TK = 512
ATTN_LOOKAHEAD = 2
ATTN_BUFFERS = 4
ATTN_Q_PER_TRIP = 2
RET_CHUNK = MXU_DIM_V7X
RET_BLOCK = 2048
RET_NORM_ROWS = 64

_NT = (((1,), (1,)), ((), ()))


def _params(*sem):
    return pltpu.CompilerParams(dimension_semantics=sem, vmem_limit_bytes=VMEM_LIMIT_BYTES_V7X)


def _resident(shape):
    return pl.BlockSpec(shape, lambda *_: (0,) * len(shape), pipeline_mode=pl.Buffered(1))


def _rms(x, g):
    return x * lax.rsqrt(jnp.mean(x * x, axis=-1, keepdims=True) + EPS) * g


def _mm(a, b):
    return jnp.dot(a, b, preferred_element_type=F32)


def _swiglu_half(x, g_ref, wg_ref, wu_ref, wd_ref):
    xn = _rms(x, g_ref[...]).astype(BF16)
    gate = _mm(xn, wg_ref[...])
    up = _mm(xn, wu_ref[...])
    h = (gate * jax.nn.sigmoid(gate) * up).astype(BF16)
    return x + 0.5 * _mm(h, wd_ref[...])


def _ffn_first_body(x_ref, g_ref, wg_ref, wu_ref, wd_ref, o_ref):
    o_ref[...] = _swiglu_half(x_ref[...], g_ref, wg_ref, wu_ref, wd_ref)


def _ffn_second_body(x_ref, a_ref, wo_ref, g_ref, wg_ref, wu_ref, wd_ref, gp_ref, o_ref):
    h = x_ref[...] + _mm(a_ref[...], wo_ref[...])
    o_ref[...] = _rms(_swiglu_half(h, g_ref, wg_ref, wu_ref, wd_ref), gp_ref[...])


def _ffn_first(x, g, f):
    n, d = x.shape
    row = pl.BlockSpec((FFN_FIRST_ROW_TILE, d), lambda i: (i, 0))
    return pl.pallas_call(
        _ffn_first_body,
        grid=(n // FFN_FIRST_ROW_TILE,),
        in_specs=[row, _resident((1, d)), _resident(f["g"].shape), _resident(f["u"].shape),
                  _resident(f["d"].shape)],
        out_specs=row,
        out_shape=jax.ShapeDtypeStruct((n, d), F32),
        compiler_params=_params("parallel"),
        name="ffn_first",
    )(x, g, f["g"], f["u"], f["d"])


def _ffn_second(x, a, w_o, g, f, g_post):
    n, d = x.shape
    row = pl.BlockSpec((FFN_SECOND_ROW_TILE, d), lambda i: (i, 0))
    return pl.pallas_call(
        _ffn_second_body,
        grid=(n // FFN_SECOND_ROW_TILE,),
        in_specs=[row, pl.BlockSpec((FFN_SECOND_ROW_TILE, a.shape[1]), lambda i: (i, 0)), _resident(w_o.shape),
                  _resident((1, d)), _resident(f["g"].shape), _resident(f["u"].shape),
                  _resident(f["d"].shape), _resident((1, d))],
        out_specs=row,
        out_shape=jax.ShapeDtypeStruct((n, d), F32),
        compiler_params=_params("parallel"),
        name="ffn_second",
    )(x, a, w_o, g, f["g"], f["u"], f["d"], g_post)


def _rope_slot(t, gain, cs):
    a = t * gain * cs
    lane = lax.broadcasted_iota(jnp.int32, a.shape, 1)
    return jnp.where(lane < QK_ROPE_DIM, a + pltpu.roll(a, QK_ROPE_DIM, axis=1), 0.0)


def _sumsq_rope(t):
    lane = lax.broadcasted_iota(jnp.int32, t.shape, 1)
    return jnp.sum(jnp.where(lane < QK_ROPE_DIM, t * t, 0.0), axis=-1, keepdims=True)


def _mla_proj_body(h_ref, g_ref, wdq_ref, qa_ref, wuqt_ref, wdkv_ref, kva_ref, wk_ref, wvt_ref,
                   qgt_ref, kg_ref, cs_ref, cst_ref, qt_ref, k_ref, vt_ref):
    hn = _rms(h_ref[...], g_ref[...]).astype(BF16)
    cs = cs_ref[...]
    kg = kg_ref[...]
    inv_dim = 1.0 / QK_HEAD_DIM

    ckv_full = _mm(hn, wdkv_ref[...])
    ckv = _rms(ckv_full[:, :KV_LORA_RANK], kva_ref[...]).astype(BF16)
    vt = lax.dot_general(wvt_ref[...], ckv, _NT, preferred_element_type=F32).astype(BF16)
    for hd in range(MLA_HEADS):
        vt_ref[hd * V_SLOT:hd * V_SLOT + V_HEAD_DIM, :] = vt[hd * V_HEAD_DIM:(hd + 1) * V_HEAD_DIM, :]
        vt_ref[hd * V_SLOT + V_HEAD_DIM:(hd + 1) * V_SLOT, :] = jnp.ones((BF16_SUBLANES_V7X, vt.shape[1]), BF16)
    k_pe_raw = ckv_full[:, KV_LORA_RANK:]
    k_pe_ss = _sumsq_rope(k_pe_raw)
    k_pe = _rope_slot(k_pe_raw, kg[:, QK_NOPE_DIM:], cs)
    k_nope = _mm(ckv, wk_ref[...])
    for hd in range(MLA_HEADS):
        kn = k_nope[:, hd * QK_NOPE_DIM:(hd + 1) * QK_NOPE_DIM]
        r = lax.rsqrt((jnp.sum(kn * kn, axis=-1, keepdims=True) + k_pe_ss) * inv_dim + EPS)
        k_ref[hd, :, :QK_NOPE_DIM] = (kn * r * kg[:, :QK_NOPE_DIM]).astype(BF16)
        k_ref[hd, :, QK_NOPE_DIM:] = (k_pe * r).astype(BF16)

    cq = _rms(_mm(hn, wdq_ref[...]), qa_ref[...]).astype(BF16)
    scale = QK_HEAD_DIM ** -0.5 * math.log2(math.e)
    cst = cst_ref[...]
    qgt = qgt_ref[...]
    for hd in range(MLA_HEADS):
        base = hd * QK_SLOT
        qt = lax.dot_general(wuqt_ref[base:base + QK_SLOT, :], cq, _NT, preferred_element_type=F32)
        qn, qp = qt[:QK_NOPE_DIM], qt[QK_NOPE_DIM:]
        pe = qp[:QK_ROPE_DIM]
        ss = jnp.sum(qn * qn, axis=0, keepdims=True) + jnp.sum(pe * pe, axis=0, keepdims=True)
        r = lax.rsqrt(ss * inv_dim + EPS) * scale
        a = qp * qgt[QK_NOPE_DIM:] * cst
        qt_ref[base:base + QK_NOPE_DIM, :] = (qn * r * qgt[:QK_NOPE_DIM]).astype(BF16)
        qt_ref[base + QK_NOPE_DIM:base + QK_HEAD_DIM, :] = (
            (a[:QK_ROPE_DIM] + a[QK_ROPE_DIM:]) * r).astype(BF16)
        qt_ref[base + QK_HEAD_DIM:base + QK_SLOT, :] = jnp.zeros((QK_SLOT - QK_HEAD_DIM, qt.shape[1]), BF16)


def _mla_proj(h, g, w, cs, batch, length):
    n, d = h.shape
    nblk = length // ROW_TILE
    slots = MLA_HEADS * QK_SLOT
    dv = MLA_HEADS * V_SLOT
    row = lambda width: pl.BlockSpec((ROW_TILE, width), lambda i: (i, 0))
    return pl.pallas_call(
        _mla_proj_body,
        grid=(n // ROW_TILE,),
        in_specs=[row(d), _resident((1, d)), _resident(w["dq"].shape), _resident((1, Q_LORA_RANK)),
                  _resident(w["uqt"].shape), _resident(w["dkv"].shape), _resident((1, KV_LORA_RANK)),
                  _resident(w["k"].shape), _resident(w["vt"].shape),
                  _resident((QK_SLOT, ROW_TILE)), _resident((1, QK_SLOT)),
                  pl.BlockSpec((ROW_TILE, LANES_V7X), lambda i: (i % nblk, 0)),
                  pl.BlockSpec((LANES_V7X, ROW_TILE), lambda i: (0, i % nblk))],
        out_specs=[pl.BlockSpec((None, slots, ROW_TILE), lambda i: (i // nblk, 0, i % nblk)),
                   pl.BlockSpec((None, MLA_HEADS, ROW_TILE, QK_SLOT), lambda i: (i // nblk, 0, i % nblk, 0)),
                   pl.BlockSpec((None, dv, ROW_TILE), lambda i: (i // nblk, 0, i % nblk))],
        out_shape=[jax.ShapeDtypeStruct((batch, slots, length), BF16),
                   jax.ShapeDtypeStruct((batch, MLA_HEADS, length, QK_SLOT), BF16),
                   jax.ShapeDtypeStruct((batch, dv, length), BF16)],
        compiler_params=_params("parallel"),
        name="mla_proj",
    )(h, g, w["dq"], w["qa"], w["uqt"], w["dkv"], w["kva"], w["k"], w["vt"], w["qgt"], w["kg"], cs, cs.T)


def _attn_body(qt_ref, k_ref, vt_ref, o_ref, s_ref, smax_ref, m_ref, acc_ref):
    length = k_ref.shape[0]
    n_chunks = length // ATTN_TK
    n_q = length // ATTN_TQ

    def scores(q_tile, j, slot):
        q_off = pl.multiple_of(q_tile * ATTN_TQ, ATTN_TQ)
        st = _mm(k_ref[j * ATTN_TK:(j + 1) * ATTN_TK, :], qt_ref[:, pl.ds(q_off, ATTN_TQ)])
        s_ref[slot] = st
        smax_ref[slot] = jnp.max(st, axis=0, keepdims=True)

    def update(j, slot):
        m_old = m_ref[...]
        m_new = jnp.maximum(m_old, smax_ref[slot])
        p = jnp.exp2(s_ref[slot] - m_new).astype(BF16)
        alpha = jnp.exp2(m_old - m_new)
        acc_ref[...] = alpha * acc_ref[...] + _mm(vt_ref[:, j * ATTN_TK:(j + 1) * ATTN_TK], p)
        m_ref[...] = m_new

    for j in range(ATTN_LOOKAHEAD):
        scores(0, j, j)

    def q_tile(tile):
        m_ref[...] = jnp.full(m_ref.shape, -jnp.inf, F32)
        acc_ref[...] = jnp.zeros(acc_ref.shape, F32)
        next_tile = lax.rem(tile + 1, n_q)
        for j in range(n_chunks):
            ahead = j + ATTN_LOOKAHEAD
            scores(tile if ahead < n_chunks else next_tile, ahead % n_chunks, ahead % ATTN_BUFFERS)
            update(j, j % ATTN_BUFFERS)
        o = acc_ref[:V_HEAD_DIM, :] / acc_ref[V_HEAD_DIM:V_HEAD_DIM + 1, :]
        o_ref[pl.ds(pl.multiple_of(tile * ATTN_TQ, ATTN_TQ), ATTN_TQ), :] = o.T.astype(BF16)

    def trip(t, carry):
        for i in range(ATTN_Q_PER_TRIP):
            q_tile(t * ATTN_Q_PER_TRIP + i)
        return carry

    lax.fori_loop(0, n_q // ATTN_Q_PER_TRIP, trip, 0)


def _attention(qt, k, vt, batch, length):
    assert length % (ATTN_TQ * ATTN_Q_PER_TRIP) == 0 and length % (ATTN_TK * ATTN_BUFFERS) == 0
    out = pl.pallas_call(
        _attn_body,
        grid=(batch, MLA_HEADS),
        in_specs=[pl.BlockSpec((None, QK_SLOT, length), lambda b, h: (b, h, 0)),
                  pl.BlockSpec((None, None, length, QK_SLOT), lambda b, h: (b, h, 0, 0)),
                  pl.BlockSpec((None, V_SLOT, length), lambda b, h: (b, h, 0))],
        out_specs=pl.BlockSpec((None, length, V_HEAD_DIM), lambda b, h: (b, 0, h)),
        out_shape=jax.ShapeDtypeStruct((batch, length, MLA_HEADS * V_HEAD_DIM), BF16),
        scratch_shapes=[pltpu.VMEM((ATTN_BUFFERS, ATTN_TK, ATTN_TQ), F32),
                        pltpu.VMEM((ATTN_BUFFERS, 1, ATTN_TQ), F32), pltpu.VMEM((1, ATTN_TQ), F32),
                        pltpu.VMEM((V_SLOT, ATTN_TQ), F32)],
        compiler_params=_params("parallel", "parallel"),
        name="mla_attn",
    )(qt, k, vt)
    return out.reshape(batch * length, MLA_HEADS * V_HEAD_DIM)


def _ret_proj_body(h_ref, g_ref, w_ref, gn_ref, cos_ref, sin_ref, q_ref, k_ref, v_ref, gate_ref):
    hn = _rms(h_ref[...], g_ref[...]).astype(BF16)
    cos = cos_ref[...]
    sin = sin_ref[...]
    hq = RET_HEADS * RET_DK
    hv = RET_HEADS * RET_DV
    half = RET_DK // 2

    def rope_store(dst, col0, mult):
        for hd in range(RET_HEADS):
            x = _mm(hn, w_ref[:, col0 + hd * RET_DK:col0 + (hd + 1) * RET_DK])
            x1, x2 = x[:, :half], x[:, half:]
            dst[hd, :, :half] = ((x1 * cos - x2 * sin) * mult).astype(BF16)
            dst[hd, :, half:] = ((x2 * cos + x1 * sin) * mult).astype(BF16)

    rope_store(q_ref, 0, 1.0)
    rope_store(k_ref, hq, RET_DK ** -0.5)
    v = _mm(hn, w_ref[:, 2 * hq:2 * hq + hv]).astype(BF16)
    gate = _mm(hn, w_ref[:, 2 * hq + hv:])
    gate = (gate * jax.nn.sigmoid(gate) * gn_ref[...]).astype(BF16)
    for hd in range(RET_HEADS):
        v_ref[hd] = v[:, hd * RET_DV:(hd + 1) * RET_DV]
        gate_ref[hd] = gate[:, hd * RET_DV:(hd + 1) * RET_DV]


def _ret_proj(h, g, w_in, gn_g, cos, sin, batch, length):
    n, d = h.shape
    nblk = length // ROW_TILE
    hv = RET_HEADS * RET_DV
    row = lambda width: pl.BlockSpec((ROW_TILE, width), lambda i: (i, 0))
    pos = pl.BlockSpec((ROW_TILE, RET_DK // 2), lambda i: (i % nblk, 0))
    heads = lambda dim: pl.BlockSpec((None, RET_HEADS, ROW_TILE, dim), lambda i: (i // nblk, 0, i % nblk, 0))
    shape = lambda dim: jax.ShapeDtypeStruct((batch, RET_HEADS, length, dim), BF16)
    return pl.pallas_call(
        _ret_proj_body,
        grid=(n // ROW_TILE,),
        in_specs=[row(d), _resident((1, d)), _resident(w_in.shape), _resident((1, hv)), pos, pos],
        out_specs=[heads(RET_DK), heads(RET_DK), heads(RET_DV), heads(RET_DV)],
        out_shape=[shape(RET_DK), shape(RET_DK), shape(RET_DV), shape(RET_DV)],
        compiler_params=_params("parallel"),
        name="ret_proj",
    )(h, g, w_in, gn_g, cos, sin)


_BWD, _FWD = 0, 1


def _ret_scan_body(lg_ref, q_ref, k_ref, v_ref, gate_ref, o_ref,
                   state_ref, sb_ref, o_scr, dmat_ref, xi_ref, zeta_ref, gc_ref):
    hd = pl.program_id(1)
    direction = pl.program_id(2)
    blk = pl.program_id(3)
    nblk = pl.num_programs(3)
    c = RET_CHUNK
    n_chunks = RET_BLOCK // c

    @pl.when(blk == 0)
    def _reset_state():
        state_ref[...] = jnp.zeros(state_ref.shape, F32)

    @pl.when((blk == 0) & (direction == _BWD))
    def _decay_tables():
        lg_b = lg_ref[_BWD, hd]
        lg_f = lg_ref[_FWD, hd]
        row = lax.broadcasted_iota(jnp.int32, (c, c), 0)
        col = lax.broadcasted_iota(jnp.int32, (c, c), 1)
        dmat_ref[...] = jnp.where(row >= col,
                                  jnp.exp(lg_f * jnp.maximum(row - col, 0).astype(F32)),
                                  jnp.exp(lg_b * jnp.maximum(col - row, 0).astype(F32)))
        qi = lax.broadcasted_iota(jnp.int32, xi_ref.shape[1:], 0)
        xi_ref[_FWD] = jnp.exp(lg_f * (qi + 1).astype(F32))
        xi_ref[_BWD] = jnp.exp(lg_b * (c - qi).astype(F32))
        kj = lax.broadcasted_iota(jnp.int32, zeta_ref.shape[1:], 0)
        zeta_ref[_FWD] = jnp.exp(lg_f * (c - 1 - kj).astype(F32))
        zeta_ref[_BWD] = jnp.exp(lg_b * kj.astype(F32))
        gc_ref[_FWD] = jnp.exp(jnp.zeros(gc_ref.shape[1:], F32) + lg_f * c)
        gc_ref[_BWD] = jnp.exp(jnp.zeros(gc_ref.shape[1:], F32) + lg_b * c)

    def absorb(ci, d):
        rows = pl.ds(ci * c, c)
        kz_t = (k_ref[rows, :].astype(F32) * zeta_ref[d]).T.astype(BF16)
        state_ref[...] = state_ref[...] * gc_ref[d] + _mm(kz_t, v_ref[rows, :])

    @pl.when(direction == _BWD)
    def _backward_states():
        first = (nblk - 1 - blk) * n_chunks
        for ci in reversed(range(n_chunks)):
            sb_ref[first + ci] = state_ref[...].astype(BF16)
            absorb(ci, _BWD)

    @pl.when(direction == _FWD)
    def _forward():
        first = blk * n_chunks
        for ci in range(n_chunks):
            rows = pl.ds(ci * c, c)
            q = q_ref[rows, :]
            s = lax.dot_general(q, k_ref[rows, :], _NT, preferred_element_type=F32) * dmat_ref[...]
            o_scr[rows, :] = (_mm(s.astype(BF16), v_ref[rows, :])
                              + _mm(q, state_ref[...].astype(BF16)) * xi_ref[_FWD]
                              + _mm(q, sb_ref[first + ci]) * xi_ref[_BWD])
            absorb(ci, _FWD)
            for r in range(ci * c, (ci + 1) * c, RET_NORM_ROWS):
                o = o_scr[pl.ds(r, RET_NORM_ROWS), :]
                dev = o - jnp.mean(o, axis=-1, keepdims=True)
                on = dev * lax.rsqrt(jnp.mean(dev * dev, axis=-1, keepdims=True) + EPS)
                gate = gate_ref[pl.ds(r, RET_NORM_ROWS), :].astype(F32)
                o_ref[pl.ds(r, RET_NORM_ROWS), :] = (gate * on).astype(BF16)


def _ret_scan(q, k, v, gate, lg, batch, length):
    hv = RET_HEADS * RET_DV
    assert length % RET_BLOCK == 0
    nblk = length // RET_BLOCK

    def sweep(b, h, d, i):
        return (b, h, d * i + (1 - d) * (nblk - 1 - i), 0)

    def fwd_only(b, h, d, i):
        return (b, h, d * i, 0)

    out = pl.pallas_call(
        _ret_scan_body,
        grid=(batch, RET_HEADS, 2, nblk),
        in_specs=[pl.BlockSpec(memory_space=pltpu.SMEM),
                  pl.BlockSpec((None, None, RET_BLOCK, RET_DK), fwd_only),
                  pl.BlockSpec((None, None, RET_BLOCK, RET_DK), sweep),
                  pl.BlockSpec((None, None, RET_BLOCK, RET_DV), sweep),
                  pl.BlockSpec((None, None, RET_BLOCK, RET_DV), fwd_only)],
        out_specs=pl.BlockSpec((None, RET_BLOCK, RET_DV), lambda b, h, d, i: (b, d * i, h)),
        out_shape=jax.ShapeDtypeStruct((batch, length, hv), BF16),
        scratch_shapes=[pltpu.VMEM((RET_DK, RET_DV), F32),
                        pltpu.VMEM((length // RET_CHUNK, RET_DK, RET_DV), BF16),
                        pltpu.VMEM((RET_BLOCK, RET_DV), F32),
                        pltpu.VMEM((RET_CHUNK, RET_CHUNK), F32),
                        pltpu.VMEM((2, RET_CHUNK, RET_DV), F32),
                        pltpu.VMEM((2, RET_CHUNK, RET_DK), F32),
                        pltpu.VMEM((2, 1, RET_DV), F32)],
        compiler_params=_params("parallel", "parallel", "arbitrary", "arbitrary"),
        name="ret_scan",
    )(lg, q, k, v, gate)
    return out.reshape(batch * length, hv)


def _swap_halves(a):
    half = a.shape[-1] // 2
    return jnp.concatenate([a[..., half:], a[..., :half]], axis=-1)


def _rope_slot_layout(a):
    pe = a[..., QK_NOPE_DIM:]
    return jnp.concatenate([a[..., :QK_NOPE_DIM], pe, _swap_halves(pe)], axis=-1)


def _mla_weights(w_dq, q_a_norm, w_uq, w_dkv, kv_a_norm, w_ukv, q_norm, k_norm):
    uq = w_uq.reshape(Q_LORA_RANK, MLA_HEADS, QK_HEAD_DIM)
    uq = _rope_slot_layout(uq).reshape(Q_LORA_RANK, MLA_HEADS * QK_SLOT)
    pe = w_dkv[:, KV_LORA_RANK:]
    dkv = jnp.concatenate([w_dkv[:, :KV_LORA_RANK], pe, _swap_halves(pe)], axis=-1)
    ukv = w_ukv.reshape(KV_LORA_RANK, MLA_HEADS, QK_NOPE_DIM + V_HEAD_DIM)
    wk = ukv[..., :QK_NOPE_DIM].reshape(KV_LORA_RANK, MLA_HEADS * QK_NOPE_DIM)
    wvt = ukv[..., QK_NOPE_DIM:].reshape(KV_LORA_RANK, MLA_HEADS * V_HEAD_DIM).T
    return {
        "dq": w_dq.astype(BF16), "qa": q_a_norm[None, :], "uqt": uq.T.astype(BF16),
        "dkv": dkv.astype(BF16), "kva": kv_a_norm[None, :],
        "k": wk.astype(BF16), "vt": wvt.astype(BF16),
        "qgt": jnp.broadcast_to(_rope_slot_layout(q_norm)[:, None], (QK_SLOT, ROW_TILE)),
        "kg": _rope_slot_layout(k_norm)[None, :],
    }


def _rope_tables(length, dim):
    inv = ROPE_BASE ** (-jnp.arange(0, dim, 2, dtype=F32) / dim)
    ang = jnp.arange(length, dtype=F32)[:, None] * inv[None, :]
    return jnp.cos(ang), jnp.sin(ang)


def _trunk(x, p):
    batch, length, d = x.shape
    assert d == D_MODEL and length % ROW_TILE == 0 and length % FFN_FIRST_ROW_TILE == 0
    x = x.reshape(batch * length, d)
    cos_m, sin_m = _rope_tables(length, QK_ROPE_DIM)
    cs_mla = jnp.concatenate([cos_m, cos_m, -sin_m, sin_m], axis=-1)
    cos_r, sin_r = _rope_tables(length, RET_DK)
    for i in range(DEPTH):
        j = i // 2
        ng = p["norm_g"][i]
        f = p["ffn"][i]
        h = _ffn_first(x, ng[0:1], f[0])
        if i % 2 == 0:
            w = p["mla"][j]
            qt, k, vt = _mla_proj(h, ng[1:2], w, cs_mla, batch, length)
            o = _attention(qt, k, vt, batch, length)
        else:
            w = p["ret"][j]
            q, k, v, gate = _ret_proj(h, ng[1:2], w["in"], w["gn"], cos_r, sin_r, batch, length)
            o = _ret_scan(q, k, v, gate, w["lg"], batch, length)
        x = _ffn_second(h, o, w["o"], ng[2:3], f[1], ng[3:4])
    return x.reshape(batch, length, d)


def kernel(x_prompt, x_sample, norm_g, ffn_w_gate, ffn_w_up, ffn_w_down, mla_w_dq, mla_q_a_norm, mla_w_uq, mla_w_dkv, mla_kv_a_norm, mla_w_ukv, mla_q_norm, mla_k_norm, mla_w_o, ret_w_in, ret_decay_fwd, ret_decay_bwd, ret_gn_g, ret_w_o):
    p = {
        "norm_g": norm_g,
        "ffn": [[{"g": ffn_w_gate[i, s].astype(BF16), "u": ffn_w_up[i, s].astype(BF16),
                  "d": ffn_w_down[i, s].astype(BF16)} for s in range(2)] for i in range(DEPTH)],
        "mla": [], "ret": [],
    }
    for j in range(mla_w_dq.shape[0]):
        w = _mla_weights(mla_w_dq[j], mla_q_a_norm[j], mla_w_uq[j], mla_w_dkv[j], mla_kv_a_norm[j],
                         mla_w_ukv[j], mla_q_norm[j], mla_k_norm[j])
        w["o"] = mla_w_o[j].astype(BF16)
        p["mla"].append(w)
    for j in range(ret_w_in.shape[0]):
        lg = jnp.stack([jnp.log1p(-jnp.exp(ret_decay_bwd[j].astype(F32))),
                        jnp.log1p(-jnp.exp(ret_decay_fwd[j].astype(F32)))])
        p["ret"].append({"in": ret_w_in[j].astype(BF16), "lg": lg, "gn": ret_gn_g[j][None, :],
                         "o": ret_w_o[j].astype(BF16)})
    return _trunk(x_prompt, p), _trunk(x_sample, p)
```

```python
import math

import jax
import jax.numpy as jnp
from jax import lax
from jax.experimental import pallas as pl
from jax.experimental.pallas import tpu as pltpu

F32 = jnp.float32
BF16 = jnp.bfloat16

D_MODEL = 1024
DEPTH = 4
D_FF = 2816
MLA_HEADS = 8
QK_NOPE_DIM = 128
QK_ROPE_DIM = 64
QK_HEAD_DIM = QK_NOPE_DIM + QK_ROPE_DIM
V_HEAD_DIM = 128
Q_LORA_RANK = 384
KV_LORA_RANK = 256
ROPE_BASE = 10000.0
RET_HEADS = 4
RET_DK = D_MODEL // RET_HEADS
RET_DV = 2 * D_MODEL // RET_HEADS
EPS = 1e-6

LANES_V7X = 128
MXU_DIM_V7X = 256
VMEM_LIMIT_BYTES_V7X = 56 * 1024 * 1024

QK_SLOT = QK_NOPE_DIM + 2 * QK_ROPE_DIM
assert QK_SLOT == MXU_DIM_V7X
BF16_SUBLANES_V7X = 16
V_SLOT = V_HEAD_DIM + BF16_SUBLANES_V7X

ROW_TILE = 512
FFN_FIRST_ROW_TILE = 1024
FFN_SECOND_ROW_TILE = 512
ATTN_TQ = 512
ATTN_TK = 512
ATTN_LOOKAHEAD = 2
ATTN_BUFFERS = 4
ATTN_Q_PER_TRIP = 2
RET_CHUNK = MXU_DIM_V7X
RET_BLOCK = 2048
RET_NORM_ROWS = 64

_NT = (((1,), (1,)), ((), ()))


def _params(*sem):
    return pltpu.CompilerParams(dimension_semantics=sem, vmem_limit_bytes=VMEM_LIMIT_BYTES_V7X)


def _resident(shape):
    return pl.BlockSpec(shape, lambda *_: (0,) * len(shape), pipeline_mode=pl.Buffered(1))


def _rms(x, g):
    return x * lax.rsqrt(jnp.mean(x * x, axis=-1, keepdims=True) + EPS) * g


def _mm(a, b):
    return jnp.dot(a, b, preferred_element_type=F32)


def _swiglu_half(x, g_ref, wg_ref, wu_ref, wd_ref):
    xn = _rms(x, g_ref[...]).astype(BF16)
    gate = _mm(xn, wg_ref[...])
    up = _mm(xn, wu_ref[...])
    h = (gate * jax.nn.sigmoid(gate) * up).astype(BF16)
    return x + 0.5 * _mm(h, wd_ref[...])


def _ffn_first_body(x_ref, g_ref, wg_ref, wu_ref, wd_ref, o_ref):
    o_ref[...] = _swiglu_half(x_ref[...], g_ref, wg_ref, wu_ref, wd_ref)


def _ffn_second_body(x_ref, a_ref, wo_ref, g_ref, wg_ref, wu_ref, wd_ref, gp_ref, o_ref):
    h = x_ref[...] + _mm(a_ref[...], wo_ref[...])
    o_ref[...] = _rms(_swiglu_half(h, g_ref, wg_ref, wu_ref, wd_ref), gp_ref[...])


def _ffn_first(x, g, f):
    n, d = x.shape
    row = pl.BlockSpec((FFN_FIRST_ROW_TILE, d), lambda i: (i, 0))
    return pl.pallas_call(
        _ffn_first_body,
        grid=(n // FFN_FIRST_ROW_TILE,),
        in_specs=[row, _resident((1, d)), _resident(f["g"].shape), _resident(f["u"].shape),
                  _resident(f["d"].shape)],
        out_specs=row,
        out_shape=jax.ShapeDtypeStruct((n, d), F32),
        compiler_params=_params("parallel"),
        name="ffn_first",
    )(x, g, f["g"], f["u"], f["d"])


def _ffn_second(x, a, w_o, g, f, g_post):
    n, d = x.shape
    row = pl.BlockSpec((FFN_SECOND_ROW_TILE, d), lambda i: (i, 0))
    return pl.pallas_call(
        _ffn_second_body,
        grid=(n // FFN_SECOND_ROW_TILE,),
        in_specs=[row, pl.BlockSpec((FFN_SECOND_ROW_TILE, a.shape[1]), lambda i: (i, 0)), _resident(w_o.shape),
                  _resident((1, d)), _resident(f["g"].shape), _resident(f["u"].shape),
                  _resident(f["d"].shape), _resident((1, d))],
        out_specs=row,
        out_shape=jax.ShapeDtypeStruct((n, d), F32),
        compiler_params=_params("parallel"),
        name="ffn_second",
    )(x, a, w_o, g, f["g"], f["u"], f["d"], g_post)


def _rope_slot(t, gain, cs):
    a = t * gain * cs
    lane = lax.broadcasted_iota(jnp.int32, a.shape, 1)
    return jnp.where(lane < QK_ROPE_DIM, a + pltpu.roll(a, QK_ROPE_DIM, axis=1), 0.0)


def _sumsq_rope(t):
    lane = lax.broadcasted_iota(jnp.int32, t.shape, 1)
    return jnp.sum(jnp.where(lane < QK_ROPE_DIM, t * t, 0.0), axis=-1, keepdims=True)


def _mla_proj_body(h_ref, g_ref, wdq_ref, qa_ref, wuqt_ref, wdkv_ref, kva_ref, wk_ref, wvt_ref,
                   qgt_ref, kg_ref, cs_ref, cst_ref, qt_ref, k_ref, vt_ref):
    hn = _rms(h_ref[...], g_ref[...]).astype(BF16)
    cs = cs_ref[...]
    kg = kg_ref[...]
    inv_dim = 1.0 / QK_HEAD_DIM

    ckv_full = _mm(hn, wdkv_ref[...])
    ckv = _rms(ckv_full[:, :KV_LORA_RANK], kva_ref[...]).astype(BF16)
    vt = lax.dot_general(wvt_ref[...], ckv, _NT, preferred_element_type=F32).astype(BF16)
    for hd in range(MLA_HEADS):
        vt_ref[hd * V_SLOT:hd * V_SLOT + V_HEAD_DIM, :] = vt[hd * V_HEAD_DIM:(hd + 1) * V_HEAD_DIM, :]
        vt_ref[hd * V_SLOT + V_HEAD_DIM:(hd + 1) * V_SLOT, :] = jnp.ones((BF16_SUBLANES_V7X, vt.shape[1]), BF16)
    k_pe_raw = ckv_full[:, KV_LORA_RANK:]
    k_pe_ss = _sumsq_rope(k_pe_raw)
    k_pe = _rope_slot(k_pe_raw, kg[:, QK_NOPE_DIM:], cs)
    cq = _rms(_mm(hn, wdq_ref[...]), qa_ref[...]).astype(BF16)
    scale = QK_HEAD_DIM ** -0.5 * math.log2(math.e)
    cst = cst_ref[...]
    qgt = qgt_ref[...]
    for hd in range(MLA_HEADS):
        if hd % 2 == 0:
            k_pair = _mm(ckv, wk_ref[:, hd * QK_NOPE_DIM:(hd + 2) * QK_NOPE_DIM])
        kn = k_pair[:, (hd % 2) * QK_NOPE_DIM:(hd % 2 + 1) * QK_NOPE_DIM]
        r = lax.rsqrt((jnp.sum(kn * kn, axis=-1, keepdims=True) + k_pe_ss) * inv_dim + EPS)
        k_ref[hd, :, :QK_NOPE_DIM] = (kn * r * kg[:, :QK_NOPE_DIM]).astype(BF16)
        k_ref[hd, :, QK_NOPE_DIM:] = (k_pe * r).astype(BF16)
        base = hd * QK_SLOT
        qt = lax.dot_general(wuqt_ref[base:base + QK_SLOT, :], cq, _NT, preferred_element_type=F32)
        qn, qp = qt[:QK_NOPE_DIM], qt[QK_NOPE_DIM:]
        pe = qp[:QK_ROPE_DIM]
        ss = jnp.sum(qn * qn, axis=0, keepdims=True) + jnp.sum(pe * pe, axis=0, keepdims=True)
        r = lax.rsqrt(ss * inv_dim + EPS) * scale
        a = qp * qgt[QK_NOPE_DIM:] * cst
        qt_ref[base:base + QK_NOPE_DIM, :] = (qn * r * qgt[:QK_NOPE_DIM]).astype(BF16)
        qt_ref[base + QK_NOPE_DIM:base + QK_HEAD_DIM, :] = (
            (a[:QK_ROPE_DIM] + a[QK_ROPE_DIM:]) * r).astype(BF16)
        qt_ref[base + QK_HEAD_DIM:base + QK_SLOT, :] = jnp.zeros((QK_SLOT - QK_HEAD_DIM, qt.shape[1]), BF16)


def _mla_proj(h, g, w, cs, batch, length):
    n, d = h.shape
    nblk = length // ROW_TILE
    slots = MLA_HEADS * QK_SLOT
    dv = MLA_HEADS * V_SLOT
    row = lambda width: pl.BlockSpec((ROW_TILE, width), lambda i: (i, 0))
    return pl.pallas_call(
        _mla_proj_body,
        grid=(n // ROW_TILE,),
        in_specs=[row(d), _resident((1, d)), _resident(w["dq"].shape), _resident((1, Q_LORA_RANK)),
                  _resident(w["uqt"].shape), _resident(w["dkv"].shape), _resident((1, KV_LORA_RANK)),
                  _resident(w["k"].shape), _resident(w["vt"].shape),
                  _resident((QK_SLOT, ROW_TILE)), _resident((1, QK_SLOT)),
                  pl.BlockSpec((ROW_TILE, LANES_V7X), lambda i: (i % nblk, 0)),
                  pl.BlockSpec((LANES_V7X, ROW_TILE), lambda i: (0, i % nblk))],
        out_specs=[pl.BlockSpec((None, slots, ROW_TILE), lambda i: (i // nblk, 0, i % nblk)),
                   pl.BlockSpec((None, MLA_HEADS, ROW_TILE, QK_SLOT), lambda i: (i // nblk, 0, i % nblk, 0)),
                   pl.BlockSpec((None, dv, ROW_TILE), lambda i: (i // nblk, 0, i % nblk))],
        out_shape=[jax.ShapeDtypeStruct((batch, slots, length), BF16),
                   jax.ShapeDtypeStruct((batch, MLA_HEADS, length, QK_SLOT), BF16),
                   jax.ShapeDtypeStruct((batch, dv, length), BF16)],
        compiler_params=_params("parallel"),
        name="mla_proj",
    )(h, g, w["dq"], w["qa"], w["uqt"], w["dkv"], w["kva"], w["k"], w["vt"], w["qgt"], w["kg"], cs, cs.T)


def _attn_body(qt_ref, k_ref, vt_ref, o_ref, s_ref, smax_ref, m_ref, acc_ref):
    length = k_ref.shape[0]
    n_chunks = length // ATTN_TK
    n_q = length // ATTN_TQ

    def scores(q_tile, j, slot):
        q_off = pl.multiple_of(q_tile * ATTN_TQ, ATTN_TQ)
        st = _mm(k_ref[j * ATTN_TK:(j + 1) * ATTN_TK, :], qt_ref[:, pl.ds(q_off, ATTN_TQ)])
        s_ref[slot] = st
        smax_ref[slot] = jnp.max(st, axis=0, keepdims=True)

    def update(j, slot):
        m_old = m_ref[...]
        m_new = jnp.maximum(m_old, smax_ref[slot])
        p = jnp.exp2(s_ref[slot] - m_new).astype(BF16)
        alpha = jnp.exp2(m_old - m_new)
        acc_ref[...] = alpha * acc_ref[...] + _mm(vt_ref[:, j * ATTN_TK:(j + 1) * ATTN_TK], p)
        m_ref[...] = m_new

    for j in range(ATTN_LOOKAHEAD):
        scores(0, j, j)

    def q_tile(tile):
        m_ref[...] = jnp.full(m_ref.shape, -jnp.inf, F32)
        acc_ref[...] = jnp.zeros(acc_ref.shape, F32)
        next_tile = lax.rem(tile + 1, n_q)
        for j in range(n_chunks):
            ahead = j + ATTN_LOOKAHEAD
            scores(tile if ahead < n_chunks else next_tile, ahead % n_chunks, ahead % ATTN_BUFFERS)
            update(j, j % ATTN_BUFFERS)
        o = acc_ref[:V_HEAD_DIM, :] / acc_ref[V_HEAD_DIM:V_HEAD_DIM + 1, :]
        o_ref[pl.ds(pl.multiple_of(tile * ATTN_TQ, ATTN_TQ), ATTN_TQ), :] = o.T.astype(BF16)

    def trip(t, carry):
        for i in range(ATTN_Q_PER_TRIP):
            q_tile(t * ATTN_Q_PER_TRIP + i)
        return carry

    lax.fori_loop(0, n_q // ATTN_Q_PER_TRIP, trip, 0)


def _attention(qt, k, vt, batch, length):
    assert length % (ATTN_TQ * ATTN_Q_PER_TRIP) == 0 and length % (ATTN_TK * ATTN_BUFFERS) == 0
    out = pl.pallas_call(
        _attn_body,
        grid=(batch, MLA_HEADS),
        in_specs=[pl.BlockSpec((None, QK_SLOT, length), lambda b, h: (b, h, 0)),
                  pl.BlockSpec((None, None, length, QK_SLOT), lambda b, h: (b, h, 0, 0)),
                  pl.BlockSpec((None, V_SLOT, length), lambda b, h: (b, h, 0))],
        out_specs=pl.BlockSpec((None, length, V_HEAD_DIM), lambda b, h: (b, 0, h)),
        out_shape=jax.ShapeDtypeStruct((batch, length, MLA_HEADS * V_HEAD_DIM), BF16),
        scratch_shapes=[pltpu.VMEM((ATTN_BUFFERS, ATTN_TK, ATTN_TQ), F32),
                        pltpu.VMEM((ATTN_BUFFERS, 1, ATTN_TQ), F32), pltpu.VMEM((1, ATTN_TQ), F32),
                        pltpu.VMEM((V_SLOT, ATTN_TQ), F32)],
        compiler_params=_params("parallel", "parallel"),
        name="mla_attn",
    )(qt, k, vt)
    return out.reshape(batch * length, MLA_HEADS * V_HEAD_DIM)


def _ret_proj_body(h_ref, g_ref, w_ref, gn_ref, cos_ref, sin_ref, q_ref, k_ref, v_ref, gate_ref):
    hn = _rms(h_ref[...], g_ref[...]).astype(BF16)
    cos = cos_ref[...]
    sin = sin_ref[...]
    hq = RET_HEADS * RET_DK
    hv = RET_HEADS * RET_DV
    half = RET_DK // 2

    def rope_store(dst, col0, mult):
        for hd in range(RET_HEADS):
            x = _mm(hn, w_ref[:, col0 + hd * RET_DK:col0 + (hd + 1) * RET_DK])
            x1, x2 = x[:, :half], x[:, half:]
            dst[hd, :, :half] = ((x1 * cos - x2 * sin) * mult).astype(BF16)
            dst[hd, :, half:] = ((x2 * cos + x1 * sin) * mult).astype(BF16)

    rope_store(q_ref, 0, 1.0)
    rope_store(k_ref, hq, RET_DK ** -0.5)
    v = _mm(hn, w_ref[:, 2 * hq:2 * hq + hv]).astype(BF16)
    gate = _mm(hn, w_ref[:, 2 * hq + hv:])
    gate = (gate * jax.nn.sigmoid(gate) * gn_ref[...]).astype(BF16)
    for hd in range(RET_HEADS):
        v_ref[hd] = v[:, hd * RET_DV:(hd + 1) * RET_DV]
        gate_ref[hd] = gate[:, hd * RET_DV:(hd + 1) * RET_DV]


def _ret_proj(h, g, w_in, gn_g, cos, sin, batch, length):
    n, d = h.shape
    nblk = length // ROW_TILE
    hv = RET_HEADS * RET_DV
    row = lambda width: pl.BlockSpec((ROW_TILE, width), lambda i: (i, 0))
    pos = pl.BlockSpec((ROW_TILE, RET_DK // 2), lambda i: (i % nblk, 0))
    heads = lambda dim: pl.BlockSpec((None, RET_HEADS, ROW_TILE, dim), lambda i: (i // nblk, 0, i % nblk, 0))
    shape = lambda dim: jax.ShapeDtypeStruct((batch, RET_HEADS, length, dim), BF16)
    return pl.pallas_call(
        _ret_proj_body,
        grid=(n // ROW_TILE,),
        in_specs=[row(d), _resident((1, d)), _resident(w_in.shape), _resident((1, hv)), pos, pos],
        out_specs=[heads(RET_DK), heads(RET_DK), heads(RET_DV), heads(RET_DV)],
        out_shape=[shape(RET_DK), shape(RET_DK), shape(RET_DV), shape(RET_DV)],
        compiler_params=_params("parallel"),
        name="ret_proj",
    )(h, g, w_in, gn_g, cos, sin)


_BWD, _FWD = 0, 1


def _ret_scan_body(lg_ref, q_ref, k_ref, v_ref, gate_ref, o_ref,
                   state_ref, sb_ref, o_scr, dmat_ref, xi_ref, zeta_ref, gc_ref):
    hd = pl.program_id(1)
    direction = pl.program_id(2)
    blk = pl.program_id(3)
    nblk = pl.num_programs(3)
    c = RET_CHUNK
    n_chunks = RET_BLOCK // c

    @pl.when(blk == 0)
    def _reset_state():
        state_ref[...] = jnp.zeros(state_ref.shape, F32)

    @pl.when((blk == 0) & (direction == _BWD))
    def _decay_tables():
        lg_b = lg_ref[_BWD, hd]
        lg_f = lg_ref[_FWD, hd]
        row = lax.broadcasted_iota(jnp.int32, (c, c), 0)
        col = lax.broadcasted_iota(jnp.int32, (c, c), 1)
        dmat_ref[...] = jnp.where(row >= col,
                                  jnp.exp(lg_f * jnp.maximum(row - col, 0).astype(F32)),
                                  jnp.exp(lg_b * jnp.maximum(col - row, 0).astype(F32)))
        qi = lax.broadcasted_iota(jnp.int32, xi_ref.shape[1:], 0)
        xi_ref[_FWD] = jnp.exp(lg_f * (qi + 1).astype(F32))
        xi_ref[_BWD] = jnp.exp(lg_b * (c - qi).astype(F32))
        kj = lax.broadcasted_iota(jnp.int32, zeta_ref.shape[1:], 0)
        zeta_ref[_FWD] = jnp.exp(lg_f * (c - 1 - kj).astype(F32))
        zeta_ref[_BWD] = jnp.exp(lg_b * kj.astype(F32))
        gc_ref[_FWD] = jnp.exp(jnp.zeros(gc_ref.shape[1:], F32) + lg_f * c)
        gc_ref[_BWD] = jnp.exp(jnp.zeros(gc_ref.shape[1:], F32) + lg_b * c)

    def absorb(ci, d):
        rows = pl.ds(ci * c, c)
        kz_t = (k_ref[rows, :].astype(F32) * zeta_ref[d]).T.astype(BF16)
        state_ref[...] = state_ref[...] * gc_ref[d] + _mm(kz_t, v_ref[rows, :])

    @pl.when(direction == _BWD)
    def _backward_states():
        first = (nblk - 1 - blk) * n_chunks
        for ci in reversed(range(n_chunks)):
            sb_ref[first + ci] = state_ref[...].astype(BF16)
            absorb(ci, _BWD)

    @pl.when(direction == _FWD)
    def _forward():
        first = blk * n_chunks
        for ci in range(n_chunks):
            rows = pl.ds(ci * c, c)
            q = q_ref[rows, :]
            s = lax.dot_general(q, k_ref[rows, :], _NT, preferred_element_type=F32) * dmat_ref[...]
            o_scr[rows, :] = (_mm(s.astype(BF16), v_ref[rows, :])
                              + _mm(q, state_ref[...].astype(BF16)) * xi_ref[_FWD]
                              + _mm(q, sb_ref[first + ci]) * xi_ref[_BWD])
            absorb(ci, _FWD)
            for r in range(ci * c, (ci + 1) * c, RET_NORM_ROWS):
                o = o_scr[pl.ds(r, RET_NORM_ROWS), :]
                dev = o - jnp.mean(o, axis=-1, keepdims=True)
                on = dev * lax.rsqrt(jnp.mean(dev * dev, axis=-1, keepdims=True) + EPS)
                gate = gate_ref[pl.ds(r, RET_NORM_ROWS), :].astype(F32)
                o_ref[pl.ds(r, RET_NORM_ROWS), :] = (gate * on).astype(BF16)


def _ret_scan(q, k, v, gate, lg, batch, length):
    hv = RET_HEADS * RET_DV
    assert length % RET_BLOCK == 0
    nblk = length // RET_BLOCK

    def sweep(b, h, d, i):
        return (b, h, d * i + (1 - d) * (nblk - 1 - i), 0)

    def fwd_only(b, h, d, i):
        return (b, h, d * i, 0)

    out = pl.pallas_call(
        _ret_scan_body,
        grid=(batch, RET_HEADS, 2, nblk),
        in_specs=[pl.BlockSpec(memory_space=pltpu.SMEM),
                  pl.BlockSpec((None, None, RET_BLOCK, RET_DK), fwd_only),
                  pl.BlockSpec((None, None, RET_BLOCK, RET_DK), sweep),
                  pl.BlockSpec((None, None, RET_BLOCK, RET_DV), sweep),
                  pl.BlockSpec((None, None, RET_BLOCK, RET_DV), fwd_only)],
        out_specs=pl.BlockSpec((None, RET_BLOCK, RET_DV), lambda b, h, d, i: (b, d * i, h)),
        out_shape=jax.ShapeDtypeStruct((batch, length, hv), BF16),
        scratch_shapes=[pltpu.VMEM((RET_DK, RET_DV), F32),
                        pltpu.VMEM((length // RET_CHUNK, RET_DK, RET_DV), BF16),
                        pltpu.VMEM((RET_BLOCK, RET_DV), F32),
                        pltpu.VMEM((RET_CHUNK, RET_CHUNK), F32),
                        pltpu.VMEM((2, RET_CHUNK, RET_DV), F32),
                        pltpu.VMEM((2, RET_CHUNK, RET_DK), F32),
                        pltpu.VMEM((2, 1, RET_DV), F32)],
        compiler_params=_params("parallel", "parallel", "arbitrary", "arbitrary"),
        name="ret_scan",
    )(lg, q, k, v, gate)
    return out.reshape(batch * length, hv)


def _swap_halves(a):
    half = a.shape[-1] // 2
    return jnp.concatenate([a[..., half:], a[..., :half]], axis=-1)


def _rope_slot_layout(a):
    pe = a[..., QK_NOPE_DIM:]
    return jnp.concatenate([a[..., :QK_NOPE_DIM], pe, _swap_halves(pe)], axis=-1)


def _mla_weights(w_dq, q_a_norm, w_uq, w_dkv, kv_a_norm, w_ukv, q_norm, k_norm):
    uq = w_uq.reshape(Q_LORA_RANK, MLA_HEADS, QK_HEAD_DIM)
    uq = _rope_slot_layout(uq).reshape(Q_LORA_RANK, MLA_HEADS * QK_SLOT)
    pe = w_dkv[:, KV_LORA_RANK:]
    dkv = jnp.concatenate([w_dkv[:, :KV_LORA_RANK], pe, _swap_halves(pe)], axis=-1)
    ukv = w_ukv.reshape(KV_LORA_RANK, MLA_HEADS, QK_NOPE_DIM + V_HEAD_DIM)
    wk = ukv[..., :QK_NOPE_DIM].reshape(KV_LORA_RANK, MLA_HEADS * QK_NOPE_DIM)
    wvt = ukv[..., QK_NOPE_DIM:].reshape(KV_LORA_RANK, MLA_HEADS * V_HEAD_DIM).T
    return {
        "dq": w_dq.astype(BF16), "qa": q_a_norm[None, :], "uqt": uq.T.astype(BF16),
        "dkv": dkv.astype(BF16), "kva": kv_a_norm[None, :],
        "k": wk.astype(BF16), "vt": wvt.astype(BF16),
        "qgt": jnp.broadcast_to(_rope_slot_layout(q_norm)[:, None], (QK_SLOT, ROW_TILE)),
        "kg": _rope_slot_layout(k_norm)[None, :],
    }


def _rope_tables(length, dim):
    inv = ROPE_BASE ** (-jnp.arange(0, dim, 2, dtype=F32) / dim)
    ang = jnp.arange(length, dtype=F32)[:, None] * inv[None, :]
    return jnp.cos(ang), jnp.sin(ang)


def _trunk(x, p):
    batch, length, d = x.shape
    assert d == D_MODEL and length % ROW_TILE == 0 and length % FFN_FIRST_ROW_TILE == 0
    x = x.reshape(batch * length, d)
    cos_m, sin_m = _rope_tables(length, QK_ROPE_DIM)
    cs_mla = jnp.concatenate([cos_m, cos_m, -sin_m, sin_m], axis=-1)
    cos_r, sin_r = _rope_tables(length, RET_DK)
    for i in range(DEPTH):
        j = i // 2
        ng = p["norm_g"][i]
        f = p["ffn"][i]
        h = _ffn_first(x, ng[0:1], f[0])
        if i % 2 == 0:
            w = p["mla"][j]
            qt, k, vt = _mla_proj(h, ng[1:2], w, cs_mla, batch, length)
            o = _attention(qt, k, vt, batch, length)
        else:
            w = p["ret"][j]
            q, k, v, gate = _ret_proj(h, ng[1:2], w["in"], w["gn"], cos_r, sin_r, batch, length)
            o = _ret_scan(q, k, v, gate, w["lg"], batch, length)
        x = _ffn_second(h, o, w["o"], ng[2:3], f[1], ng[3:4])
    return x.reshape(batch, length, d)


def kernel(x_prompt, x_sample, norm_g, ffn_w_gate, ffn_w_up, ffn_w_down, mla_w_dq, mla_q_a_norm, mla_w_uq, mla_w_dkv, mla_kv_a_norm, mla_w_ukv, mla_q_norm, mla_k_norm, mla_w_o, ret_w_in, ret_decay_fwd, ret_decay_bwd, ret_gn_g, ret_w_o):
    p = {
        "norm_g": norm_g,
        "ffn": [[{"g": ffn_w_gate[i, s].astype(BF16), "u": ffn_w_up[i, s].astype(BF16),
                  "d": ffn_w_down[i, s].astype(BF16)} for s in range(2)] for i in range(DEPTH)],
        "mla": [], "ret": [],
    }
    for j in range(mla_w_dq.shape[0]):
        w = _mla_weights(mla_w_dq[j], mla_q_a_norm[j], mla_w_uq[j], mla_w_dkv[j], mla_kv_a_norm[j],
                         mla_w_ukv[j], mla_q_norm[j], mla_k_norm[j])
        w["o"] = mla_w_o[j].astype(BF16)
        p["mla"].append(w)
    for j in range(ret_w_in.shape[0]):
        lg = jnp.stack([jnp.log1p(-jnp.exp(ret_decay_bwd[j].astype(F32))),
                        jnp.log1p(-jnp.exp(ret_decay_fwd[j].astype(F32)))])
        p["ret"].append({"in": ret_w_in[j].astype(BF16), "lg": lg, "gn": ret_gn_g[j][None, :],
                         "o": ret_w_o[j].astype(BF16)})
    return _trunk(x_prompt, p), _trunk(x_sample, p)
```

```python
import math

import jax
import jax.numpy as jnp
from jax import lax
from jax.experimental import pallas as pl
from jax.experimental.pallas import tpu as pltpu

F32 = jnp.float32
BF16 = jnp.bfloat16

D_MODEL = 1024
DEPTH = 4
D_FF = 2816
MLA_HEADS = 8
QK_NOPE_DIM = 128
QK_ROPE_DIM = 64
QK_HEAD_DIM = QK_NOPE_DIM + QK_ROPE_DIM
V_HEAD_DIM = 128
Q_LORA_RANK = 384
KV_LORA_RANK = 256
ROPE_BASE = 10000.0
RET_HEADS = 4
RET_DK = D_MODEL // RET_HEADS
RET_DV = 2 * D_MODEL // RET_HEADS
EPS = 1e-6

LANES_V7X = 128
MXU_DIM_V7X = 256
VMEM_LIMIT_BYTES_V7X = 56 * 1024 * 1024

QK_SLOT = QK_NOPE_DIM + 2 * QK_ROPE_DIM
assert QK_SLOT == MXU_DIM_V7X
BF16_SUBLANES_V7X = 16
V_SLOT = V_HEAD_DIM + BF16_SUBLANES_V7X

ROW_TILE = 512
FFN_FIRST_ROW_TILE = 1024
FFN_SECOND_ROW_TILE = 1024
FFN_SECOND_VMEM_LIMIT_BYTES_V7X = 62 * 1024 * 1024
ATTN_TQ = 512
ATTN_TK = 512
ATTN_LOOKAHEAD = 2
ATTN_BUFFERS = 4
ATTN_Q_PER_TRIP = 2
RET_CHUNK = MXU_DIM_V7X
RET_BLOCK = 2048
RET_NORM_ROWS = 64

_NT = (((1,), (1,)), ((), ()))


def _params(*sem, vmem_limit_bytes=VMEM_LIMIT_BYTES_V7X):
    return pltpu.CompilerParams(dimension_semantics=sem, vmem_limit_bytes=vmem_limit_bytes)


def _resident(shape):
    return pl.BlockSpec(shape, lambda *_: (0,) * len(shape), pipeline_mode=pl.Buffered(1))


def _rms(x, g):
    return x * lax.rsqrt(jnp.mean(x * x, axis=-1, keepdims=True) + EPS) * g


def _mm(a, b):
    return jnp.dot(a, b, preferred_element_type=F32)


def _swiglu_half(x, g_ref, wg_ref, wu_ref, wd_ref):
    xn = _rms(x, g_ref[...]).astype(BF16)
    gate = _mm(xn, wg_ref[...])
    up = _mm(xn, wu_ref[...])
    h = (gate * jax.nn.sigmoid(gate) * up).astype(BF16)
    return x + 0.5 * _mm(h, wd_ref[...])


def _ffn_first_body(x_ref, g_ref, wg_ref, wu_ref, wd_ref, o_ref):
    o_ref[...] = _swiglu_half(x_ref[...], g_ref, wg_ref, wu_ref, wd_ref)


def _ffn_second_body(x_ref, a_ref, wo_ref, g_ref, wg_ref, wu_ref, wd_ref, gp_ref, o_ref):
    h = x_ref[...] + _mm(a_ref[...], wo_ref[...])
    o_ref[...] = _rms(_swiglu_half(h, g_ref, wg_ref, wu_ref, wd_ref), gp_ref[...])


def _ffn_first(x, g, f):
    n, d = x.shape
    row = pl.BlockSpec((FFN_FIRST_ROW_TILE, d), lambda i: (i, 0))
    return pl.pallas_call(
        _ffn_first_body,
        grid=(n // FFN_FIRST_ROW_TILE,),
        in_specs=[row, _resident((1, d)), _resident(f["g"].shape), _resident(f["u"].shape),
                  _resident(f["d"].shape)],
        out_specs=row,
        out_shape=jax.ShapeDtypeStruct((n, d), F32),
        compiler_params=_params("parallel"),
        name="ffn_first",
    )(x, g, f["g"], f["u"], f["d"])


def _ffn_second(x, a, w_o, g, f, g_post):
    n, d = x.shape
    row = pl.BlockSpec((FFN_SECOND_ROW_TILE, d), lambda i: (i, 0))
    return pl.pallas_call(
        _ffn_second_body,
        grid=(n // FFN_SECOND_ROW_TILE,),
        in_specs=[row, pl.BlockSpec((FFN_SECOND_ROW_TILE, a.shape[1]), lambda i: (i, 0)), _resident(w_o.shape),
                  _resident((1, d)), _resident(f["g"].shape), _resident(f["u"].shape),
                  _resident(f["d"].shape), _resident((1, d))],
        out_specs=row,
        out_shape=jax.ShapeDtypeStruct((n, d), F32),
        compiler_params=_params("parallel", vmem_limit_bytes=FFN_SECOND_VMEM_LIMIT_BYTES_V7X),
        name="ffn_second",
    )(x, a, w_o, g, f["g"], f["u"], f["d"], g_post)


def _rope_slot(t, gain, cs):
    a = t * gain * cs
    lane = lax.broadcasted_iota(jnp.int32, a.shape, 1)
    return jnp.where(lane < QK_ROPE_DIM, a + pltpu.roll(a, QK_ROPE_DIM, axis=1), 0.0)


def _sumsq_rope(t):
    lane = lax.broadcasted_iota(jnp.int32, t.shape, 1)
    return jnp.sum(jnp.where(lane < QK_ROPE_DIM, t * t, 0.0), axis=-1, keepdims=True)


def _mla_proj_body(h_ref, g_ref, wdq_ref, qa_ref, wuqt_ref, wdkv_ref, kva_ref, wk_ref, wvt_ref,
                   qgt_ref, kg_ref, cs_ref, cst_ref, qt_ref, k_ref, vt_ref):
    hn = _rms(h_ref[...], g_ref[...]).astype(BF16)
    cs = cs_ref[...]
    kg = kg_ref[...]
    inv_dim = 1.0 / QK_HEAD_DIM

    ckv_full = _mm(hn, wdkv_ref[...])
    ckv = _rms(ckv_full[:, :KV_LORA_RANK], kva_ref[...]).astype(BF16)
    vt = lax.dot_general(wvt_ref[...], ckv, _NT, preferred_element_type=F32).astype(BF16)
    for hd in range(MLA_HEADS):
        vt_ref[hd * V_SLOT:hd * V_SLOT + V_HEAD_DIM, :] = vt[hd * V_HEAD_DIM:(hd + 1) * V_HEAD_DIM, :]
        vt_ref[hd * V_SLOT + V_HEAD_DIM:(hd + 1) * V_SLOT, :] = jnp.ones((BF16_SUBLANES_V7X, vt.shape[1]), BF16)
    k_pe_raw = ckv_full[:, KV_LORA_RANK:]
    k_pe_ss = _sumsq_rope(k_pe_raw)
    k_pe = _rope_slot(k_pe_raw, kg[:, QK_NOPE_DIM:], cs)
    cq = _rms(_mm(hn, wdq_ref[...]), qa_ref[...]).astype(BF16)
    scale = QK_HEAD_DIM ** -0.5 * math.log2(math.e)
    cst = cst_ref[...]
    qgt = qgt_ref[...]
    for hd in range(MLA_HEADS):
        if hd % 2 == 0:
            k_pair = _mm(ckv, wk_ref[:, hd * QK_NOPE_DIM:(hd + 2) * QK_NOPE_DIM])
        kn = k_pair[:, (hd % 2) * QK_NOPE_DIM:(hd % 2 + 1) * QK_NOPE_DIM]
        r = lax.rsqrt((jnp.sum(kn * kn, axis=-1, keepdims=True) + k_pe_ss) * inv_dim + EPS)
        k_ref[hd, :, :QK_NOPE_DIM] = (kn * r * kg[:, :QK_NOPE_DIM]).astype(BF16)
        k_ref[hd, :, QK_NOPE_DIM:] = (k_pe * r).astype(BF16)
        base = hd * QK_SLOT
        qt = lax.dot_general(wuqt_ref[base:base + QK_SLOT, :], cq, _NT, preferred_element_type=F32)
        qn, qp = qt[:QK_NOPE_DIM], qt[QK_NOPE_DIM:]
        pe = qp[:QK_ROPE_DIM]
        ss = jnp.sum(qn * qn, axis=0, keepdims=True) + jnp.sum(pe * pe, axis=0, keepdims=True)
        r = lax.rsqrt(ss * inv_dim + EPS) * scale
        a = qp * qgt[QK_NOPE_DIM:] * cst
        qt_ref[base:base + QK_NOPE_DIM, :] = (qn * r * qgt[:QK_NOPE_DIM]).astype(BF16)
        qt_ref[base + QK_NOPE_DIM:base + QK_HEAD_DIM, :] = (
            (a[:QK_ROPE_DIM] + a[QK_ROPE_DIM:]) * r).astype(BF16)
        qt_ref[base + QK_HEAD_DIM:base + QK_SLOT, :] = jnp.zeros((QK_SLOT - QK_HEAD_DIM, qt.shape[1]), BF16)


def _mla_proj(h, g, w, cs, batch, length):
    n, d = h.shape
    nblk = length // ROW_TILE
    slots = MLA_HEADS * QK_SLOT
    dv = MLA_HEADS * V_SLOT
    row = lambda width: pl.BlockSpec((ROW_TILE, width), lambda i: (i, 0))
    return pl.pallas_call(
        _mla_proj_body,
        grid=(n // ROW_TILE,),
        in_specs=[row(d), _resident((1, d)), _resident(w["dq"].shape), _resident((1, Q_LORA_RANK)),
                  _resident(w["uqt"].shape), _resident(w["dkv"].shape), _resident((1, KV_LORA_RANK)),
                  _resident(w["k"].shape), _resident(w["vt"].shape),
                  _resident((QK_SLOT, ROW_TILE)), _resident((1, QK_SLOT)),
                  pl.BlockSpec((ROW_TILE, LANES_V7X), lambda i: (i % nblk, 0)),
                  pl.BlockSpec((LANES_V7X, ROW_TILE), lambda i: (0, i % nblk))],
        out_specs=[pl.BlockSpec((None, slots, ROW_TILE), lambda i: (i // nblk, 0, i % nblk)),
                   pl.BlockSpec((None, MLA_HEADS, ROW_TILE, QK_SLOT), lambda i: (i // nblk, 0, i % nblk, 0)),
                   pl.BlockSpec((None, dv, ROW_TILE), lambda i: (i // nblk, 0, i % nblk))],
        out_shape=[jax.ShapeDtypeStruct((batch, slots, length), BF16),
                   jax.ShapeDtypeStruct((batch, MLA_HEADS, length, QK_SLOT), BF16),
                   jax.ShapeDtypeStruct((batch, dv, length), BF16)],
        compiler_params=_params("parallel"),
        name="mla_proj",
    )(h, g, w["dq"], w["qa"], w["uqt"], w["dkv"], w["kva"], w["k"], w["vt"], w["qgt"], w["kg"], cs, cs.T)


def _attn_body(qt_ref, k_ref, vt_ref, o_ref, s_ref, smax_ref, m_ref, acc_ref):
    length = k_ref.shape[0]
    n_chunks = length // ATTN_TK
    n_q = length // ATTN_TQ

    def scores(q_tile, j, slot):
        q_off = pl.multiple_of(q_tile * ATTN_TQ, ATTN_TQ)
        st = _mm(k_ref[j * ATTN_TK:(j + 1) * ATTN_TK, :], qt_ref[:, pl.ds(q_off, ATTN_TQ)])
        s_ref[slot] = st
        smax_ref[slot] = jnp.max(st, axis=0, keepdims=True)

    def update(j, slot):
        m_old = m_ref[...]
        m_new = jnp.maximum(m_old, smax_ref[slot])
        p = jnp.exp2(s_ref[slot] - m_new).astype(BF16)
        alpha = jnp.exp2(m_old - m_new)
        acc_ref[...] = alpha * acc_ref[...] + _mm(vt_ref[:, j * ATTN_TK:(j + 1) * ATTN_TK], p)
        m_ref[...] = m_new

    for j in range(ATTN_LOOKAHEAD):
        scores(0, j, j)

    def q_tile(tile):
        m_ref[...] = jnp.full(m_ref.shape, -jnp.inf, F32)
        acc_ref[...] = jnp.zeros(acc_ref.shape, F32)
        next_tile = lax.rem(tile + 1, n_q)
        for j in range(n_chunks):
            ahead = j + ATTN_LOOKAHEAD
            scores(tile if ahead < n_chunks else next_tile, ahead % n_chunks, ahead % ATTN_BUFFERS)
            update(j, j % ATTN_BUFFERS)
        o = acc_ref[:V_HEAD_DIM, :] / acc_ref[V_HEAD_DIM:V_HEAD_DIM + 1, :]
        o_ref[pl.ds(pl.multiple_of(tile * ATTN_TQ, ATTN_TQ), ATTN_TQ), :] = o.T.astype(BF16)

    def trip(t, carry):
        for i in range(ATTN_Q_PER_TRIP):
            q_tile(t * ATTN_Q_PER_TRIP + i)
        return carry

    lax.fori_loop(0, n_q // ATTN_Q_PER_TRIP, trip, 0)


def _attention(qt, k, vt, batch, length):
    assert length % (ATTN_TQ * ATTN_Q_PER_TRIP) == 0 and length % (ATTN_TK * ATTN_BUFFERS) == 0
    out = pl.pallas_call(
        _attn_body,
        grid=(batch, MLA_HEADS),
        in_specs=[pl.BlockSpec((None, QK_SLOT, length), lambda b, h: (b, h, 0)),
                  pl.BlockSpec((None, None, length, QK_SLOT), lambda b, h: (b, h, 0, 0)),
                  pl.BlockSpec((None, V_SLOT, length), lambda b, h: (b, h, 0))],
        out_specs=pl.BlockSpec((None, length, V_HEAD_DIM), lambda b, h: (b, 0, h)),
        out_shape=jax.ShapeDtypeStruct((batch, length, MLA_HEADS * V_HEAD_DIM), BF16),
        scratch_shapes=[pltpu.VMEM((ATTN_BUFFERS, ATTN_TK, ATTN_TQ), F32),
                        pltpu.VMEM((ATTN_BUFFERS, 1, ATTN_TQ), F32), pltpu.VMEM((1, ATTN_TQ), F32),
                        pltpu.VMEM((V_SLOT, ATTN_TQ), F32)],
        compiler_params=_params("parallel", "parallel"),
        name="mla_attn",
    )(qt, k, vt)
    return out.reshape(batch * length, MLA_HEADS * V_HEAD_DIM)


def _ret_proj_body(h_ref, g_ref, w_ref, gn_ref, cos_ref, sin_ref, q_ref, k_ref, v_ref, gate_ref):
    hn = _rms(h_ref[...], g_ref[...]).astype(BF16)
    cos = cos_ref[...]
    sin = sin_ref[...]
    hq = RET_HEADS * RET_DK
    hv = RET_HEADS * RET_DV
    half = RET_DK // 2

    def rope_store(dst, col0, mult):
        for hd in range(RET_HEADS):
            x = _mm(hn, w_ref[:, col0 + hd * RET_DK:col0 + (hd + 1) * RET_DK])
            x1, x2 = x[:, :half], x[:, half:]
            dst[hd, :, :half] = ((x1 * cos - x2 * sin) * mult).astype(BF16)
            dst[hd, :, half:] = ((x2 * cos + x1 * sin) * mult).astype(BF16)

    rope_store(q_ref, 0, 1.0)
    rope_store(k_ref, hq, RET_DK ** -0.5)
    v = _mm(hn, w_ref[:, 2 * hq:2 * hq + hv]).astype(BF16)
    gate = _mm(hn, w_ref[:, 2 * hq + hv:])
    gate = (gate * jax.nn.sigmoid(gate) * gn_ref[...]).astype(BF16)
    for hd in range(RET_HEADS):
        v_ref[hd] = v[:, hd * RET_DV:(hd + 1) * RET_DV]
        gate_ref[hd] = gate[:, hd * RET_DV:(hd + 1) * RET_DV]


def _ret_proj(h, g, w_in, gn_g, cos, sin, batch, length):
    n, d = h.shape
    nblk = length // ROW_TILE
    hv = RET_HEADS * RET_DV
    row = lambda width: pl.BlockSpec((ROW_TILE, width), lambda i: (i, 0))
    pos = pl.BlockSpec((ROW_TILE, RET_DK // 2), lambda i: (i % nblk, 0))
    heads = lambda dim: pl.BlockSpec((None, RET_HEADS, ROW_TILE, dim), lambda i: (i // nblk, 0, i % nblk, 0))
    shape = lambda dim: jax.ShapeDtypeStruct((batch, RET_HEADS, length, dim), BF16)
    return pl.pallas_call(
        _ret_proj_body,
        grid=(n // ROW_TILE,),
        in_specs=[row(d), _resident((1, d)), _resident(w_in.shape), _resident((1, hv)), pos, pos],
        out_specs=[heads(RET_DK), heads(RET_DK), heads(RET_DV), heads(RET_DV)],
        out_shape=[shape(RET_DK), shape(RET_DK), shape(RET_DV), shape(RET_DV)],
        compiler_params=_params("parallel"),
        name="ret_proj",
    )(h, g, w_in, gn_g, cos, sin)


_BWD, _FWD = 0, 1


def _ret_scan_body(lg_ref, q_ref, k_ref, v_ref, gate_ref, o_ref,
                   state_ref, sb_ref, o_scr, dmat_ref, xi_ref, zeta_ref, gc_ref):
    hd = pl.program_id(1)
    direction = pl.program_id(2)
    blk = pl.program_id(3)
    nblk = pl.num_programs(3)
    c = RET_CHUNK
    n_chunks = RET_BLOCK // c

    @pl.when(blk == 0)
    def _reset_state():
        state_ref[...] = jnp.zeros(state_ref.shape, F32)

    @pl.when((blk == 0) & (direction == _BWD))
    def _decay_tables():
        lg_b = lg_ref[_BWD, hd]
        lg_f = lg_ref[_FWD, hd]
        row = lax.broadcasted_iota(jnp.int32, (c, c), 0)
        col = lax.broadcasted_iota(jnp.int32, (c, c), 1)
        dmat_ref[...] = jnp.where(row >= col,
                                  jnp.exp(lg_f * jnp.maximum(row - col, 0).astype(F32)),
                                  jnp.exp(lg_b * jnp.maximum(col - row, 0).astype(F32)))
        qi = lax.broadcasted_iota(jnp.int32, xi_ref.shape[1:], 0)
        xi_ref[_FWD] = jnp.exp(lg_f * (qi + 1).astype(F32))
        xi_ref[_BWD] = jnp.exp(lg_b * (c - qi).astype(F32))
        kj = lax.broadcasted_iota(jnp.int32, zeta_ref.shape[1:], 0)
        zeta_ref[_FWD] = jnp.exp(lg_f * (c - 1 - kj).astype(F32))
        zeta_ref[_BWD] = jnp.exp(lg_b * kj.astype(F32))
        gc_ref[_FWD] = jnp.exp(jnp.zeros(gc_ref.shape[1:], F32) + lg_f * c)
        gc_ref[_BWD] = jnp.exp(jnp.zeros(gc_ref.shape[1:], F32) + lg_b * c)

    def absorb(ci, d):
        rows = pl.ds(ci * c, c)
        kz_t = (k_ref[rows, :].astype(F32) * zeta_ref[d]).T.astype(BF16)
        state_ref[...] = state_ref[...] * gc_ref[d] + _mm(kz_t, v_ref[rows, :])

    @pl.when(direction == _BWD)
    def _backward_states():
        first = (nblk - 1 - blk) * n_chunks
        for ci in reversed(range(n_chunks)):
            sb_ref[first + ci] = state_ref[...].astype(BF16)
            absorb(ci, _BWD)

    @pl.when(direction == _FWD)
    def _forward():
        first = blk * n_chunks
        for ci in range(n_chunks):
            rows = pl.ds(ci * c, c)
            q = q_ref[rows, :]
            s = lax.dot_general(q, k_ref[rows, :], _NT, preferred_element_type=F32) * dmat_ref[...]
            o_scr[rows, :] = (_mm(s.astype(BF16), v_ref[rows, :])
                              + _mm(q, state_ref[...].astype(BF16)) * xi_ref[_FWD]
                              + _mm(q, sb_ref[first + ci]) * xi_ref[_BWD])
            absorb(ci, _FWD)
            for r in range(ci * c, (ci + 1) * c, RET_NORM_ROWS):
                o = o_scr[pl.ds(r, RET_NORM_ROWS), :]
                dev = o - jnp.mean(o, axis=-1, keepdims=True)
                on = dev * lax.rsqrt(jnp.mean(dev * dev, axis=-1, keepdims=True) + EPS)
                gate = gate_ref[pl.ds(r, RET_NORM_ROWS), :].astype(F32)
                o_ref[pl.ds(r, RET_NORM_ROWS), :] = (gate * on).astype(BF16)


def _ret_scan(q, k, v, gate, lg, batch, length):
    hv = RET_HEADS * RET_DV
    assert length % RET_BLOCK == 0
    nblk = length // RET_BLOCK

    def sweep(b, h, d, i):
        return (b, h, d * i + (1 - d) * (nblk - 1 - i), 0)

    def fwd_only(b, h, d, i):
        return (b, h, d * i, 0)

    out = pl.pallas_call(
        _ret_scan_body,
        grid=(batch, RET_HEADS, 2, nblk),
        in_specs=[pl.BlockSpec(memory_space=pltpu.SMEM),
                  pl.BlockSpec((None, None, RET_BLOCK, RET_DK), fwd_only),
                  pl.BlockSpec((None, None, RET_BLOCK, RET_DK), sweep),
                  pl.BlockSpec((None, None, RET_BLOCK, RET_DV), sweep),
                  pl.BlockSpec((None, None, RET_BLOCK, RET_DV), fwd_only)],
        out_specs=pl.BlockSpec((None, RET_BLOCK, RET_DV), lambda b, h, d, i: (b, d * i, h)),
        out_shape=jax.ShapeDtypeStruct((batch, length, hv), BF16),
        scratch_shapes=[pltpu.VMEM((RET_DK, RET_DV), F32),
                        pltpu.VMEM((length // RET_CHUNK, RET_DK, RET_DV), BF16),
                        pltpu.VMEM((RET_BLOCK, RET_DV), F32),
                        pltpu.VMEM((RET_CHUNK, RET_CHUNK), F32),
                        pltpu.VMEM((2, RET_CHUNK, RET_DV), F32),
                        pltpu.VMEM((2, RET_CHUNK, RET_DK), F32),
                        pltpu.VMEM((2, 1, RET_DV), F32)],
        compiler_params=_params("parallel", "parallel", "arbitrary", "arbitrary"),
        name="ret_scan",
    )(lg, q, k, v, gate)
    return out.reshape(batch * length, hv)


def _swap_halves(a):
    half = a.shape[-1] // 2
    return jnp.concatenate([a[..., half:], a[..., :half]], axis=-1)


def _rope_slot_layout(a):
    pe = a[..., QK_NOPE_DIM:]
    return jnp.concatenate([a[..., :QK_NOPE_DIM], pe, _swap_halves(pe)], axis=-1)


def _mla_weights(w_dq, q_a_norm, w_uq, w_dkv, kv_a_norm, w_ukv, q_norm, k_norm):
    uq = w_uq.reshape(Q_LORA_RANK, MLA_HEADS, QK_HEAD_DIM)
    uq = _rope_slot_layout(uq).reshape(Q_LORA_RANK, MLA_HEADS * QK_SLOT)
    pe = w_dkv[:, KV_LORA_RANK:]
    dkv = jnp.concatenate([w_dkv[:, :KV_LORA_RANK], pe, _swap_halves(pe)], axis=-1)
    ukv = w_ukv.reshape(KV_LORA_RANK, MLA_HEADS, QK_NOPE_DIM + V_HEAD_DIM)
    wk = ukv[..., :QK_NOPE_DIM].reshape(KV_LORA_RANK, MLA_HEADS * QK_NOPE_DIM)
    wvt = ukv[..., QK_NOPE_DIM:].reshape(KV_LORA_RANK, MLA_HEADS * V_HEAD_DIM).T
    return {
        "dq": w_dq.astype(BF16), "qa": q_a_norm[None, :], "uqt": uq.T.astype(BF16),
        "dkv": dkv.astype(BF16), "kva": kv_a_norm[None, :],
        "k": wk.astype(BF16), "vt": wvt.astype(BF16),
        "qgt": jnp.broadcast_to(_rope_slot_layout(q_norm)[:, None], (QK_SLOT, ROW_TILE)),
        "kg": _rope_slot_layout(k_norm)[None, :],
    }


def _rope_tables(length, dim):
    inv = ROPE_BASE ** (-jnp.arange(0, dim, 2, dtype=F32) / dim)
    ang = jnp.arange(length, dtype=F32)[:, None] * inv[None, :]
    return jnp.cos(ang), jnp.sin(ang)


def _trunk(x, p):
    batch, length, d = x.shape
    assert d == D_MODEL and length % max(ROW_TILE, FFN_FIRST_ROW_TILE, FFN_SECOND_ROW_TILE) == 0
    x = x.reshape(batch * length, d)
    cos_m, sin_m = _rope_tables(length, QK_ROPE_DIM)
    cs_mla = jnp.concatenate([cos_m, cos_m, -sin_m, sin_m], axis=-1)
    cos_r, sin_r = _rope_tables(length, RET_DK)
    for i in range(DEPTH):
        j = i // 2
        ng = p["norm_g"][i]
        f = p["ffn"][i]
        h = _ffn_first(x, ng[0:1], f[0])
        if i % 2 == 0:
            w = p["mla"][j]
            qt, k, vt = _mla_proj(h, ng[1:2], w, cs_mla, batch, length)
            o = _attention(qt, k, vt, batch, length)
        else:
            w = p["ret"][j]
            q, k, v, gate = _ret_proj(h, ng[1:2], w["in"], w["gn"], cos_r, sin_r, batch, length)
            o = _ret_scan(q, k, v, gate, w["lg"], batch, length)
        x = _ffn_second(h, o, w["o"], ng[2:3], f[1], ng[3:4])
    return x.reshape(batch, length, d)


def kernel(x_prompt, x_sample, norm_g, ffn_w_gate, ffn_w_up, ffn_w_down, mla_w_dq, mla_q_a_norm, mla_w_uq, mla_w_dkv, mla_kv_a_norm, mla_w_ukv, mla_q_norm, mla_k_norm, mla_w_o, ret_w_in, ret_decay_fwd, ret_decay_bwd, ret_gn_g, ret_w_o):
    p = {
        "norm_g": norm_g,
        "ffn": [[{"g": ffn_w_gate[i, s].astype(BF16), "u": ffn_w_up[i, s].astype(BF16),
                  "d": ffn_w_down[i, s].astype(BF16)} for s in range(2)] for i in range(DEPTH)],
        "mla": [], "ret": [],
    }
    for j in range(mla_w_dq.shape[0]):
        w = _mla_weights(mla_w_dq[j], mla_q_a_norm[j], mla_w_uq[j], mla_w_dkv[j], mla_kv_a_norm[j],
                         mla_w_ukv[j], mla_q_norm[j], mla_k_norm[j])
        w["o"] = mla_w_o[j].astype(BF16)
        p["mla"].append(w)
    for j in range(ret_w_in.shape[0]):
        lg = jnp.stack([jnp.log1p(-jnp.exp(ret_decay_bwd[j].astype(F32))),
                        jnp.log1p(-jnp.exp(ret_decay_fwd[j].astype(F32)))])
        p["ret"].append({"in": ret_w_in[j].astype(BF16), "lg": lg, "gn": ret_gn_g[j][None, :],
                         "o": ret_w_o[j].astype(BF16)})
    return _trunk(x_prompt, p), _trunk(x_sample, p)
```

```python
import math

import jax
import jax.numpy as jnp
from jax import lax
from jax.experimental import pallas as pl
from jax.experimental.pallas import tpu as pltpu

F32 = jnp.float32
BF16 = jnp.bfloat16

D_MODEL = 1024
DEPTH = 4
D_FF = 2816
MLA_HEADS = 8
QK_NOPE_DIM = 128
QK_ROPE_DIM = 64
QK_HEAD_DIM = QK_NOPE_DIM + QK_ROPE_DIM
V_HEAD_DIM = 128
Q_LORA_RANK = 384
KV_LORA_RANK = 256
ROPE_BASE = 10000.0
RET_HEADS = 4
RET_DK = D_MODEL // RET_HEADS
RET_DV = 2 * D_MODEL // RET_HEADS
EPS = 1e-6

LANES_V7X = 128
MXU_DIM_V7X = 256
VMEM_LIMIT_BYTES_V7X = 56 * 1024 * 1024

QK_SLOT = QK_NOPE_DIM + 2 * QK_ROPE_DIM
assert QK_SLOT == MXU_DIM_V7X
BF16_SUBLANES_V7X = 16
V_SLOT = V_HEAD_DIM + BF16_SUBLANES_V7X

ROW_TILE = 1024
FFN_FIRST_ROW_TILE = 1024
FFN_SECOND_ROW_TILE = 1024
FFN_SECOND_VMEM_LIMIT_BYTES_V7X = 62 * 1024 * 1024
ATTN_TQ = 512
ATTN_TK = 512
ATTN_LOOKAHEAD = 2
ATTN_BUFFERS = 4
ATTN_Q_PER_TRIP = 2
RET_CHUNK = MXU_DIM_V7X
RET_BLOCK = 2048
RET_NORM_ROWS = 64

_NT = (((1,), (1,)), ((), ()))


def _params(*sem, vmem_limit_bytes=VMEM_LIMIT_BYTES_V7X):
    return pltpu.CompilerParams(dimension_semantics=sem, vmem_limit_bytes=vmem_limit_bytes)


def _resident(shape):
    return pl.BlockSpec(shape, lambda *_: (0,) * len(shape), pipeline_mode=pl.Buffered(1))


def _rms(x, g):
    return x * lax.rsqrt(jnp.mean(x * x, axis=-1, keepdims=True) + EPS) * g


def _mm(a, b):
    return jnp.dot(a, b, preferred_element_type=F32)


def _swiglu_half(x, g_ref, wg_ref, wu_ref, wd_ref):
    xn = _rms(x, g_ref[...]).astype(BF16)
    gate = _mm(xn, wg_ref[...])
    up = _mm(xn, wu_ref[...])
    h = (gate * jax.nn.sigmoid(gate) * up).astype(BF16)
    return x + 0.5 * _mm(h, wd_ref[...])


def _ffn_first_body(x_ref, g_ref, wg_ref, wu_ref, wd_ref, o_ref):
    o_ref[...] = _swiglu_half(x_ref[...], g_ref, wg_ref, wu_ref, wd_ref)


def _ffn_second_body(x_ref, a_ref, wo_ref, g_ref, wg_ref, wu_ref, wd_ref, gp_ref, o_ref):
    h = x_ref[...] + _mm(a_ref[...], wo_ref[...])
    o_ref[...] = _rms(_swiglu_half(h, g_ref, wg_ref, wu_ref, wd_ref), gp_ref[...])


def _ffn_first(x, g, f):
    n, d = x.shape
    row = pl.BlockSpec((FFN_FIRST_ROW_TILE, d), lambda i: (i, 0))
    return pl.pallas_call(
        _ffn_first_body,
        grid=(n // FFN_FIRST_ROW_TILE,),
        in_specs=[row, _resident((1, d)), _resident(f["g"].shape), _resident(f["u"].shape),
                  _resident(f["d"].shape)],
        out_specs=row,
        out_shape=jax.ShapeDtypeStruct((n, d), F32),
        compiler_params=_params("parallel"),
        name="ffn_first",
    )(x, g, f["g"], f["u"], f["d"])


def _ffn_second(x, a, w_o, g, f, g_post):
    n, d = x.shape
    row = pl.BlockSpec((FFN_SECOND_ROW_TILE, d), lambda i: (i, 0))
    return pl.pallas_call(
        _ffn_second_body,
        grid=(n // FFN_SECOND_ROW_TILE,),
        in_specs=[row, pl.BlockSpec((FFN_SECOND_ROW_TILE, a.shape[1]), lambda i: (i, 0)), _resident(w_o.shape),
                  _resident((1, d)), _resident(f["g"].shape), _resident(f["u"].shape),
                  _resident(f["d"].shape), _resident((1, d))],
        out_specs=row,
        out_shape=jax.ShapeDtypeStruct((n, d), F32),
        compiler_params=_params("parallel", vmem_limit_bytes=FFN_SECOND_VMEM_LIMIT_BYTES_V7X),
        name="ffn_second",
    )(x, a, w_o, g, f["g"], f["u"], f["d"], g_post)


def _rope_slot(t, gain, cs):
    a = t * gain * cs
    lane = lax.broadcasted_iota(jnp.int32, a.shape, 1)
    return jnp.where(lane < QK_ROPE_DIM, a + pltpu.roll(a, QK_ROPE_DIM, axis=1), 0.0)


def _sumsq_rope(t):
    lane = lax.broadcasted_iota(jnp.int32, t.shape, 1)
    return jnp.sum(jnp.where(lane < QK_ROPE_DIM, t * t, 0.0), axis=-1, keepdims=True)


def _mla_proj_body(h_ref, g_ref, wdq_ref, qa_ref, wuqt_ref, wdkv_ref, kva_ref, wk_ref, wvt_ref,
                   qgt_ref, kg_ref, cs_ref, cst_ref, qt_ref, k_ref, vt_ref):
    hn = _rms(h_ref[...], g_ref[...]).astype(BF16)
    cs = cs_ref[...]
    kg = kg_ref[...]
    inv_dim = 1.0 / QK_HEAD_DIM

    ckv_full = _mm(hn, wdkv_ref[...])
    ckv = _rms(ckv_full[:, :KV_LORA_RANK], kva_ref[...]).astype(BF16)
    vt = lax.dot_general(wvt_ref[...], ckv, _NT, preferred_element_type=F32).astype(BF16)
    for hd in range(MLA_HEADS):
        vt_ref[hd * V_SLOT:hd * V_SLOT + V_HEAD_DIM, :] = vt[hd * V_HEAD_DIM:(hd + 1) * V_HEAD_DIM, :]
        vt_ref[hd * V_SLOT + V_HEAD_DIM:(hd + 1) * V_SLOT, :] = jnp.ones((BF16_SUBLANES_V7X, vt.shape[1]), BF16)
    k_pe_raw = ckv_full[:, KV_LORA_RANK:]
    k_pe_ss = _sumsq_rope(k_pe_raw)
    k_pe = _rope_slot(k_pe_raw, kg[:, QK_NOPE_DIM:], cs)
    cq = _rms(_mm(hn, wdq_ref[...]), qa_ref[...]).astype(BF16)
    scale = QK_HEAD_DIM ** -0.5 * math.log2(math.e)
    cst = cst_ref[...]
    qgt = qgt_ref[...]
    for hd in range(MLA_HEADS):
        if hd % 2 == 0:
            k_pair = _mm(ckv, wk_ref[:, hd * QK_NOPE_DIM:(hd + 2) * QK_NOPE_DIM])
        kn = k_pair[:, (hd % 2) * QK_NOPE_DIM:(hd % 2 + 1) * QK_NOPE_DIM]
        r = lax.rsqrt((jnp.sum(kn * kn, axis=-1, keepdims=True) + k_pe_ss) * inv_dim + EPS)
        k_ref[hd, :, :QK_NOPE_DIM] = (kn * r * kg[:, :QK_NOPE_DIM]).astype(BF16)
        k_ref[hd, :, QK_NOPE_DIM:] = (k_pe * r).astype(BF16)
        base = hd * QK_SLOT
        qt = lax.dot_general(wuqt_ref[base:base + QK_SLOT, :], cq, _NT, preferred_element_type=F32)
        qn, qp = qt[:QK_NOPE_DIM], qt[QK_NOPE_DIM:]
        pe = qp[:QK_ROPE_DIM]
        ss = jnp.sum(qn * qn, axis=0, keepdims=True) + jnp.sum(pe * pe, axis=0, keepdims=True)
        r = lax.rsqrt(ss * inv_dim + EPS) * scale
        a = qp * qgt[QK_NOPE_DIM:] * cst
        qt_ref[base:base + QK_NOPE_DIM, :] = (qn * r * qgt[:QK_NOPE_DIM]).astype(BF16)
        qt_ref[base + QK_NOPE_DIM:base + QK_HEAD_DIM, :] = (
            (a[:QK_ROPE_DIM] + a[QK_ROPE_DIM:]) * r).astype(BF16)
        qt_ref[base + QK_HEAD_DIM:base + QK_SLOT, :] = jnp.zeros((QK_SLOT - QK_HEAD_DIM, qt.shape[1]), BF16)


def _mla_proj(h, g, w, cs, batch, length):
    n, d = h.shape
    nblk = length // ROW_TILE
    slots = MLA_HEADS * QK_SLOT
    dv = MLA_HEADS * V_SLOT
    row = lambda width: pl.BlockSpec((ROW_TILE, width), lambda i: (i, 0))
    return pl.pallas_call(
        _mla_proj_body,
        grid=(n // ROW_TILE,),
        in_specs=[row(d), _resident((1, d)), _resident(w["dq"].shape), _resident((1, Q_LORA_RANK)),
                  _resident(w["uqt"].shape), _resident(w["dkv"].shape), _resident((1, KV_LORA_RANK)),
                  _resident(w["k"].shape), _resident(w["vt"].shape),
                  _resident((QK_SLOT, ROW_TILE)), _resident((1, QK_SLOT)),
                  pl.BlockSpec((ROW_TILE, LANES_V7X), lambda i: (i % nblk, 0)),
                  pl.BlockSpec((LANES_V7X, ROW_TILE), lambda i: (0, i % nblk))],
        out_specs=[pl.BlockSpec((None, slots, ROW_TILE), lambda i: (i // nblk, 0, i % nblk)),
                   pl.BlockSpec((None, MLA_HEADS, ROW_TILE, QK_SLOT), lambda i: (i // nblk, 0, i % nblk, 0)),
                   pl.BlockSpec((None, dv, ROW_TILE), lambda i: (i // nblk, 0, i % nblk))],
        out_shape=[jax.ShapeDtypeStruct((batch, slots, length), BF16),
                   jax.ShapeDtypeStruct((batch, MLA_HEADS, length, QK_SLOT), BF16),
                   jax.ShapeDtypeStruct((batch, dv, length), BF16)],
        compiler_params=_params("parallel"),
        name="mla_proj",
    )(h, g, w["dq"], w["qa"], w["uqt"], w["dkv"], w["kva"], w["k"], w["vt"], w["qgt"], w["kg"], cs, cs.T)


def _attn_body(qt_ref, k_ref, vt_ref, o_ref, s_ref, smax_ref, m_ref, acc_ref):
    length = k_ref.shape[0]
    n_chunks = length // ATTN_TK
    n_q = length // ATTN_TQ

    def scores(q_tile, j, slot):
        q_off = pl.multiple_of(q_tile * ATTN_TQ, ATTN_TQ)
        st = _mm(k_ref[j * ATTN_TK:(j + 1) * ATTN_TK, :], qt_ref[:, pl.ds(q_off, ATTN_TQ)])
        s_ref[slot] = st
        smax_ref[slot] = jnp.max(st, axis=0, keepdims=True)

    def update(j, slot):
        m_old = m_ref[...]
        m_new = jnp.maximum(m_old, smax_ref[slot])
        p = jnp.exp2(s_ref[slot] - m_new).astype(BF16)
        alpha = jnp.exp2(m_old - m_new)
        acc_ref[...] = alpha * acc_ref[...] + _mm(vt_ref[:, j * ATTN_TK:(j + 1) * ATTN_TK], p)
        m_ref[...] = m_new

    for j in range(ATTN_LOOKAHEAD):
        scores(0, j, j)

    def q_tile(tile):
        m_ref[...] = jnp.full(m_ref.shape, -jnp.inf, F32)
        acc_ref[...] = jnp.zeros(acc_ref.shape, F32)
        next_tile = lax.rem(tile + 1, n_q)
        for j in range(n_chunks):
            ahead = j + ATTN_LOOKAHEAD
            scores(tile if ahead < n_chunks else next_tile, ahead % n_chunks, ahead % ATTN_BUFFERS)
            update(j, j % ATTN_BUFFERS)
        o = acc_ref[:V_HEAD_DIM, :] / acc_ref[V_HEAD_DIM:V_HEAD_DIM + 1, :]
        o_ref[pl.ds(pl.multiple_of(tile * ATTN_TQ, ATTN_TQ), ATTN_TQ), :] = o.T.astype(BF16)

    def trip(t, carry):
        for i in range(ATTN_Q_PER_TRIP):
            q_tile(t * ATTN_Q_PER_TRIP + i)
        return carry

    lax.fori_loop(0, n_q // ATTN_Q_PER_TRIP, trip, 0)


def _attention(qt, k, vt, batch, length):
    assert length % (ATTN_TQ * ATTN_Q_PER_TRIP) == 0 and length % (ATTN_TK * ATTN_BUFFERS) == 0
    out = pl.pallas_call(
        _attn_body,
        grid=(batch, MLA_HEADS),
        in_specs=[pl.BlockSpec((None, QK_SLOT, length), lambda b, h: (b, h, 0)),
                  pl.BlockSpec((None, None, length, QK_SLOT), lambda b, h: (b, h, 0, 0)),
                  pl.BlockSpec((None, V_SLOT, length), lambda b, h: (b, h, 0))],
        out_specs=pl.BlockSpec((None, length, V_HEAD_DIM), lambda b, h: (b, 0, h)),
        out_shape=jax.ShapeDtypeStruct((batch, length, MLA_HEADS * V_HEAD_DIM), BF16),
        scratch_shapes=[pltpu.VMEM((ATTN_BUFFERS, ATTN_TK, ATTN_TQ), F32),
                        pltpu.VMEM((ATTN_BUFFERS, 1, ATTN_TQ), F32), pltpu.VMEM((1, ATTN_TQ), F32),
                        pltpu.VMEM((V_SLOT, ATTN_TQ), F32)],
        compiler_params=_params("parallel", "parallel"),
        name="mla_attn",
    )(qt, k, vt)
    return out.reshape(batch * length, MLA_HEADS * V_HEAD_DIM)


def _ret_proj_body(h_ref, g_ref, w_ref, gn_ref, cos_ref, sin_ref, q_ref, k_ref, v_ref, gate_ref):
    hn = _rms(h_ref[...], g_ref[...]).astype(BF16)
    cos = cos_ref[...]
    sin = sin_ref[...]
    hq = RET_HEADS * RET_DK
    hv = RET_HEADS * RET_DV
    half = RET_DK // 2

    def rope_store(dst, col0, mult):
        for hd in range(RET_HEADS):
            x = _mm(hn, w_ref[:, col0 + hd * RET_DK:col0 + (hd + 1) * RET_DK])
            x1, x2 = x[:, :half], x[:, half:]
            dst[hd, :, :half] = ((x1 * cos - x2 * sin) * mult).astype(BF16)
            dst[hd, :, half:] = ((x2 * cos + x1 * sin) * mult).astype(BF16)

    rope_store(q_ref, 0, 1.0)
    rope_store(k_ref, hq, RET_DK ** -0.5)
    v = _mm(hn, w_ref[:, 2 * hq:2 * hq + hv]).astype(BF16)
    gate = _mm(hn, w_ref[:, 2 * hq + hv:])
    gate = (gate * jax.nn.sigmoid(gate) * gn_ref[...]).astype(BF16)
    for hd in range(RET_HEADS):
        v_ref[hd] = v[:, hd * RET_DV:(hd + 1) * RET_DV]
        gate_ref[hd] = gate[:, hd * RET_DV:(hd + 1) * RET_DV]


def _ret_proj(h, g, w_in, gn_g, cos, sin, batch, length):
    n, d = h.shape
    nblk = length // ROW_TILE
    hv = RET_HEADS * RET_DV
    row = lambda width: pl.BlockSpec((ROW_TILE, width), lambda i: (i, 0))
    pos = pl.BlockSpec((ROW_TILE, RET_DK // 2), lambda i: (i % nblk, 0))
    heads = lambda dim: pl.BlockSpec((None, RET_HEADS, ROW_TILE, dim), lambda i: (i // nblk, 0, i % nblk, 0))
    shape = lambda dim: jax.ShapeDtypeStruct((batch, RET_HEADS, length, dim), BF16)
    return pl.pallas_call(
        _ret_proj_body,
        grid=(n // ROW_TILE,),
        in_specs=[row(d), _resident((1, d)), _resident(w_in.shape), _resident((1, hv)), pos, pos],
        out_specs=[heads(RET_DK), heads(RET_DK), heads(RET_DV), heads(RET_DV)],
        out_shape=[shape(RET_DK), shape(RET_DK), shape(RET_DV), shape(RET_DV)],
        compiler_params=_params("parallel"),
        name="ret_proj",
    )(h, g, w_in, gn_g, cos, sin)


_BWD, _FWD = 0, 1


def _ret_scan_body(lg_ref, q_ref, k_ref, v_ref, gate_ref, o_ref,
                   state_ref, sb_ref, o_scr, dmat_ref, xi_ref, zeta_ref, gc_ref):
    hd = pl.program_id(1)
    direction = pl.program_id(2)
    blk = pl.program_id(3)
    nblk = pl.num_programs(3)
    c = RET_CHUNK
    n_chunks = RET_BLOCK // c

    @pl.when(blk == 0)
    def _reset_state():
        state_ref[...] = jnp.zeros(state_ref.shape, F32)

    @pl.when((blk == 0) & (direction == _BWD))
    def _decay_tables():
        lg_b = lg_ref[_BWD, hd]
        lg_f = lg_ref[_FWD, hd]
        row = lax.broadcasted_iota(jnp.int32, (c, c), 0)
        col = lax.broadcasted_iota(jnp.int32, (c, c), 1)
        dmat_ref[...] = jnp.where(row >= col,
                                  jnp.exp(lg_f * jnp.maximum(row - col, 0).astype(F32)),
                                  jnp.exp(lg_b * jnp.maximum(col - row, 0).astype(F32)))
        qi = lax.broadcasted_iota(jnp.int32, xi_ref.shape[1:], 0)
        xi_ref[_FWD] = jnp.exp(lg_f * (qi + 1).astype(F32))
        xi_ref[_BWD] = jnp.exp(lg_b * (c - qi).astype(F32))
        kj = lax.broadcasted_iota(jnp.int32, zeta_ref.shape[1:], 0)
        zeta_ref[_FWD] = jnp.exp(lg_f * (c - 1 - kj).astype(F32))
        zeta_ref[_BWD] = jnp.exp(lg_b * kj.astype(F32))
        gc_ref[_FWD] = jnp.exp(jnp.zeros(gc_ref.shape[1:], F32) + lg_f * c)
        gc_ref[_BWD] = jnp.exp(jnp.zeros(gc_ref.shape[1:], F32) + lg_b * c)

    def absorb(ci, d):
        rows = pl.ds(ci * c, c)
        kz_t = (k_ref[rows, :].astype(F32) * zeta_ref[d]).T.astype(BF16)
        state_ref[...] = state_ref[...] * gc_ref[d] + _mm(kz_t, v_ref[rows, :])

    @pl.when(direction == _BWD)
    def _backward_states():
        first = (nblk - 1 - blk) * n_chunks
        for ci in reversed(range(n_chunks)):
            sb_ref[first + ci] = state_ref[...].astype(BF16)
            absorb(ci, _BWD)

    @pl.when(direction == _FWD)
    def _forward():
        first = blk * n_chunks
        for ci in range(n_chunks):
            rows = pl.ds(ci * c, c)
            q = q_ref[rows, :]
            s = lax.dot_general(q, k_ref[rows, :], _NT, preferred_element_type=F32) * dmat_ref[...]
            o_scr[rows, :] = (_mm(s.astype(BF16), v_ref[rows, :])
                              + _mm(q, state_ref[...].astype(BF16)) * xi_ref[_FWD]
                              + _mm(q, sb_ref[first + ci]) * xi_ref[_BWD])
            absorb(ci, _FWD)
            for r in range(ci * c, (ci + 1) * c, RET_NORM_ROWS):
                o = o_scr[pl.ds(r, RET_NORM_ROWS), :]
                dev = o - jnp.mean(o, axis=-1, keepdims=True)
                on = dev * lax.rsqrt(jnp.mean(dev * dev, axis=-1, keepdims=True) + EPS)
                gate = gate_ref[pl.ds(r, RET_NORM_ROWS), :].astype(F32)
                o_ref[pl.ds(r, RET_NORM_ROWS), :] = (gate * on).astype(BF16)


def _ret_scan(q, k, v, gate, lg, batch, length):
    hv = RET_HEADS * RET_DV
    assert length % RET_BLOCK == 0
    nblk = length // RET_BLOCK

    def sweep(b, h, d, i):
        return (b, h, d * i + (1 - d) * (nblk - 1 - i), 0)

    def fwd_only(b, h, d, i):
        return (b, h, d * i, 0)

    out = pl.pallas_call(
        _ret_scan_body,
        grid=(batch, RET_HEADS, 2, nblk),
        in_specs=[pl.BlockSpec(memory_space=pltpu.SMEM),
                  pl.BlockSpec((None, None, RET_BLOCK, RET_DK), fwd_only),
                  pl.BlockSpec((None, None, RET_BLOCK, RET_DK), sweep),
                  pl.BlockSpec((None, None, RET_BLOCK, RET_DV), sweep),
                  pl.BlockSpec((None, None, RET_BLOCK, RET_DV), fwd_only)],
        out_specs=pl.BlockSpec((None, RET_BLOCK, RET_DV), lambda b, h, d, i: (b, d * i, h)),
        out_shape=jax.ShapeDtypeStruct((batch, length, hv), BF16),
        scratch_shapes=[pltpu.VMEM((RET_DK, RET_DV), F32),
                        pltpu.VMEM((length // RET_CHUNK, RET_DK, RET_DV), BF16),
                        pltpu.VMEM((RET_BLOCK, RET_DV), F32),
                        pltpu.VMEM((RET_CHUNK, RET_CHUNK), F32),
                        pltpu.VMEM((2, RET_CHUNK, RET_DV), F32),
                        pltpu.VMEM((2, RET_CHUNK, RET_DK), F32),
                        pltpu.VMEM((2, 1, RET_DV), F32)],
        compiler_params=_params("parallel", "parallel", "arbitrary", "arbitrary"),
        name="ret_scan",
    )(lg, q, k, v, gate)
    return out.reshape(batch * length, hv)


def _swap_halves(a):
    half = a.shape[-1] // 2
    return jnp.concatenate([a[..., half:], a[..., :half]], axis=-1)


def _rope_slot_layout(a):
    pe = a[..., QK_NOPE_DIM:]
    return jnp.concatenate([a[..., :QK_NOPE_DIM], pe, _swap_halves(pe)], axis=-1)


def _mla_weights(w_dq, q_a_norm, w_uq, w_dkv, kv_a_norm, w_ukv, q_norm, k_norm):
    uq = w_uq.reshape(Q_LORA_RANK, MLA_HEADS, QK_HEAD_DIM)
    uq = _rope_slot_layout(uq).reshape(Q_LORA_RANK, MLA_HEADS * QK_SLOT)
    pe = w_dkv[:, KV_LORA_RANK:]
    dkv = jnp.concatenate([w_dkv[:, :KV_LORA_RANK], pe, _swap_halves(pe)], axis=-1)
    ukv = w_ukv.reshape(KV_LORA_RANK, MLA_HEADS, QK_NOPE_DIM + V_HEAD_DIM)
    wk = ukv[..., :QK_NOPE_DIM].reshape(KV_LORA_RANK, MLA_HEADS * QK_NOPE_DIM)
    wvt = ukv[..., QK_NOPE_DIM:].reshape(KV_LORA_RANK, MLA_HEADS * V_HEAD_DIM).T
    return {
        "dq": w_dq.astype(BF16), "qa": q_a_norm[None, :], "uqt": uq.T.astype(BF16),
        "dkv": dkv.astype(BF16), "kva": kv_a_norm[None, :],
        "k": wk.astype(BF16), "vt": wvt.astype(BF16),
        "qgt": jnp.broadcast_to(_rope_slot_layout(q_norm)[:, None], (QK_SLOT, ROW_TILE)),
        "kg": _rope_slot_layout(k_norm)[None, :],
    }


def _rope_tables(length, dim):
    inv = ROPE_BASE ** (-jnp.arange(0, dim, 2, dtype=F32) / dim)
    ang = jnp.arange(length, dtype=F32)[:, None] * inv[None, :]
    return jnp.cos(ang), jnp.sin(ang)


def _trunk(x, p):
    batch, length, d = x.shape
    assert d == D_MODEL and length % max(ROW_TILE, FFN_FIRST_ROW_TILE, FFN_SECOND_ROW_TILE) == 0
    x = x.reshape(batch * length, d)
    cos_m, sin_m = _rope_tables(length, QK_ROPE_DIM)
    cs_mla = jnp.concatenate([cos_m, cos_m, -sin_m, sin_m], axis=-1)
    cos_r, sin_r = _rope_tables(length, RET_DK)
    for i in range(DEPTH):
        j = i // 2
        ng = p["norm_g"][i]
        f = p["ffn"][i]
        h = _ffn_first(x, ng[0:1], f[0])
        if i % 2 == 0:
            w = p["mla"][j]
            qt, k, vt = _mla_proj(h, ng[1:2], w, cs_mla, batch, length)
            o = _attention(qt, k, vt, batch, length)
        else:
            w = p["ret"][j]
            q, k, v, gate = _ret_proj(h, ng[1:2], w["in"], w["gn"], cos_r, sin_r, batch, length)
            o = _ret_scan(q, k, v, gate, w["lg"], batch, length)
        x = _ffn_second(h, o, w["o"], ng[2:3], f[1], ng[3:4])
    return x.reshape(batch, length, d)


def kernel(x_prompt, x_sample, norm_g, ffn_w_gate, ffn_w_up, ffn_w_down, mla_w_dq, mla_q_a_norm, mla_w_uq, mla_w_dkv, mla_kv_a_norm, mla_w_ukv, mla_q_norm, mla_k_norm, mla_w_o, ret_w_in, ret_decay_fwd, ret_decay_bwd, ret_gn_g, ret_w_o):
    p = {
        "norm_g": norm_g,
        "ffn": [[{"g": ffn_w_gate[i, s].astype(BF16), "u": ffn_w_up[i, s].astype(BF16),
                  "d": ffn_w_down[i, s].astype(BF16)} for s in range(2)] for i in range(DEPTH)],
        "mla": [], "ret": [],
    }
    for j in range(mla_w_dq.shape[0]):
        w = _mla_weights(mla_w_dq[j], mla_q_a_norm[j], mla_w_uq[j], mla_w_dkv[j], mla_kv_a_norm[j],
                         mla_w_ukv[j], mla_q_norm[j], mla_k_norm[j])
        w["o"] = mla_w_o[j].astype(BF16)
        p["mla"].append(w)
    for j in range(ret_w_in.shape[0]):
        lg = jnp.stack([jnp.log1p(-jnp.exp(ret_decay_bwd[j].astype(F32))),
                        jnp.log1p(-jnp.exp(ret_decay_fwd[j].astype(F32)))])
        p["ret"].append({"in": ret_w_in[j].astype(BF16), "lg": lg, "gn": ret_gn_g[j][None, :],
                         "o": ret_w_o[j].astype(BF16)})
    return _trunk(x_prompt, p), _trunk(x_sample, p)
```

```python
import math

import jax
import jax.numpy as jnp
from jax import lax
from jax.experimental import pallas as pl
from jax.experimental.pallas import tpu as pltpu

F32 = jnp.float32
BF16 = jnp.bfloat16

D_MODEL = 1024
DEPTH = 4
D_FF = 2816
MLA_HEADS = 8
QK_NOPE_DIM = 128
QK_ROPE_DIM = 64
QK_HEAD_DIM = QK_NOPE_DIM + QK_ROPE_DIM
V_HEAD_DIM = 128
Q_LORA_RANK = 384
KV_LORA_RANK = 256
ROPE_BASE = 10000.0
RET_HEADS = 4
RET_DK = D_MODEL // RET_HEADS
RET_DV = 2 * D_MODEL // RET_HEADS
EPS = 1e-6

LANES_V7X = 128
MXU_DIM_V7X = 256
VMEM_LIMIT_BYTES_V7X = 56 * 1024 * 1024

QK_SLOT = QK_NOPE_DIM + 2 * QK_ROPE_DIM
assert QK_SLOT == MXU_DIM_V7X
BF16_SUBLANES_V7X = 16
V_SLOT = V_HEAD_DIM + BF16_SUBLANES_V7X

ROW_TILE = 1024
FFN_FIRST_ROW_TILE = 1024
FFN_SECOND_ROW_TILE = 1024
FFN_SECOND_VMEM_LIMIT_BYTES_V7X = 62 * 1024 * 1024
ATTN_TQ = 512
ATTN_TK = 512
ATTN_LOOKAHEAD = 2
ATTN_BUFFERS = 4
ATTN_Q_PER_TRIP = 4
RET_CHUNK = MXU_DIM_V7X
RET_BLOCK = 2048
RET_NORM_ROWS = 64

_NT = (((1,), (1,)), ((), ()))


def _params(*sem, vmem_limit_bytes=VMEM_LIMIT_BYTES_V7X):
    return pltpu.CompilerParams(dimension_semantics=sem, vmem_limit_bytes=vmem_limit_bytes)


def _resident(shape):
    return pl.BlockSpec(shape, lambda *_: (0,) * len(shape), pipeline_mode=pl.Buffered(1))


def _resident_slice(stacked):
    array, index = stacked
    block = (None,) * len(index) + array.shape[len(index):]
    at = tuple(index) + (0,) * (array.ndim - len(index))
    return pl.BlockSpec(block, lambda *_: at, pipeline_mode=pl.Buffered(1))


def _rms(x, g):
    return x * lax.rsqrt(jnp.mean(x * x, axis=-1, keepdims=True) + EPS) * g


def _mm(a, b):
    return jnp.dot(a, b, preferred_element_type=F32)


def _swiglu_half(x, g_ref, wg_ref, wu_ref, wd_ref):
    xn = _rms(x, g_ref[...]).astype(BF16)
    gate = _mm(xn, wg_ref[...])
    up = _mm(xn, wu_ref[...])
    h = (gate * jax.nn.sigmoid(gate) * up).astype(BF16)
    return x + 0.5 * _mm(h, wd_ref[...])


def _ffn_first_body(x_ref, g_ref, wg_ref, wu_ref, wd_ref, o_ref):
    o_ref[...] = _swiglu_half(x_ref[...], g_ref, wg_ref, wu_ref, wd_ref)


def _ffn_second_body(x_ref, a_ref, wo_ref, g_ref, wg_ref, wu_ref, wd_ref, gp_ref, o_ref):
    h = x_ref[...] + _mm(a_ref[...], wo_ref[...])
    o_ref[...] = _rms(_swiglu_half(h, g_ref, wg_ref, wu_ref, wd_ref), gp_ref[...])


def _ffn_first(x, g, f):
    n, d = x.shape
    row = pl.BlockSpec((FFN_FIRST_ROW_TILE, d), lambda i: (i, 0))
    return pl.pallas_call(
        _ffn_first_body,
        grid=(n // FFN_FIRST_ROW_TILE,),
        in_specs=[row, _resident((1, d)), _resident_slice(f["g"]), _resident_slice(f["u"]),
                  _resident_slice(f["d"])],
        out_specs=row,
        out_shape=jax.ShapeDtypeStruct((n, d), F32),
        compiler_params=_params("parallel"),
        name="ffn_first",
    )(x, g, f["g"][0], f["u"][0], f["d"][0])


def _ffn_second(x, a, w_o, g, f, g_post):
    n, d = x.shape
    row = pl.BlockSpec((FFN_SECOND_ROW_TILE, d), lambda i: (i, 0))
    return pl.pallas_call(
        _ffn_second_body,
        grid=(n // FFN_SECOND_ROW_TILE,),
        in_specs=[row, pl.BlockSpec((FFN_SECOND_ROW_TILE, a.shape[1]), lambda i: (i, 0)), _resident_slice(w_o),
                  _resident((1, d)), _resident_slice(f["g"]), _resident_slice(f["u"]),
                  _resident_slice(f["d"]), _resident((1, d))],
        out_specs=row,
        out_shape=jax.ShapeDtypeStruct((n, d), F32),
        compiler_params=_params("parallel", vmem_limit_bytes=FFN_SECOND_VMEM_LIMIT_BYTES_V7X),
        name="ffn_second",
    )(x, a, w_o[0], g, f["g"][0], f["u"][0], f["d"][0], g_post)


def _rope_slot(t, gain, cs):
    a = t * gain * cs
    lane = lax.broadcasted_iota(jnp.int32, a.shape, 1)
    return jnp.where(lane < QK_ROPE_DIM, a + pltpu.roll(a, QK_ROPE_DIM, axis=1), 0.0)


def _sumsq_rope(t):
    lane = lax.broadcasted_iota(jnp.int32, t.shape, 1)
    return jnp.sum(jnp.where(lane < QK_ROPE_DIM, t * t, 0.0), axis=-1, keepdims=True)


def _mla_proj_body(h_ref, g_ref, wdq_ref, qa_ref, wuqt_ref, wdkv_ref, kva_ref, wk_ref, wvt_ref,
                   qgt_ref, kg_ref, cs_ref, cst_ref, qt_ref, k_ref, vt_ref):
    hn = _rms(h_ref[...], g_ref[...]).astype(BF16)
    cs = cs_ref[...]
    kg = kg_ref[...]
    inv_dim = 1.0 / QK_HEAD_DIM

    ckv_full = _mm(hn, wdkv_ref[...])
    ckv = _rms(ckv_full[:, :KV_LORA_RANK], kva_ref[...]).astype(BF16)
    vt = lax.dot_general(wvt_ref[...], ckv, _NT, preferred_element_type=F32).astype(BF16)
    for hd in range(MLA_HEADS):
        vt_ref[hd * V_SLOT:hd * V_SLOT + V_HEAD_DIM, :] = vt[hd * V_HEAD_DIM:(hd + 1) * V_HEAD_DIM, :]
        vt_ref[hd * V_SLOT + V_HEAD_DIM:(hd + 1) * V_SLOT, :] = jnp.ones((BF16_SUBLANES_V7X, vt.shape[1]), BF16)
    k_pe_raw = ckv_full[:, KV_LORA_RANK:]
    k_pe_ss = _sumsq_rope(k_pe_raw)
    k_pe = _rope_slot(k_pe_raw, kg[:, QK_NOPE_DIM:], cs)
    cq = _rms(_mm(hn, wdq_ref[...]), qa_ref[...]).astype(BF16)
    scale = QK_HEAD_DIM ** -0.5 * math.log2(math.e)
    cst = cst_ref[...]
    qgt = qgt_ref[...]
    for hd in range(MLA_HEADS):
        if hd % 2 == 0:
            k_pair = _mm(ckv, wk_ref[:, hd * QK_NOPE_DIM:(hd + 2) * QK_NOPE_DIM])
        kn = k_pair[:, (hd % 2) * QK_NOPE_DIM:(hd % 2 + 1) * QK_NOPE_DIM]
        r = lax.rsqrt((jnp.sum(kn * kn, axis=-1, keepdims=True) + k_pe_ss) * inv_dim + EPS)
        k_ref[hd, :, :QK_NOPE_DIM] = (kn * r * kg[:, :QK_NOPE_DIM]).astype(BF16)
        k_ref[hd, :, QK_NOPE_DIM:] = (k_pe * r).astype(BF16)
        base = hd * QK_SLOT
        qt = lax.dot_general(wuqt_ref[base:base + QK_SLOT, :], cq, _NT, preferred_element_type=F32)
        qn, qp = qt[:QK_NOPE_DIM], qt[QK_NOPE_DIM:]
        pe = qp[:QK_ROPE_DIM]
        ss = jnp.sum(qn * qn, axis=0, keepdims=True) + jnp.sum(pe * pe, axis=0, keepdims=True)
        r = lax.rsqrt(ss * inv_dim + EPS) * scale
        a = qp * qgt[QK_NOPE_DIM:] * cst
        qt_ref[base:base + QK_NOPE_DIM, :] = (qn * r * qgt[:QK_NOPE_DIM]).astype(BF16)
        qt_ref[base + QK_NOPE_DIM:base + QK_HEAD_DIM, :] = (
            (a[:QK_ROPE_DIM] + a[QK_ROPE_DIM:]) * r).astype(BF16)
        qt_ref[base + QK_HEAD_DIM:base + QK_SLOT, :] = jnp.zeros((QK_SLOT - QK_HEAD_DIM, qt.shape[1]), BF16)


def _mla_proj(h, g, w, cs, batch, length):
    n, d = h.shape
    nblk = length // ROW_TILE
    slots = MLA_HEADS * QK_SLOT
    dv = MLA_HEADS * V_SLOT
    row = lambda width: pl.BlockSpec((ROW_TILE, width), lambda i: (i, 0))
    return pl.pallas_call(
        _mla_proj_body,
        grid=(n // ROW_TILE,),
        in_specs=[row(d), _resident((1, d)), _resident(w["dq"].shape), _resident((1, Q_LORA_RANK)),
                  _resident(w["uqt"].shape), _resident(w["dkv"].shape), _resident((1, KV_LORA_RANK)),
                  _resident(w["k"].shape), _resident(w["vt"].shape),
                  _resident((QK_SLOT, ROW_TILE)), _resident((1, QK_SLOT)),
                  pl.BlockSpec((ROW_TILE, LANES_V7X), lambda i: (i % nblk, 0)),
                  pl.BlockSpec((LANES_V7X, ROW_TILE), lambda i: (0, i % nblk))],
        out_specs=[pl.BlockSpec((None, slots, ROW_TILE), lambda i: (i // nblk, 0, i % nblk)),
                   pl.BlockSpec((None, MLA_HEADS, ROW_TILE, QK_SLOT), lambda i: (i // nblk, 0, i % nblk, 0)),
                   pl.BlockSpec((None, dv, ROW_TILE), lambda i: (i // nblk, 0, i % nblk))],
        out_shape=[jax.ShapeDtypeStruct((batch, slots, length), BF16),
                   jax.ShapeDtypeStruct((batch, MLA_HEADS, length, QK_SLOT), BF16),
                   jax.ShapeDtypeStruct((batch, dv, length), BF16)],
        compiler_params=_params("parallel"),
        name="mla_proj",
    )(h, g, w["dq"], w["qa"], w["uqt"], w["dkv"], w["kva"], w["k"], w["vt"], w["qgt"], w["kg"], cs, cs.T)


def _attn_body(qt_ref, k_ref, vt_ref, o_ref, s_ref, smax_ref, m_ref, acc_ref):
    length = k_ref.shape[0]
    n_chunks = length // ATTN_TK
    n_q = length // ATTN_TQ

    def scores(q_tile, j, slot):
        q_off = pl.multiple_of(q_tile * ATTN_TQ, ATTN_TQ)
        st = _mm(k_ref[j * ATTN_TK:(j + 1) * ATTN_TK, :], qt_ref[:, pl.ds(q_off, ATTN_TQ)])
        s_ref[slot] = st
        smax_ref[slot] = jnp.max(st, axis=0, keepdims=True)

    def update(j, slot):
        m_old = m_ref[...]
        m_new = jnp.maximum(m_old, smax_ref[slot])
        p = jnp.exp2(s_ref[slot] - m_new).astype(BF16)
        alpha = jnp.exp2(m_old - m_new)
        acc_ref[...] = alpha * acc_ref[...] + _mm(vt_ref[:, j * ATTN_TK:(j + 1) * ATTN_TK], p)
        m_ref[...] = m_new

    for j in range(ATTN_LOOKAHEAD):
        scores(0, j, j)

    def q_tile(tile):
        m_ref[...] = jnp.full(m_ref.shape, -jnp.inf, F32)
        acc_ref[...] = jnp.zeros(acc_ref.shape, F32)
        next_tile = lax.rem(tile + 1, n_q)
        for j in range(n_chunks):
            ahead = j + ATTN_LOOKAHEAD
            scores(tile if ahead < n_chunks else next_tile, ahead % n_chunks, ahead % ATTN_BUFFERS)
            update(j, j % ATTN_BUFFERS)
        o = acc_ref[:V_HEAD_DIM, :] / acc_ref[V_HEAD_DIM:V_HEAD_DIM + 1, :]
        o_ref[pl.ds(pl.multiple_of(tile * ATTN_TQ, ATTN_TQ), ATTN_TQ), :] = o.T.astype(BF16)

    def trip(t, carry):
        for i in range(ATTN_Q_PER_TRIP):
            q_tile(t * ATTN_Q_PER_TRIP + i)
        return carry

    lax.fori_loop(0, n_q // ATTN_Q_PER_TRIP, trip, 0)


def _attention(qt, k, vt, batch, length):
    assert length % (ATTN_TQ * ATTN_Q_PER_TRIP) == 0 and length % (ATTN_TK * ATTN_BUFFERS) == 0
    out = pl.pallas_call(
        _attn_body,
        grid=(batch, MLA_HEADS),
        in_specs=[pl.BlockSpec((None, QK_SLOT, length), lambda b, h: (b, h, 0)),
                  pl.BlockSpec((None, None, length, QK_SLOT), lambda b, h: (b, h, 0, 0)),
                  pl.BlockSpec((None, V_SLOT, length), lambda b, h: (b, h, 0))],
        out_specs=pl.BlockSpec((None, length, V_HEAD_DIM), lambda b, h: (b, 0, h)),
        out_shape=jax.ShapeDtypeStruct((batch, length, MLA_HEADS * V_HEAD_DIM), BF16),
        scratch_shapes=[pltpu.VMEM((ATTN_BUFFERS, ATTN_TK, ATTN_TQ), F32),
                        pltpu.VMEM((ATTN_BUFFERS, 1, ATTN_TQ), F32), pltpu.VMEM((1, ATTN_TQ), F32),
                        pltpu.VMEM((V_SLOT, ATTN_TQ), F32)],
        compiler_params=_params("parallel", "parallel"),
        name="mla_attn",
    )(qt, k, vt)
    return out.reshape(batch * length, MLA_HEADS * V_HEAD_DIM)


def _ret_proj_body(h_ref, g_ref, w_ref, gn_ref, cos_ref, sin_ref, q_ref, k_ref, v_ref, gate_ref):
    hn = _rms(h_ref[...], g_ref[...]).astype(BF16)
    cos = cos_ref[...]
    sin = sin_ref[...]
    hq = RET_HEADS * RET_DK
    hv = RET_HEADS * RET_DV
    half = RET_DK // 2

    def rope_store(dst, col0, mult):
        for hd in range(RET_HEADS):
            x = _mm(hn, w_ref[:, col0 + hd * RET_DK:col0 + (hd + 1) * RET_DK])
            x1, x2 = x[:, :half], x[:, half:]
            dst[hd, :, :half] = ((x1 * cos - x2 * sin) * mult).astype(BF16)
            dst[hd, :, half:] = ((x2 * cos + x1 * sin) * mult).astype(BF16)

    rope_store(q_ref, 0, 1.0)
    rope_store(k_ref, hq, RET_DK ** -0.5)
    v = _mm(hn, w_ref[:, 2 * hq:2 * hq + hv]).astype(BF16)
    gate = _mm(hn, w_ref[:, 2 * hq + hv:])
    gate = (gate * jax.nn.sigmoid(gate) * gn_ref[...]).astype(BF16)
    for hd in range(RET_HEADS):
        v_ref[hd] = v[:, hd * RET_DV:(hd + 1) * RET_DV]
        gate_ref[hd] = gate[:, hd * RET_DV:(hd + 1) * RET_DV]


def _ret_proj(h, g, w_in, gn_g, cos, sin, batch, length):
    n, d = h.shape
    nblk = length // ROW_TILE
    hv = RET_HEADS * RET_DV
    row = lambda width: pl.BlockSpec((ROW_TILE, width), lambda i: (i, 0))
    pos = pl.BlockSpec((ROW_TILE, RET_DK // 2), lambda i: (i % nblk, 0))
    heads = lambda dim: pl.BlockSpec((None, RET_HEADS, ROW_TILE, dim), lambda i: (i // nblk, 0, i % nblk, 0))
    shape = lambda dim: jax.ShapeDtypeStruct((batch, RET_HEADS, length, dim), BF16)
    return pl.pallas_call(
        _ret_proj_body,
        grid=(n // ROW_TILE,),
        in_specs=[row(d), _resident((1, d)), _resident_slice(w_in), _resident((1, hv)), pos, pos],
        out_specs=[heads(RET_DK), heads(RET_DK), heads(RET_DV), heads(RET_DV)],
        out_shape=[shape(RET_DK), shape(RET_DK), shape(RET_DV), shape(RET_DV)],
        compiler_params=_params("parallel"),
        name="ret_proj",
    )(h, g, w_in[0], gn_g, cos, sin)


_BWD, _FWD = 0, 1


def _ret_scan_body(lg_ref, q_ref, k_ref, v_ref, gate_ref, o_ref,
                   state_ref, sb_ref, o_scr, dmat_ref, xi_ref, zeta_ref, gc_ref):
    hd = pl.program_id(1)
    direction = pl.program_id(2)
    blk = pl.program_id(3)
    nblk = pl.num_programs(3)
    c = RET_CHUNK
    n_chunks = RET_BLOCK // c

    @pl.when(blk == 0)
    def _reset_state():
        state_ref[...] = jnp.zeros(state_ref.shape, F32)

    @pl.when((blk == 0) & (direction == _BWD))
    def _decay_tables():
        lg_b = lg_ref[_BWD, hd]
        lg_f = lg_ref[_FWD, hd]
        row = lax.broadcasted_iota(jnp.int32, (c, c), 0)
        col = lax.broadcasted_iota(jnp.int32, (c, c), 1)
        dmat_ref[...] = jnp.where(row >= col,
                                  jnp.exp(lg_f * jnp.maximum(row - col, 0).astype(F32)),
                                  jnp.exp(lg_b * jnp.maximum(col - row, 0).astype(F32)))
        qi = lax.broadcasted_iota(jnp.int32, xi_ref.shape[1:], 0)
        xi_ref[_FWD] = jnp.exp(lg_f * (qi + 1).astype(F32))
        xi_ref[_BWD] = jnp.exp(lg_b * (c - qi).astype(F32))
        kj = lax.broadcasted_iota(jnp.int32, zeta_ref.shape[1:], 0)
        zeta_ref[_FWD] = jnp.exp(lg_f * (c - 1 - kj).astype(F32))
        zeta_ref[_BWD] = jnp.exp(lg_b * kj.astype(F32))
        gc_ref[_FWD] = jnp.exp(jnp.zeros(gc_ref.shape[1:], F32) + lg_f * c)
        gc_ref[_BWD] = jnp.exp(jnp.zeros(gc_ref.shape[1:], F32) + lg_b * c)

    def absorb(ci, d):
        rows = pl.ds(ci * c, c)
        kz_t = (k_ref[rows, :].astype(F32) * zeta_ref[d]).T.astype(BF16)
        state_ref[...] = state_ref[...] * gc_ref[d] + _mm(kz_t, v_ref[rows, :])

    @pl.when(direction == _BWD)
    def _backward_states():
        first = (nblk - 1 - blk) * n_chunks
        for ci in reversed(range(n_chunks)):
            sb_ref[first + ci] = state_ref[...].astype(BF16)
            absorb(ci, _BWD)

    @pl.when(direction == _FWD)
    def _forward():
        first = blk * n_chunks
        for ci in range(n_chunks):
            rows = pl.ds(ci * c, c)
            q = q_ref[rows, :]
            s = lax.dot_general(q, k_ref[rows, :], _NT, preferred_element_type=F32) * dmat_ref[...]
            o_scr[rows, :] = (_mm(s.astype(BF16), v_ref[rows, :])
                              + _mm(q, state_ref[...].astype(BF16)) * xi_ref[_FWD]
                              + _mm(q, sb_ref[first + ci]) * xi_ref[_BWD])
            absorb(ci, _FWD)
            for r in range(ci * c, (ci + 1) * c, RET_NORM_ROWS):
                o = o_scr[pl.ds(r, RET_NORM_ROWS), :]
                dev = o - jnp.mean(o, axis=-1, keepdims=True)
                on = dev * lax.rsqrt(jnp.mean(dev * dev, axis=-1, keepdims=True) + EPS)
                gate = gate_ref[pl.ds(r, RET_NORM_ROWS), :].astype(F32)
                o_ref[pl.ds(r, RET_NORM_ROWS), :] = (gate * on).astype(BF16)


def _ret_scan(q, k, v, gate, lg, batch, length):
    hv = RET_HEADS * RET_DV
    assert length % RET_BLOCK == 0
    nblk = length // RET_BLOCK

    def sweep(b, h, d, i):
        return (b, h, d * i + (1 - d) * (nblk - 1 - i), 0)

    def fwd_only(b, h, d, i):
        return (b, h, d * i, 0)

    out = pl.pallas_call(
        _ret_scan_body,
        grid=(batch, RET_HEADS, 2, nblk),
        in_specs=[pl.BlockSpec(memory_space=pltpu.SMEM),
                  pl.BlockSpec((None, None, RET_BLOCK, RET_DK), fwd_only),
                  pl.BlockSpec((None, None, RET_BLOCK, RET_DK), sweep),
                  pl.BlockSpec((None, None, RET_BLOCK, RET_DV), sweep),
                  pl.BlockSpec((None, None, RET_BLOCK, RET_DV), fwd_only)],
        out_specs=pl.BlockSpec((None, RET_BLOCK, RET_DV), lambda b, h, d, i: (b, d * i, h)),
        out_shape=jax.ShapeDtypeStruct((batch, length, hv), BF16),
        scratch_shapes=[pltpu.VMEM((RET_DK, RET_DV), F32),
                        pltpu.VMEM((length // RET_CHUNK, RET_DK, RET_DV), BF16),
                        pltpu.VMEM((RET_BLOCK, RET_DV), F32),
                        pltpu.VMEM((RET_CHUNK, RET_CHUNK), F32),
                        pltpu.VMEM((2, RET_CHUNK, RET_DV), F32),
                        pltpu.VMEM((2, RET_CHUNK, RET_DK), F32),
                        pltpu.VMEM((2, 1, RET_DV), F32)],
        compiler_params=_params("parallel", "parallel", "arbitrary", "arbitrary"),
        name="ret_scan",
    )(lg, q, k, v, gate)
    return out.reshape(batch * length, hv)


def _swap_halves(a):
    half = a.shape[-1] // 2
    return jnp.concatenate([a[..., half:], a[..., :half]], axis=-1)


def _rope_slot_layout(a):
    pe = a[..., QK_NOPE_DIM:]
    return jnp.concatenate([a[..., :QK_NOPE_DIM], pe, _swap_halves(pe)], axis=-1)


def _mla_weights(w_dq, q_a_norm, w_uq, w_dkv, kv_a_norm, w_ukv, q_norm, k_norm):
    uq = w_uq.reshape(Q_LORA_RANK, MLA_HEADS, QK_HEAD_DIM)
    uq = _rope_slot_layout(uq).reshape(Q_LORA_RANK, MLA_HEADS * QK_SLOT)
    pe = w_dkv[:, KV_LORA_RANK:]
    dkv = jnp.concatenate([w_dkv[:, :KV_LORA_RANK], pe, _swap_halves(pe)], axis=-1)
    ukv = w_ukv.reshape(KV_LORA_RANK, MLA_HEADS, QK_NOPE_DIM + V_HEAD_DIM)
    wk = ukv[..., :QK_NOPE_DIM].reshape(KV_LORA_RANK, MLA_HEADS * QK_NOPE_DIM)
    wvt = ukv[..., QK_NOPE_DIM:].reshape(KV_LORA_RANK, MLA_HEADS * V_HEAD_DIM).T
    return {
        "dq": w_dq.astype(BF16), "qa": q_a_norm[None, :], "uqt": uq.T.astype(BF16),
        "dkv": dkv.astype(BF16), "kva": kv_a_norm[None, :],
        "k": wk.astype(BF16), "vt": wvt.astype(BF16),
        "qgt": jnp.broadcast_to(_rope_slot_layout(q_norm)[:, None], (QK_SLOT, ROW_TILE)),
        "kg": _rope_slot_layout(k_norm)[None, :],
    }


def _rope_tables(length, dim):
    inv = ROPE_BASE ** (-jnp.arange(0, dim, 2, dtype=F32) / dim)
    ang = jnp.arange(length, dtype=F32)[:, None] * inv[None, :]
    return jnp.cos(ang), jnp.sin(ang)


def _trunk(x, p):
    batch, length, d = x.shape
    assert d == D_MODEL and length % max(ROW_TILE, FFN_FIRST_ROW_TILE, FFN_SECOND_ROW_TILE) == 0
    x = x.reshape(batch * length, d)
    cos_m, sin_m = _rope_tables(length, QK_ROPE_DIM)
    cs_mla = jnp.concatenate([cos_m, cos_m, -sin_m, sin_m], axis=-1)
    cos_r, sin_r = _rope_tables(length, RET_DK)
    for i in range(DEPTH):
        j = i // 2
        ng = p["norm_g"][i]
        f = p["ffn"][i]
        h = _ffn_first(x, ng[0:1], f[0])
        if i % 2 == 0:
            w = p["mla"][j]
            qt, k, vt = _mla_proj(h, ng[1:2], w, cs_mla, batch, length)
            o = _attention(qt, k, vt, batch, length)
        else:
            w = p["ret"][j]
            q, k, v, gate = _ret_proj(h, ng[1:2], w["in"], w["gn"], cos_r, sin_r, batch, length)
            o = _ret_scan(q, k, v, gate, w["lg"], batch, length)
        x = _ffn_second(h, o, w["o"], ng[2:3], f[1], ng[3:4])
    return x.reshape(batch, length, d)


def kernel(x_prompt, x_sample, norm_g, ffn_w_gate, ffn_w_up, ffn_w_down, mla_w_dq, mla_q_a_norm, mla_w_uq, mla_w_dkv, mla_kv_a_norm, mla_w_ukv, mla_q_norm, mla_k_norm, mla_w_o, ret_w_in, ret_decay_fwd, ret_decay_bwd, ret_gn_g, ret_w_o):
    wg, wu, wd = ffn_w_gate.astype(BF16), ffn_w_up.astype(BF16), ffn_w_down.astype(BF16)
    mla_o, ret_in, ret_o = mla_w_o.astype(BF16), ret_w_in.astype(BF16), ret_w_o.astype(BF16)
    p = {
        "norm_g": norm_g,
        "ffn": [[{"g": (wg, (i, s)), "u": (wu, (i, s)), "d": (wd, (i, s))} for s in range(2)]
                for i in range(DEPTH)],
        "mla": [], "ret": [],
    }
    for j in range(mla_w_dq.shape[0]):
        w = _mla_weights(mla_w_dq[j], mla_q_a_norm[j], mla_w_uq[j], mla_w_dkv[j], mla_kv_a_norm[j],
                         mla_w_ukv[j], mla_q_norm[j], mla_k_norm[j])
        w["o"] = (mla_o, (j,))
        p["mla"].append(w)
    for j in range(ret_w_in.shape[0]):
        lg = jnp.stack([jnp.log1p(-jnp.exp(ret_decay_bwd[j].astype(F32))),
                        jnp.log1p(-jnp.exp(ret_decay_fwd[j].astype(F32)))])
        p["ret"].append({"in": (ret_in, (j,)), "lg": lg, "gn": ret_gn_g[j][None, :], "o": (ret_o, (j,))})
    return _trunk(x_prompt, p), _trunk(x_sample, p)
```

```python
import math

import jax
import jax.numpy as jnp
from jax import lax
from jax.experimental import pallas as pl
from jax.experimental.pallas import tpu as pltpu

F32 = jnp.float32
BF16 = jnp.bfloat16

D_MODEL = 1024
DEPTH = 4
D_FF = 2816
MLA_HEADS = 8
QK_NOPE_DIM = 128
QK_ROPE_DIM = 64
QK_HEAD_DIM = QK_NOPE_DIM + QK_ROPE_DIM
V_HEAD_DIM = 128
Q_LORA_RANK = 384
KV_LORA_RANK = 256
ROPE_BASE = 10000.0
RET_HEADS = 4
RET_DK = D_MODEL // RET_HEADS
RET_DV = 2 * D_MODEL // RET_HEADS
EPS = 1e-6

LANES_V7X = 128
MXU_DIM_V7X = 256
VMEM_LIMIT_BYTES_V7X = 56 * 1024 * 1024

QK_SLOT = QK_NOPE_DIM + 2 * QK_ROPE_DIM
assert QK_SLOT == MXU_DIM_V7X
BF16_SUBLANES_V7X = 16
V_SLOT = V_HEAD_DIM + BF16_SUBLANES_V7X

ROW_TILE = 1024
FFN_FIRST_ROW_TILE = 1024
FFN_SECOND_ROW_TILE = 1024
FFN_SECOND_VMEM_LIMIT_BYTES_V7X = 62 * 1024 * 1024
ATTN_TQ = 512
ATTN_TK = 512
ATTN_LOOKAHEAD = 2
ATTN_BUFFERS = 4
ATTN_Q_PER_TRIP = 2
RET_CHUNK = MXU_DIM_V7X
RET_BLOCK = 2048
RET_NORM_ROWS = 64

_NT = (((1,), (1,)), ((), ()))


def _params(*sem, vmem_limit_bytes=VMEM_LIMIT_BYTES_V7X):
    return pltpu.CompilerParams(dimension_semantics=sem, vmem_limit_bytes=vmem_limit_bytes)


def _resident(shape):
    return pl.BlockSpec(shape, lambda *_: (0,) * len(shape), pipeline_mode=pl.Buffered(1))


def _resident_slice(stacked):
    array, index = stacked
    block = (None,) * len(index) + array.shape[len(index):]
    at = tuple(index) + (0,) * (array.ndim - len(index))
    return pl.BlockSpec(block, lambda *_: at, pipeline_mode=pl.Buffered(1))


def _rms(x, g):
    return x * lax.rsqrt(jnp.mean(x * x, axis=-1, keepdims=True) + EPS) * g


def _mm(a, b):
    return jnp.dot(a, b, preferred_element_type=F32)


def _swiglu_half(x, g_ref, wg_ref, wu_ref, wd_ref):
    xn = _rms(x, g_ref[...]).astype(BF16)
    gate = _mm(xn, wg_ref[...])
    up = _mm(xn, wu_ref[...])
    h = (gate * jax.nn.sigmoid(gate) * up).astype(BF16)
    return x + 0.5 * _mm(h, wd_ref[...])


def _ffn_first_body(x_ref, g_ref, wg_ref, wu_ref, wd_ref, o_ref):
    o_ref[...] = _swiglu_half(x_ref[...], g_ref, wg_ref, wu_ref, wd_ref)


def _ffn_second_body(x_ref, a_ref, wo_ref, g_ref, wg_ref, wu_ref, wd_ref, gp_ref, o_ref):
    h = x_ref[...] + _mm(a_ref[...], wo_ref[...])
    o_ref[...] = _rms(_swiglu_half(h, g_ref, wg_ref, wu_ref, wd_ref), gp_ref[...])


def _ffn_first(x, g, f):
    n, d = x.shape
    row = pl.BlockSpec((FFN_FIRST_ROW_TILE, d), lambda i: (i, 0))
    return pl.pallas_call(
        _ffn_first_body,
        grid=(n // FFN_FIRST_ROW_TILE,),
        in_specs=[row, _resident((1, d)), _resident_slice(f["g"]), _resident_slice(f["u"]),
                  _resident_slice(f["d"])],
        out_specs=row,
        out_shape=jax.ShapeDtypeStruct((n, d), F32),
        compiler_params=_params("parallel"),
        name="ffn_first",
    )(x, g, f["g"][0], f["u"][0], f["d"][0])


def _ffn_second(x, a, w_o, g, f, g_post):
    n, d = x.shape
    row = pl.BlockSpec((FFN_SECOND_ROW_TILE, d), lambda i: (i, 0))
    return pl.pallas_call(
        _ffn_second_body,
        grid=(n // FFN_SECOND_ROW_TILE,),
        in_specs=[row, pl.BlockSpec((FFN_SECOND_ROW_TILE, a.shape[1]), lambda i: (i, 0)), _resident_slice(w_o),
                  _resident((1, d)), _resident_slice(f["g"]), _resident_slice(f["u"]),
                  _resident_slice(f["d"]), _resident((1, d))],
        out_specs=row,
        out_shape=jax.ShapeDtypeStruct((n, d), F32),
        compiler_params=_params("parallel", vmem_limit_bytes=FFN_SECOND_VMEM_LIMIT_BYTES_V7X),
        name="ffn_second",
    )(x, a, w_o[0], g, f["g"][0], f["u"][0], f["d"][0], g_post)


def _rope_slot(t, gain, cs):
    a = t * gain * cs
    lane = lax.broadcasted_iota(jnp.int32, a.shape, 1)
    return jnp.where(lane < QK_ROPE_DIM, a + pltpu.roll(a, QK_ROPE_DIM, axis=1), 0.0)


def _sumsq_rope(t):
    lane = lax.broadcasted_iota(jnp.int32, t.shape, 1)
    return jnp.sum(jnp.where(lane < QK_ROPE_DIM, t * t, 0.0), axis=-1, keepdims=True)


def _mla_proj_body(h_ref, g_ref, wdq_ref, qa_ref, wuqt_ref, wdkv_ref, kva_ref, wk_ref, wvt_ref,
                   qgt_ref, kg_ref, cs_ref, cst_ref, qt_ref, k_ref, vt_ref):
    hn = _rms(h_ref[...], g_ref[...]).astype(BF16)
    cs = cs_ref[...]
    kg = kg_ref[...]
    inv_dim = 1.0 / QK_HEAD_DIM

    ckv_full = _mm(hn, wdkv_ref[...])
    ckv = _rms(ckv_full[:, :KV_LORA_RANK], kva_ref[...]).astype(BF16)
    vt = lax.dot_general(wvt_ref[...], ckv, _NT, preferred_element_type=F32).astype(BF16)
    for hd in range(MLA_HEADS):
        vt_ref[hd * V_SLOT:hd * V_SLOT + V_HEAD_DIM, :] = vt[hd * V_HEAD_DIM:(hd + 1) * V_HEAD_DIM, :]
        vt_ref[hd * V_SLOT + V_HEAD_DIM:(hd + 1) * V_SLOT, :] = jnp.ones((BF16_SUBLANES_V7X, vt.shape[1]), BF16)
    k_pe_raw = ckv_full[:, KV_LORA_RANK:]
    k_pe_ss = _sumsq_rope(k_pe_raw)
    k_pe = _rope_slot(k_pe_raw, kg[:, QK_NOPE_DIM:], cs)
    cq = _rms(_mm(hn, wdq_ref[...]), qa_ref[...]).astype(BF16)
    scale = QK_HEAD_DIM ** -0.5 * math.log2(math.e)
    cst = cst_ref[...]
    qgt = qgt_ref[...]
    for hd in range(MLA_HEADS):
        if hd % 2 == 0:
            k_pair = _mm(ckv, wk_ref[:, hd * QK_NOPE_DIM:(hd + 2) * QK_NOPE_DIM])
        kn = k_pair[:, (hd % 2) * QK_NOPE_DIM:(hd % 2 + 1) * QK_NOPE_DIM]
        r = lax.rsqrt((jnp.sum(kn * kn, axis=-1, keepdims=True) + k_pe_ss) * inv_dim + EPS)
        k_ref[hd, :, :QK_NOPE_DIM] = (kn * r * kg[:, :QK_NOPE_DIM]).astype(BF16)
        k_ref[hd, :, QK_NOPE_DIM:] = (k_pe * r).astype(BF16)
        base = hd * QK_SLOT
        qt = lax.dot_general(wuqt_ref[base:base + QK_SLOT, :], cq, _NT, preferred_element_type=F32)
        qn, qp = qt[:QK_NOPE_DIM], qt[QK_NOPE_DIM:]
        pe = qp[:QK_ROPE_DIM]
        ss = jnp.sum(qn * qn, axis=0, keepdims=True) + jnp.sum(pe * pe, axis=0, keepdims=True)
        r = lax.rsqrt(ss * inv_dim + EPS) * scale
        a = qp * qgt[QK_NOPE_DIM:] * cst
        qt_ref[base:base + QK_NOPE_DIM, :] = (qn * r * qgt[:QK_NOPE_DIM]).astype(BF16)
        qt_ref[base + QK_NOPE_DIM:base + QK_HEAD_DIM, :] = (
            (a[:QK_ROPE_DIM] + a[QK_ROPE_DIM:]) * r).astype(BF16)
        qt_ref[base + QK_HEAD_DIM:base + QK_SLOT, :] = jnp.zeros((QK_SLOT - QK_HEAD_DIM, qt.shape[1]), BF16)


def _mla_proj(h, g, w, cs, batch, length):
    n, d = h.shape
    nblk = length // ROW_TILE
    slots = MLA_HEADS * QK_SLOT
    dv = MLA_HEADS * V_SLOT
    row = lambda width: pl.BlockSpec((ROW_TILE, width), lambda i: (i, 0))
    return pl.pallas_call(
        _mla_proj_body,
        grid=(n // ROW_TILE,),
        in_specs=[row(d), _resident((1, d)), _resident(w["dq"].shape), _resident((1, Q_LORA_RANK)),
                  _resident(w["uqt"].shape), _resident(w["dkv"].shape), _resident((1, KV_LORA_RANK)),
                  _resident(w["k"].shape), _resident(w["vt"].shape),
                  _resident((QK_SLOT, ROW_TILE)), _resident((1, QK_SLOT)),
                  pl.BlockSpec((ROW_TILE, LANES_V7X), lambda i: (i % nblk, 0)),
                  pl.BlockSpec((LANES_V7X, ROW_TILE), lambda i: (0, i % nblk))],
        out_specs=[pl.BlockSpec((None, slots, ROW_TILE), lambda i: (i // nblk, 0, i % nblk)),
                   pl.BlockSpec((None, MLA_HEADS, ROW_TILE, QK_SLOT), lambda i: (i // nblk, 0, i % nblk, 0)),
                   pl.BlockSpec((None, dv, ROW_TILE), lambda i: (i // nblk, 0, i % nblk))],
        out_shape=[jax.ShapeDtypeStruct((batch, slots, length), BF16),
                   jax.ShapeDtypeStruct((batch, MLA_HEADS, length, QK_SLOT), BF16),
                   jax.ShapeDtypeStruct((batch, dv, length), BF16)],
        compiler_params=_params("parallel"),
        name="mla_proj",
    )(h, g, w["dq"], w["qa"], w["uqt"], w["dkv"], w["kva"], w["k"], w["vt"], w["qgt"], w["kg"], cs, cs.T)


def _attn_body(qt_ref, k_ref, vt_ref, o_ref, s_ref, smax_ref, m_ref, acc_ref):
    length = k_ref.shape[0]
    n_chunks = length // ATTN_TK
    n_q = length // ATTN_TQ

    def scores(q_tile, j, slot):
        q_off = pl.multiple_of(q_tile * ATTN_TQ, ATTN_TQ)
        st = _mm(k_ref[j * ATTN_TK:(j + 1) * ATTN_TK, :], qt_ref[:, pl.ds(q_off, ATTN_TQ)])
        s_ref[slot] = st
        smax_ref[slot] = jnp.max(st, axis=0, keepdims=True)

    def update(j, slot):
        m_old = m_ref[...]
        m_new = jnp.maximum(m_old, smax_ref[slot])
        p = jnp.exp2(s_ref[slot] - m_new).astype(BF16)
        alpha = jnp.exp2(m_old - m_new)
        acc_ref[...] = alpha * acc_ref[...] + _mm(vt_ref[:, j * ATTN_TK:(j + 1) * ATTN_TK], p)
        m_ref[...] = m_new

    for j in range(ATTN_LOOKAHEAD):
        scores(0, j, j)

    def q_tile(tile):
        m_ref[...] = jnp.full(m_ref.shape, -jnp.inf, F32)
        acc_ref[...] = jnp.zeros(acc_ref.shape, F32)
        next_tile = lax.rem(tile + 1, n_q)
        for j in range(n_chunks):
            ahead = j + ATTN_LOOKAHEAD
            scores(tile if ahead < n_chunks else next_tile, ahead % n_chunks, ahead % ATTN_BUFFERS)
            update(j, j % ATTN_BUFFERS)
        o = acc_ref[:V_HEAD_DIM, :] / acc_ref[V_HEAD_DIM:V_HEAD_DIM + 1, :]
        o_ref[pl.ds(pl.multiple_of(tile * ATTN_TQ, ATTN_TQ), ATTN_TQ), :] = o.T.astype(BF16)

    def trip(t, carry):
        for i in range(ATTN_Q_PER_TRIP):
            q_tile(t * ATTN_Q_PER_TRIP + i)
        return carry

    lax.fori_loop(0, n_q // ATTN_Q_PER_TRIP, trip, 0)


def _attention(qt, k, vt, batch, length):
    assert length % (ATTN_TQ * ATTN_Q_PER_TRIP) == 0 and length % (ATTN_TK * ATTN_BUFFERS) == 0
    out = pl.pallas_call(
        _attn_body,
        grid=(batch, MLA_HEADS),
        in_specs=[pl.BlockSpec((None, QK_SLOT, length), lambda b, h: (b, h, 0)),
                  pl.BlockSpec((None, None, length, QK_SLOT), lambda b, h: (b, h, 0, 0)),
                  pl.BlockSpec((None, V_SLOT, length), lambda b, h: (b, h, 0))],
        out_specs=pl.BlockSpec((None, length, V_HEAD_DIM), lambda b, h: (b, 0, h)),
        out_shape=jax.ShapeDtypeStruct((batch, length, MLA_HEADS * V_HEAD_DIM), BF16),
        scratch_shapes=[pltpu.VMEM((ATTN_BUFFERS, ATTN_TK, ATTN_TQ), F32),
                        pltpu.VMEM((ATTN_BUFFERS, 1, ATTN_TQ), F32), pltpu.VMEM((1, ATTN_TQ), F32),
                        pltpu.VMEM((V_SLOT, ATTN_TQ), F32)],
        compiler_params=_params("parallel", "parallel"),
        name="mla_attn",
    )(qt, k, vt)
    return out.reshape(batch * length, MLA_HEADS * V_HEAD_DIM)


def _ret_proj_body(h_ref, g_ref, w_ref, gn_ref, cos_ref, sin_ref, q_ref, k_ref, v_ref, gate_ref):
    hn = _rms(h_ref[...], g_ref[...]).astype(BF16)
    cos = cos_ref[...]
    sin = sin_ref[...]
    hq = RET_HEADS * RET_DK
    hv = RET_HEADS * RET_DV
    half = RET_DK // 2

    def rope_store(dst, col0, mult):
        for hd in range(RET_HEADS):
            x = _mm(hn, w_ref[:, col0 + hd * RET_DK:col0 + (hd + 1) * RET_DK])
            x1, x2 = x[:, :half], x[:, half:]
            dst[hd, :, :half] = ((x1 * cos - x2 * sin) * mult).astype(BF16)
            dst[hd, :, half:] = ((x2 * cos + x1 * sin) * mult).astype(BF16)

    rope_store(q_ref, 0, 1.0)
    rope_store(k_ref, hq, RET_DK ** -0.5)
    v = _mm(hn, w_ref[:, 2 * hq:2 * hq + hv]).astype(BF16)
    gate = _mm(hn, w_ref[:, 2 * hq + hv:])
    gate = (gate * jax.nn.sigmoid(gate) * gn_ref[...]).astype(BF16)
    for hd in range(RET_HEADS):
        v_ref[hd] = v[:, hd * RET_DV:(hd + 1) * RET_DV]
        gate_ref[hd] = gate[:, hd * RET_DV:(hd + 1) * RET_DV]


def _ret_proj(h, g, w_in, gn_g, cos, sin, batch, length):
    n, d = h.shape
    nblk = length // ROW_TILE
    hv = RET_HEADS * RET_DV
    row = lambda width: pl.BlockSpec((ROW_TILE, width), lambda i: (i, 0))
    pos = pl.BlockSpec((ROW_TILE, RET_DK // 2), lambda i: (i % nblk, 0))
    heads = lambda dim: pl.BlockSpec((None, RET_HEADS, ROW_TILE, dim), lambda i: (i // nblk, 0, i % nblk, 0))
    shape = lambda dim: jax.ShapeDtypeStruct((batch, RET_HEADS, length, dim), BF16)
    return pl.pallas_call(
        _ret_proj_body,
        grid=(n // ROW_TILE,),
        in_specs=[row(d), _resident((1, d)), _resident_slice(w_in), _resident((1, hv)), pos, pos],
        out_specs=[heads(RET_DK), heads(RET_DK), heads(RET_DV), heads(RET_DV)],
        out_shape=[shape(RET_DK), shape(RET_DK), shape(RET_DV), shape(RET_DV)],
        compiler_params=_params("parallel"),
        name="ret_proj",
    )(h, g, w_in[0], gn_g, cos, sin)


_BWD, _FWD = 0, 1


def _ret_scan_body(lg_ref, q_ref, k_ref, v_ref, gate_ref, o_ref,
                   state_ref, sb_ref, o_scr, dmat_ref, xi_ref, zeta_ref, gc_ref):
    hd = pl.program_id(1)
    direction = pl.program_id(2)
    blk = pl.program_id(3)
    nblk = pl.num_programs(3)
    c = RET_CHUNK
    n_chunks = RET_BLOCK // c

    @pl.when(blk == 0)
    def _reset_state():
        state_ref[...] = jnp.zeros(state_ref.shape, F32)

    @pl.when((blk == 0) & (direction == _BWD))
    def _decay_tables():
        lg_b = lg_ref[_BWD, hd]
        lg_f = lg_ref[_FWD, hd]
        row = lax.broadcasted_iota(jnp.int32, (c, c), 0)
        col = lax.broadcasted_iota(jnp.int32, (c, c), 1)
        dmat_ref[...] = jnp.where(row >= col,
                                  jnp.exp(lg_f * jnp.maximum(row - col, 0).astype(F32)),
                                  jnp.exp(lg_b * jnp.maximum(col - row, 0).astype(F32)))
        qi = lax.broadcasted_iota(jnp.int32, xi_ref.shape[1:], 0)
        xi_ref[_FWD] = jnp.exp(lg_f * (qi + 1).astype(F32))
        xi_ref[_BWD] = jnp.exp(lg_b * (c - qi).astype(F32))
        kj = lax.broadcasted_iota(jnp.int32, zeta_ref.shape[1:], 0)
        zeta_ref[_FWD] = jnp.exp(lg_f * (c - 1 - kj).astype(F32))
        zeta_ref[_BWD] = jnp.exp(lg_b * kj.astype(F32))
        gc_ref[_FWD] = jnp.exp(jnp.zeros(gc_ref.shape[1:], F32) + lg_f * c)
        gc_ref[_BWD] = jnp.exp(jnp.zeros(gc_ref.shape[1:], F32) + lg_b * c)

    def absorb(ci, d):
        rows = pl.ds(ci * c, c)
        kz_t = (k_ref[rows, :].astype(F32) * zeta_ref[d]).T.astype(BF16)
        state_ref[...] = state_ref[...] * gc_ref[d] + _mm(kz_t, v_ref[rows, :])

    @pl.when(direction == _BWD)
    def _backward_states():
        first = (nblk - 1 - blk) * n_chunks
        for ci in reversed(range(n_chunks)):
            sb_ref[first + ci] = state_ref[...].astype(BF16)
            absorb(ci, _BWD)

    @pl.when(direction == _FWD)
    def _forward():
        first = blk * n_chunks
        for ci in range(n_chunks):
            rows = pl.ds(ci * c, c)
            q = q_ref[rows, :]
            s = lax.dot_general(q, k_ref[rows, :], _NT, preferred_element_type=F32) * dmat_ref[...]
            o_scr[rows, :] = (_mm(s.astype(BF16), v_ref[rows, :])
                              + _mm(q, state_ref[...].astype(BF16)) * xi_ref[_FWD]
                              + _mm(q, sb_ref[first + ci]) * xi_ref[_BWD])
            absorb(ci, _FWD)
            for r in range(ci * c, (ci + 1) * c, RET_NORM_ROWS):
                o = o_scr[pl.ds(r, RET_NORM_ROWS), :]
                dev = o - jnp.mean(o, axis=-1, keepdims=True)
                on = dev * lax.rsqrt(jnp.mean(dev * dev, axis=-1, keepdims=True) + EPS)
                gate = gate_ref[pl.ds(r, RET_NORM_ROWS), :].astype(F32)
                o_ref[pl.ds(r, RET_NORM_ROWS), :] = (gate * on).astype(BF16)


def _ret_scan(q, k, v, gate, lg, batch, length):
    hv = RET_HEADS * RET_DV
    assert length % RET_BLOCK == 0
    nblk = length // RET_BLOCK

    def sweep(b, h, d, i):
        return (b, h, d * i + (1 - d) * (nblk - 1 - i), 0)

    def fwd_only(b, h, d, i):
        return (b, h, d * i, 0)

    out = pl.pallas_call(
        _ret_scan_body,
        grid=(batch, RET_HEADS, 2, nblk),
        in_specs=[pl.BlockSpec(memory_space=pltpu.SMEM),
                  pl.BlockSpec((None, None, RET_BLOCK, RET_DK), fwd_only),
                  pl.BlockSpec((None, None, RET_BLOCK, RET_DK), sweep),
                  pl.BlockSpec((None, None, RET_BLOCK, RET_DV), sweep),
                  pl.BlockSpec((None, None, RET_BLOCK, RET_DV), fwd_only)],
        out_specs=pl.BlockSpec((None, RET_BLOCK, RET_DV), lambda b, h, d, i: (b, d * i, h)),
        out_shape=jax.ShapeDtypeStruct((batch, length, hv), BF16),
        scratch_shapes=[pltpu.VMEM((RET_DK, RET_DV), F32),
                        pltpu.VMEM((length // RET_CHUNK, RET_DK, RET_DV), BF16),
                        pltpu.VMEM((RET_BLOCK, RET_DV), F32),
                        pltpu.VMEM((RET_CHUNK, RET_CHUNK), F32),
                        pltpu.VMEM((2, RET_CHUNK, RET_DV), F32),
                        pltpu.VMEM((2, RET_CHUNK, RET_DK), F32),
                        pltpu.VMEM((2, 1, RET_DV), F32)],
        compiler_params=_params("parallel", "parallel", "arbitrary", "arbitrary"),
        name="ret_scan",
    )(lg, q, k, v, gate)
    return out.reshape(batch * length, hv)


def _swap_halves(a):
    half = a.shape[-1] // 2
    return jnp.concatenate([a[..., half:], a[..., :half]], axis=-1)


def _rope_slot_layout(a):
    pe = a[..., QK_NOPE_DIM:]
    return jnp.concatenate([a[..., :QK_NOPE_DIM], pe, _swap_halves(pe)], axis=-1)


def _mla_weights(w_dq, q_a_norm, w_uq, w_dkv, kv_a_norm, w_ukv, q_norm, k_norm):
    uq = w_uq.reshape(Q_LORA_RANK, MLA_HEADS, QK_HEAD_DIM)
    uq = _rope_slot_layout(uq).reshape(Q_LORA_RANK, MLA_HEADS * QK_SLOT)
    pe = w_dkv[:, KV_LORA_RANK:]
    dkv = jnp.concatenate([w_dkv[:, :KV_LORA_RANK], pe, _swap_halves(pe)], axis=-1)
    ukv = w_ukv.reshape(KV_LORA_RANK, MLA_HEADS, QK_NOPE_DIM + V_HEAD_DIM)
    wk = ukv[..., :QK_NOPE_DIM].reshape(KV_LORA_RANK, MLA_HEADS * QK_NOPE_DIM)
    wvt = ukv[..., QK_NOPE_DIM:].reshape(KV_LORA_RANK, MLA_HEADS * V_HEAD_DIM).T
    return {
        "dq": w_dq.astype(BF16), "qa": q_a_norm[None, :], "uqt": uq.T.astype(BF16),
        "dkv": dkv.astype(BF16), "kva": kv_a_norm[None, :],
        "k": wk.astype(BF16), "vt": wvt.astype(BF16),
        "qgt": jnp.broadcast_to(_rope_slot_layout(q_norm)[:, None], (QK_SLOT, ROW_TILE)),
        "kg": _rope_slot_layout(k_norm)[None, :],
    }


def _rope_tables(length, dim):
    inv = ROPE_BASE ** (-jnp.arange(0, dim, 2, dtype=F32) / dim)
    ang = jnp.arange(length, dtype=F32)[:, None] * inv[None, :]
    return jnp.cos(ang), jnp.sin(ang)


def _trunk(x, p):
    batch, length, d = x.shape
    assert d == D_MODEL and length % max(ROW_TILE, FFN_FIRST_ROW_TILE, FFN_SECOND_ROW_TILE) == 0
    x = x.reshape(batch * length, d)
    cos_m, sin_m = _rope_tables(length, QK_ROPE_DIM)
    cs_mla = jnp.concatenate([cos_m, cos_m, -sin_m, sin_m], axis=-1)
    cos_r, sin_r = _rope_tables(length, RET_DK)
    for i in range(DEPTH):
        j = i // 2
        ng = p["norm_g"][i]
        f = p["ffn"][i]
        h = _ffn_first(x, ng[0:1], f[0])
        if i % 2 == 0:
            w = p["mla"][j]
            qt, k, vt = _mla_proj(h, ng[1:2], w, cs_mla, batch, length)
            o = _attention(qt, k, vt, batch, length)
        else:
            w = p["ret"][j]
            q, k, v, gate = _ret_proj(h, ng[1:2], w["in"], w["gn"], cos_r, sin_r, batch, length)
            o = _ret_scan(q, k, v, gate, w["lg"], batch, length)
        x = _ffn_second(h, o, w["o"], ng[2:3], f[1], ng[3:4])
    return x.reshape(batch, length, d)


def kernel(x_prompt, x_sample, norm_g, ffn_w_gate, ffn_w_up, ffn_w_down, mla_w_dq, mla_q_a_norm, mla_w_uq, mla_w_dkv, mla_kv_a_norm, mla_w_ukv, mla_q_norm, mla_k_norm, mla_w_o, ret_w_in, ret_decay_fwd, ret_decay_bwd, ret_gn_g, ret_w_o):
    wg, wu, wd = ffn_w_gate.astype(BF16), ffn_w_up.astype(BF16), ffn_w_down.astype(BF16)
    mla_o, ret_in, ret_o = mla_w_o.astype(BF16), ret_w_in.astype(BF16), ret_w_o.astype(BF16)
    p = {
        "norm_g": norm_g,
        "ffn": [[{"g": (wg, (i, s)), "u": (wu, (i, s)), "d": (wd, (i, s))} for s in range(2)]
                for i in range(DEPTH)],
        "mla": [], "ret": [],
    }
    for j in range(mla_w_dq.shape[0]):
        w = _mla_weights(mla_w_dq[j], mla_q_a_norm[j], mla_w_uq[j], mla_w_dkv[j], mla_kv_a_norm[j],
                         mla_w_ukv[j], mla_q_norm[j], mla_k_norm[j])
        w["o"] = (mla_o, (j,))
        p["mla"].append(w)
    for j in range(ret_w_in.shape[0]):
        lg = jnp.stack([jnp.log1p(-jnp.exp(ret_decay_bwd[j].astype(F32))),
                        jnp.log1p(-jnp.exp(ret_decay_fwd[j].astype(F32)))])
        p["ret"].append({"in": (ret_in, (j,)), "lg": lg, "gn": ret_gn_g[j][None, :], "o": (ret_o, (j,))})
    return _trunk(x_prompt, p), _trunk(x_sample, p)
```

```python
import math

import jax
import jax.numpy as jnp
from jax import lax
from jax.experimental import pallas as pl
from jax.experimental.pallas import tpu as pltpu

F32 = jnp.float32
BF16 = jnp.bfloat16

D_MODEL = 1024
DEPTH = 4
D_FF = 2816
MLA_HEADS = 8
QK_NOPE_DIM = 128
QK_ROPE_DIM = 64
QK_HEAD_DIM = QK_NOPE_DIM + QK_ROPE_DIM
V_HEAD_DIM = 128
Q_LORA_RANK = 384
KV_LORA_RANK = 256
ROPE_BASE = 10000.0
RET_HEADS = 4
RET_DK = D_MODEL // RET_HEADS
RET_DV = 2 * D_MODEL // RET_HEADS
EPS = 1e-6

LANES_V7X = 128
MXU_DIM_V7X = 256
VMEM_LIMIT_BYTES_V7X = 56 * 1024 * 1024

QK_SLOT = QK_NOPE_DIM + 2 * QK_ROPE_DIM
assert QK_SLOT == MXU_DIM_V7X
BF16_SUBLANES_V7X = 16
V_SLOT = V_HEAD_DIM + BF16_SUBLANES_V7X

ROW_TILE = 1024
FFN_FIRST_ROW_TILE = 1024
FFN_SECOND_ROW_TILE = 1024
FFN_SECOND_VMEM_LIMIT_BYTES_V7X = 62 * 1024 * 1024
ATTN_TQ = 512
ATTN_TK = 512
ATTN_LOOKAHEAD = 2
ATTN_BUFFERS = 4
ATTN_Q_PER_TRIP = 2
RET_CHUNK = MXU_DIM_V7X
RET_BLOCK = 4096
RET_NORM_ROWS = 64

_NT = (((1,), (1,)), ((), ()))


def _params(*sem, vmem_limit_bytes=VMEM_LIMIT_BYTES_V7X):
    return pltpu.CompilerParams(dimension_semantics=sem, vmem_limit_bytes=vmem_limit_bytes)


def _resident(shape):
    return pl.BlockSpec(shape, lambda *_: (0,) * len(shape), pipeline_mode=pl.Buffered(1))


def _resident_slice(stacked):
    array, index = stacked
    block = (None,) * len(index) + array.shape[len(index):]
    at = tuple(index) + (0,) * (array.ndim - len(index))
    return pl.BlockSpec(block, lambda *_: at, pipeline_mode=pl.Buffered(1))


def _rms(x, g):
    return x * lax.rsqrt(jnp.mean(x * x, axis=-1, keepdims=True) + EPS) * g


def _mm(a, b):
    return jnp.dot(a, b, preferred_element_type=F32)


def _swiglu_half(x, g_ref, wg_ref, wu_ref, wd_ref):
    xn = _rms(x, g_ref[...]).astype(BF16)
    gate = _mm(xn, wg_ref[...])
    up = _mm(xn, wu_ref[...])
    h = (gate * jax.nn.sigmoid(gate) * up).astype(BF16)
    return x + 0.5 * _mm(h, wd_ref[...])


def _ffn_first_body(x_ref, g_ref, wg_ref, wu_ref, wd_ref, o_ref):
    o_ref[...] = _swiglu_half(x_ref[...], g_ref, wg_ref, wu_ref, wd_ref)


def _ffn_second_body(x_ref, a_ref, wo_ref, g_ref, wg_ref, wu_ref, wd_ref, gp_ref, o_ref):
    h = x_ref[...] + _mm(a_ref[...], wo_ref[...])
    o_ref[...] = _rms(_swiglu_half(h, g_ref, wg_ref, wu_ref, wd_ref), gp_ref[...])


def _ffn_first(x, g, f):
    n, d = x.shape
    row = pl.BlockSpec((FFN_FIRST_ROW_TILE, d), lambda i: (i, 0))
    return pl.pallas_call(
        _ffn_first_body,
        grid=(n // FFN_FIRST_ROW_TILE,),
        in_specs=[row, _resident((1, d)), _resident_slice(f["g"]), _resident_slice(f["u"]),
                  _resident_slice(f["d"])],
        out_specs=row,
        out_shape=jax.ShapeDtypeStruct((n, d), F32),
        compiler_params=_params("parallel"),
        name="ffn_first",
    )(x, g, f["g"][0], f["u"][0], f["d"][0])


def _ffn_second(x, a, w_o, g, f, g_post):
    n, d = x.shape
    row = pl.BlockSpec((FFN_SECOND_ROW_TILE, d), lambda i: (i, 0))
    return pl.pallas_call(
        _ffn_second_body,
        grid=(n // FFN_SECOND_ROW_TILE,),
        in_specs=[row, pl.BlockSpec((FFN_SECOND_ROW_TILE, a.shape[1]), lambda i: (i, 0)), _resident_slice(w_o),
                  _resident((1, d)), _resident_slice(f["g"]), _resident_slice(f["u"]),
                  _resident_slice(f["d"]), _resident((1, d))],
        out_specs=row,
        out_shape=jax.ShapeDtypeStruct((n, d), F32),
        compiler_params=_params("parallel", vmem_limit_bytes=FFN_SECOND_VMEM_LIMIT_BYTES_V7X),
        name="ffn_second",
    )(x, a, w_o[0], g, f["g"][0], f["u"][0], f["d"][0], g_post)


def _rope_slot(t, gain, cs):
    a = t * gain * cs
    lane = lax.broadcasted_iota(jnp.int32, a.shape, 1)
    return jnp.where(lane < QK_ROPE_DIM, a + pltpu.roll(a, QK_ROPE_DIM, axis=1), 0.0)


def _sumsq_rope(t):
    lane = lax.broadcasted_iota(jnp.int32, t.shape, 1)
    return jnp.sum(jnp.where(lane < QK_ROPE_DIM, t * t, 0.0), axis=-1, keepdims=True)


def _mla_proj_body(h_ref, g_ref, wdq_ref, qa_ref, wuqt_ref, wdkv_ref, kva_ref, wk_ref, wvt_ref,
                   qgt_ref, kg_ref, cs_ref, cst_ref, qt_ref, k_ref, vt_ref):
    hn = _rms(h_ref[...], g_ref[...]).astype(BF16)
    cs = cs_ref[...]
    kg = kg_ref[...]
    inv_dim = 1.0 / QK_HEAD_DIM

    ckv_full = _mm(hn, wdkv_ref[...])
    ckv = _rms(ckv_full[:, :KV_LORA_RANK], kva_ref[...]).astype(BF16)
    vt = lax.dot_general(wvt_ref[...], ckv, _NT, preferred_element_type=F32).astype(BF16)
    for hd in range(MLA_HEADS):
        vt_ref[hd * V_SLOT:hd * V_SLOT + V_HEAD_DIM, :] = vt[hd * V_HEAD_DIM:(hd + 1) * V_HEAD_DIM, :]
        vt_ref[hd * V_SLOT + V_HEAD_DIM:(hd + 1) * V_SLOT, :] = jnp.ones((BF16_SUBLANES_V7X, vt.shape[1]), BF16)
    k_pe_raw = ckv_full[:, KV_LORA_RANK:]
    k_pe_ss = _sumsq_rope(k_pe_raw)
    k_pe = _rope_slot(k_pe_raw, kg[:, QK_NOPE_DIM:], cs)
    cq = _rms(_mm(hn, wdq_ref[...]), qa_ref[...]).astype(BF16)
    scale = QK_HEAD_DIM ** -0.5 * math.log2(math.e)
    cst = cst_ref[...]
    qgt = qgt_ref[...]
    for hd in range(MLA_HEADS):
        if hd % 2 == 0:
            k_pair = _mm(ckv, wk_ref[:, hd * QK_NOPE_DIM:(hd + 2) * QK_NOPE_DIM])
        kn = k_pair[:, (hd % 2) * QK_NOPE_DIM:(hd % 2 + 1) * QK_NOPE_DIM]
        r = lax.rsqrt((jnp.sum(kn * kn, axis=-1, keepdims=True) + k_pe_ss) * inv_dim + EPS)
        k_ref[hd, :, :QK_NOPE_DIM] = (kn * r * kg[:, :QK_NOPE_DIM]).astype(BF16)
        k_ref[hd, :, QK_NOPE_DIM:] = (k_pe * r).astype(BF16)
        base = hd * QK_SLOT
        qt = lax.dot_general(wuqt_ref[base:base + QK_SLOT, :], cq, _NT, preferred_element_type=F32)
        qn, qp = qt[:QK_NOPE_DIM], qt[QK_NOPE_DIM:]
        pe = qp[:QK_ROPE_DIM]
        ss = jnp.sum(qn * qn, axis=0, keepdims=True) + jnp.sum(pe * pe, axis=0, keepdims=True)
        r = lax.rsqrt(ss * inv_dim + EPS) * scale
        a = qp * qgt[QK_NOPE_DIM:] * cst
        qt_ref[base:base + QK_NOPE_DIM, :] = (qn * r * qgt[:QK_NOPE_DIM]).astype(BF16)
        qt_ref[base + QK_NOPE_DIM:base + QK_HEAD_DIM, :] = (
            (a[:QK_ROPE_DIM] + a[QK_ROPE_DIM:]) * r).astype(BF16)
        qt_ref[base + QK_HEAD_DIM:base + QK_SLOT, :] = jnp.zeros((QK_SLOT - QK_HEAD_DIM, qt.shape[1]), BF16)


def _mla_proj(h, g, w, cs, batch, length):
    n, d = h.shape
    nblk = length // ROW_TILE
    slots = MLA_HEADS * QK_SLOT
    dv = MLA_HEADS * V_SLOT
    row = lambda width: pl.BlockSpec((ROW_TILE, width), lambda i: (i, 0))
    return pl.pallas_call(
        _mla_proj_body,
        grid=(n // ROW_TILE,),
        in_specs=[row(d), _resident((1, d)), _resident(w["dq"].shape), _resident((1, Q_LORA_RANK)),
                  _resident(w["uqt"].shape), _resident(w["dkv"].shape), _resident((1, KV_LORA_RANK)),
                  _resident(w["k"].shape), _resident(w["vt"].shape),
                  _resident((QK_SLOT, ROW_TILE)), _resident((1, QK_SLOT)),
                  pl.BlockSpec((ROW_TILE, LANES_V7X), lambda i: (i % nblk, 0)),
                  pl.BlockSpec((LANES_V7X, ROW_TILE), lambda i: (0, i % nblk))],
        out_specs=[pl.BlockSpec((None, slots, ROW_TILE), lambda i: (i // nblk, 0, i % nblk)),
                   pl.BlockSpec((None, MLA_HEADS, ROW_TILE, QK_SLOT), lambda i: (i // nblk, 0, i % nblk, 0)),
                   pl.BlockSpec((None, dv, ROW_TILE), lambda i: (i // nblk, 0, i % nblk))],
        out_shape=[jax.ShapeDtypeStruct((batch, slots, length), BF16),
                   jax.ShapeDtypeStruct((batch, MLA_HEADS, length, QK_SLOT), BF16),
                   jax.ShapeDtypeStruct((batch, dv, length), BF16)],
        compiler_params=_params("parallel"),
        name="mla_proj",
    )(h, g, w["dq"], w["qa"], w["uqt"], w["dkv"], w["kva"], w["k"], w["vt"], w["qgt"], w["kg"], cs, cs.T)


def _attn_body(qt_ref, k_ref, vt_ref, o_ref, s_ref, smax_ref, m_ref, acc_ref):
    length = k_ref.shape[0]
    n_chunks = length // ATTN_TK
    n_q = length // ATTN_TQ

    def scores(q_tile, j, slot):
        q_off = pl.multiple_of(q_tile * ATTN_TQ, ATTN_TQ)
        st = _mm(k_ref[j * ATTN_TK:(j + 1) * ATTN_TK, :], qt_ref[:, pl.ds(q_off, ATTN_TQ)])
        s_ref[slot] = st
        smax_ref[slot] = jnp.max(st, axis=0, keepdims=True)

    def update(j, slot):
        m_old = m_ref[...]
        m_new = jnp.maximum(m_old, smax_ref[slot])
        p = jnp.exp2(s_ref[slot] - m_new).astype(BF16)
        alpha = jnp.exp2(m_old - m_new)
        acc_ref[...] = alpha * acc_ref[...] + _mm(vt_ref[:, j * ATTN_TK:(j + 1) * ATTN_TK], p)
        m_ref[...] = m_new

    for j in range(ATTN_LOOKAHEAD):
        scores(0, j, j)

    def q_tile(tile):
        m_ref[...] = jnp.full(m_ref.shape, -jnp.inf, F32)
        acc_ref[...] = jnp.zeros(acc_ref.shape, F32)
        next_tile = lax.rem(tile + 1, n_q)
        for j in range(n_chunks):
            ahead = j + ATTN_LOOKAHEAD
            scores(tile if ahead < n_chunks else next_tile, ahead % n_chunks, ahead % ATTN_BUFFERS)
            update(j, j % ATTN_BUFFERS)
        o = acc_ref[:V_HEAD_DIM, :] / acc_ref[V_HEAD_DIM:V_HEAD_DIM + 1, :]
        o_ref[pl.ds(pl.multiple_of(tile * ATTN_TQ, ATTN_TQ), ATTN_TQ), :] = o.T.astype(BF16)

    def trip(t, carry):
        for i in range(ATTN_Q_PER_TRIP):
            q_tile(t * ATTN_Q_PER_TRIP + i)
        return carry

    lax.fori_loop(0, n_q // ATTN_Q_PER_TRIP, trip, 0)


def _attention(qt, k, vt, batch, length):
    assert length % (ATTN_TQ * ATTN_Q_PER_TRIP) == 0 and length % (ATTN_TK * ATTN_BUFFERS) == 0
    out = pl.pallas_call(
        _attn_body,
        grid=(batch, MLA_HEADS),
        in_specs=[pl.BlockSpec((None, QK_SLOT, length), lambda b, h: (b, h, 0)),
                  pl.BlockSpec((None, None, length, QK_SLOT), lambda b, h: (b, h, 0, 0)),
                  pl.BlockSpec((None, V_SLOT, length), lambda b, h: (b, h, 0))],
        out_specs=pl.BlockSpec((None, length, V_HEAD_DIM), lambda b, h: (b, 0, h)),
        out_shape=jax.ShapeDtypeStruct((batch, length, MLA_HEADS * V_HEAD_DIM), BF16),
        scratch_shapes=[pltpu.VMEM((ATTN_BUFFERS, ATTN_TK, ATTN_TQ), F32),
                        pltpu.VMEM((ATTN_BUFFERS, 1, ATTN_TQ), F32), pltpu.VMEM((1, ATTN_TQ), F32),
                        pltpu.VMEM((V_SLOT, ATTN_TQ), F32)],
        compiler_params=_params("parallel", "parallel"),
        name="mla_attn",
    )(qt, k, vt)
    return out.reshape(batch * length, MLA_HEADS * V_HEAD_DIM)


def _ret_proj_body(h_ref, g_ref, w_ref, gn_ref, cos_ref, sin_ref, q_ref, k_ref, v_ref, gate_ref):
    hn = _rms(h_ref[...], g_ref[...]).astype(BF16)
    cos = cos_ref[...]
    sin = sin_ref[...]
    hq = RET_HEADS * RET_DK
    hv = RET_HEADS * RET_DV
    half = RET_DK // 2

    def rope_store(dst, col0, mult):
        for hd in range(RET_HEADS):
            x = _mm(hn, w_ref[:, col0 + hd * RET_DK:col0 + (hd + 1) * RET_DK])
            x1, x2 = x[:, :half], x[:, half:]
            dst[hd, :, :half] = ((x1 * cos - x2 * sin) * mult).astype(BF16)
            dst[hd, :, half:] = ((x2 * cos + x1 * sin) * mult).astype(BF16)

    rope_store(q_ref, 0, 1.0)
    rope_store(k_ref, hq, RET_DK ** -0.5)
    v = _mm(hn, w_ref[:, 2 * hq:2 * hq + hv]).astype(BF16)
    gate = _mm(hn, w_ref[:, 2 * hq + hv:])
    gate = (gate * jax.nn.sigmoid(gate) * gn_ref[...]).astype(BF16)
    for hd in range(RET_HEADS):
        v_ref[hd] = v[:, hd * RET_DV:(hd + 1) * RET_DV]
        gate_ref[hd] = gate[:, hd * RET_DV:(hd + 1) * RET_DV]


def _ret_proj(h, g, w_in, gn_g, cos, sin, batch, length):
    n, d = h.shape
    nblk = length // ROW_TILE
    hv = RET_HEADS * RET_DV
    row = lambda width: pl.BlockSpec((ROW_TILE, width), lambda i: (i, 0))
    pos = pl.BlockSpec((ROW_TILE, RET_DK // 2), lambda i: (i % nblk, 0))
    heads = lambda dim: pl.BlockSpec((None, RET_HEADS, ROW_TILE, dim), lambda i: (i // nblk, 0, i % nblk, 0))
    shape = lambda dim: jax.ShapeDtypeStruct((batch, RET_HEADS, length, dim), BF16)
    return pl.pallas_call(
        _ret_proj_body,
        grid=(n // ROW_TILE,),
        in_specs=[row(d), _resident((1, d)), _resident_slice(w_in), _resident((1, hv)), pos, pos],
        out_specs=[heads(RET_DK), heads(RET_DK), heads(RET_DV), heads(RET_DV)],
        out_shape=[shape(RET_DK), shape(RET_DK), shape(RET_DV), shape(RET_DV)],
        compiler_params=_params("parallel"),
        name="ret_proj",
    )(h, g, w_in[0], gn_g, cos, sin)


_BWD, _FWD = 0, 1


def _ret_scan_body(lg_ref, q_ref, k_ref, v_ref, gate_ref, o_ref,
                   state_ref, sb_ref, o_scr, dmat_ref, xi_ref, zeta_ref, gc_ref):
    hd = pl.program_id(1)
    direction = pl.program_id(2)
    blk = pl.program_id(3)
    nblk = pl.num_programs(3)
    c = RET_CHUNK
    n_chunks = RET_BLOCK // c

    @pl.when(blk == 0)
    def _reset_state():
        state_ref[...] = jnp.zeros(state_ref.shape, F32)

    @pl.when((blk == 0) & (direction == _BWD))
    def _decay_tables():
        lg_b = lg_ref[_BWD, hd]
        lg_f = lg_ref[_FWD, hd]
        row = lax.broadcasted_iota(jnp.int32, (c, c), 0)
        col = lax.broadcasted_iota(jnp.int32, (c, c), 1)
        dmat_ref[...] = jnp.where(row >= col,
                                  jnp.exp(lg_f * jnp.maximum(row - col, 0).astype(F32)),
                                  jnp.exp(lg_b * jnp.maximum(col - row, 0).astype(F32)))
        qi = lax.broadcasted_iota(jnp.int32, xi_ref.shape[1:], 0)
        xi_ref[_FWD] = jnp.exp(lg_f * (qi + 1).astype(F32))
        xi_ref[_BWD] = jnp.exp(lg_b * (c - qi).astype(F32))
        kj = lax.broadcasted_iota(jnp.int32, zeta_ref.shape[1:], 0)
        zeta_ref[_FWD] = jnp.exp(lg_f * (c - 1 - kj).astype(F32))
        zeta_ref[_BWD] = jnp.exp(lg_b * kj.astype(F32))
        gc_ref[_FWD] = jnp.exp(jnp.zeros(gc_ref.shape[1:], F32) + lg_f * c)
        gc_ref[_BWD] = jnp.exp(jnp.zeros(gc_ref.shape[1:], F32) + lg_b * c)

    def absorb(ci, d):
        rows = pl.ds(ci * c, c)
        kz_t = (k_ref[rows, :].astype(F32) * zeta_ref[d]).T.astype(BF16)
        state_ref[...] = state_ref[...] * gc_ref[d] + _mm(kz_t, v_ref[rows, :])

    @pl.when(direction == _BWD)
    def _backward_states():
        first = (nblk - 1 - blk) * n_chunks
        for ci in reversed(range(n_chunks)):
            sb_ref[first + ci] = state_ref[...].astype(BF16)
            absorb(ci, _BWD)

    @pl.when(direction == _FWD)
    def _forward():
        first = blk * n_chunks
        for ci in range(n_chunks):
            rows = pl.ds(ci * c, c)
            q = q_ref[rows, :]
            s = lax.dot_general(q, k_ref[rows, :], _NT, preferred_element_type=F32) * dmat_ref[...]
            o_scr[rows, :] = (_mm(s.astype(BF16), v_ref[rows, :])
                              + _mm(q, state_ref[...].astype(BF16)) * xi_ref[_FWD]
                              + _mm(q, sb_ref[first + ci]) * xi_ref[_BWD])
            absorb(ci, _FWD)
            for r in range(ci * c, (ci + 1) * c, RET_NORM_ROWS):
                o = o_scr[pl.ds(r, RET_NORM_ROWS), :]
                dev = o - jnp.mean(o, axis=-1, keepdims=True)
                on = dev * lax.rsqrt(jnp.mean(dev * dev, axis=-1, keepdims=True) + EPS)
                gate = gate_ref[pl.ds(r, RET_NORM_ROWS), :].astype(F32)
                o_ref[pl.ds(r, RET_NORM_ROWS), :] = (gate * on).astype(BF16)


def _ret_scan(q, k, v, gate, lg, batch, length):
    hv = RET_HEADS * RET_DV
    assert length % RET_BLOCK == 0
    nblk = length // RET_BLOCK

    def sweep(b, h, d, i):
        return (b, h, d * i + (1 - d) * (nblk - 1 - i), 0)

    def fwd_only(b, h, d, i):
        return (b, h, d * i, 0)

    out = pl.pallas_call(
        _ret_scan_body,
        grid=(batch, RET_HEADS, 2, nblk),
        in_specs=[pl.BlockSpec(memory_space=pltpu.SMEM),
                  pl.BlockSpec((None, None, RET_BLOCK, RET_DK), fwd_only),
                  pl.BlockSpec((None, None, RET_BLOCK, RET_DK), sweep),
                  pl.BlockSpec((None, None, RET_BLOCK, RET_DV), sweep),
                  pl.BlockSpec((None, None, RET_BLOCK, RET_DV), fwd_only)],
        out_specs=pl.BlockSpec((None, RET_BLOCK, RET_DV), lambda b, h, d, i: (b, d * i, h)),
        out_shape=jax.ShapeDtypeStruct((batch, length, hv), BF16),
        scratch_shapes=[pltpu.VMEM((RET_DK, RET_DV), F32),
                        pltpu.VMEM((length // RET_CHUNK, RET_DK, RET_DV), BF16),
                        pltpu.VMEM((RET_BLOCK, RET_DV), F32),
                        pltpu.VMEM((RET_CHUNK, RET_CHUNK), F32),
                        pltpu.VMEM((2, RET_CHUNK, RET_DV), F32),
                        pltpu.VMEM((2, RET_CHUNK, RET_DK), F32),
                        pltpu.VMEM((2, 1, RET_DV), F32)],
        compiler_params=_params("parallel", "parallel", "arbitrary", "arbitrary"),
        name="ret_scan",
    )(lg, q, k, v, gate)
    return out.reshape(batch * length, hv)


def _swap_halves(a):
    half = a.shape[-1] // 2
    return jnp.concatenate([a[..., half:], a[..., :half]], axis=-1)


def _rope_slot_layout(a):
    pe = a[..., QK_NOPE_DIM:]
    return jnp.concatenate([a[..., :QK_NOPE_DIM], pe, _swap_halves(pe)], axis=-1)


def _mla_weights(w_dq, q_a_norm, w_uq, w_dkv, kv_a_norm, w_ukv, q_norm, k_norm):
    uq = w_uq.reshape(Q_LORA_RANK, MLA_HEADS, QK_HEAD_DIM)
    uq = _rope_slot_layout(uq).reshape(Q_LORA_RANK, MLA_HEADS * QK_SLOT)
    pe = w_dkv[:, KV_LORA_RANK:]
    dkv = jnp.concatenate([w_dkv[:, :KV_LORA_RANK], pe, _swap_halves(pe)], axis=-1)
    ukv = w_ukv.reshape(KV_LORA_RANK, MLA_HEADS, QK_NOPE_DIM + V_HEAD_DIM)
    wk = ukv[..., :QK_NOPE_DIM].reshape(KV_LORA_RANK, MLA_HEADS * QK_NOPE_DIM)
    wvt = ukv[..., QK_NOPE_DIM:].reshape(KV_LORA_RANK, MLA_HEADS * V_HEAD_DIM).T
    return {
        "dq": w_dq.astype(BF16), "qa": q_a_norm[None, :], "uqt": uq.T.astype(BF16),
        "dkv": dkv.astype(BF16), "kva": kv_a_norm[None, :],
        "k": wk.astype(BF16), "vt": wvt.astype(BF16),
        "qgt": jnp.broadcast_to(_rope_slot_layout(q_norm)[:, None], (QK_SLOT, ROW_TILE)),
        "kg": _rope_slot_layout(k_norm)[None, :],
    }


def _rope_tables(length, dim):
    inv = ROPE_BASE ** (-jnp.arange(0, dim, 2, dtype=F32) / dim)
    ang = jnp.arange(length, dtype=F32)[:, None] * inv[None, :]
    return jnp.cos(ang), jnp.sin(ang)


def _trunk(x, p):
    batch, length, d = x.shape
    assert d == D_MODEL and length % max(ROW_TILE, FFN_FIRST_ROW_TILE, FFN_SECOND_ROW_TILE) == 0
    x = x.reshape(batch * length, d)
    cos_m, sin_m = _rope_tables(length, QK_ROPE_DIM)
    cs_mla = jnp.concatenate([cos_m, cos_m, -sin_m, sin_m], axis=-1)
    cos_r, sin_r = _rope_tables(length, RET_DK)
    for i in range(DEPTH):
        j = i // 2
        ng = p["norm_g"][i]
        f = p["ffn"][i]
        h = _ffn_first(x, ng[0:1], f[0])
        if i % 2 == 0:
            w = p["mla"][j]
            qt, k, vt = _mla_proj(h, ng[1:2], w, cs_mla, batch, length)
            o = _attention(qt, k, vt, batch, length)
        else:
            w = p["ret"][j]
            q, k, v, gate = _ret_proj(h, ng[1:2], w["in"], w["gn"], cos_r, sin_r, batch, length)
            o = _ret_scan(q, k, v, gate, w["lg"], batch, length)
        x = _ffn_second(h, o, w["o"], ng[2:3], f[1], ng[3:4])
    return x.reshape(batch, length, d)


def kernel(x_prompt, x_sample, norm_g, ffn_w_gate, ffn_w_up, ffn_w_down, mla_w_dq, mla_q_a_norm, mla_w_uq, mla_w_dkv, mla_kv_a_norm, mla_w_ukv, mla_q_norm, mla_k_norm, mla_w_o, ret_w_in, ret_decay_fwd, ret_decay_bwd, ret_gn_g, ret_w_o):
    wg, wu, wd = ffn_w_gate.astype(BF16), ffn_w_up.astype(BF16), ffn_w_down.astype(BF16)
    mla_o, ret_in, ret_o = mla_w_o.astype(BF16), ret_w_in.astype(BF16), ret_w_o.astype(BF16)
    p = {
        "norm_g": norm_g,
        "ffn": [[{"g": (wg, (i, s)), "u": (wu, (i, s)), "d": (wd, (i, s))} for s in range(2)]
                for i in range(DEPTH)],
        "mla": [], "ret": [],
    }
    for j in range(mla_w_dq.shape[0]):
        w = _mla_weights(mla_w_dq[j], mla_q_a_norm[j], mla_w_uq[j], mla_w_dkv[j], mla_kv_a_norm[j],
                         mla_w_ukv[j], mla_q_norm[j], mla_k_norm[j])
        w["o"] = (mla_o, (j,))
        p["mla"].append(w)
    for j in range(ret_w_in.shape[0]):
        lg = jnp.stack([jnp.log1p(-jnp.exp(ret_decay_bwd[j].astype(F32))),
                        jnp.log1p(-jnp.exp(ret_decay_fwd[j].astype(F32)))])
        p["ret"].append({"in": (ret_in, (j,)), "lg": lg, "gn": ret_gn_g[j][None, :], "o": (ret_o, (j,))})
    return _trunk(x_prompt, p), _trunk(x_sample, p)
```

```python
import math

import jax
import jax.numpy as jnp
from jax import lax
from jax.experimental import pallas as pl
from jax.experimental.pallas import tpu as pltpu

F32 = jnp.float32
BF16 = jnp.bfloat16

D_MODEL = 1024
DEPTH = 4
D_FF = 2816
MLA_HEADS = 8
QK_NOPE_DIM = 128
QK_ROPE_DIM = 64
QK_HEAD_DIM = QK_NOPE_DIM + QK_ROPE_DIM
V_HEAD_DIM = 128
Q_LORA_RANK = 384
KV_LORA_RANK = 256
ROPE_BASE = 10000.0
RET_HEADS = 4
RET_DK = D_MODEL // RET_HEADS
RET_DV = 2 * D_MODEL // RET_HEADS
EPS = 1e-6

LANES_V7X = 128
MXU_DIM_V7X = 256
VMEM_LIMIT_BYTES_V7X = 56 * 1024 * 1024

QK_SLOT = QK_NOPE_DIM + 2 * QK_ROPE_DIM
assert QK_SLOT == MXU_DIM_V7X
BF16_SUBLANES_V7X = 16
V_SLOT = V_HEAD_DIM + BF16_SUBLANES_V7X

ROW_TILE = 1024
FFN_FIRST_ROW_TILE = 1024
FFN_SECOND_ROW_TILE = 1024
FFN_SECOND_VMEM_LIMIT_BYTES_V7X = 62 * 1024 * 1024
ATTN_TQ = 512
ATTN_TK = 512
ATTN_LOOKAHEAD = 2
ATTN_BUFFERS = 4
ATTN_Q_PER_TRIP = 2
RET_CHUNK = MXU_DIM_V7X
RET_BLOCK = 4096
RET_NORM_ROWS = 64

_NT = (((1,), (1,)), ((), ()))


def _params(*sem, vmem_limit_bytes=VMEM_LIMIT_BYTES_V7X):
    return pltpu.CompilerParams(dimension_semantics=sem, vmem_limit_bytes=vmem_limit_bytes)


def _resident(shape):
    return pl.BlockSpec(shape, lambda *_: (0,) * len(shape), pipeline_mode=pl.Buffered(1))


def _resident_slice(stacked):
    array, index = stacked
    block = (None,) * len(index) + array.shape[len(index):]
    at = tuple(index) + (0,) * (array.ndim - len(index))
    return pl.BlockSpec(block, lambda *_: at, pipeline_mode=pl.Buffered(1))


def _rms(x, g):
    return x * lax.rsqrt(jnp.mean(x * x, axis=-1, keepdims=True) + EPS) * g


def _mm(a, b):
    return jnp.dot(a, b, preferred_element_type=F32)


def _swiglu_half(x, g_ref, wg_ref, wu_ref, wd_ref):
    xn = _rms(x, g_ref[...]).astype(BF16)
    gate = _mm(xn, wg_ref[...])
    up = _mm(xn, wu_ref[...])
    h = (gate * jax.nn.sigmoid(gate) * up).astype(BF16)
    return x + 0.5 * _mm(h, wd_ref[...])


def _ffn_first_body(x_ref, g_ref, wg_ref, wu_ref, wd_ref, o_ref):
    o_ref[...] = _swiglu_half(x_ref[...], g_ref, wg_ref, wu_ref, wd_ref)


def _ffn_second_body(x_ref, a_ref, wo_ref, g_ref, wg_ref, wu_ref, wd_ref, gp_ref, o_ref):
    h = x_ref[...] + _mm(a_ref[...], wo_ref[...])
    o_ref[...] = _rms(_swiglu_half(h, g_ref, wg_ref, wu_ref, wd_ref), gp_ref[...])


def _ffn_first(x, g, f):
    n, d = x.shape
    row = pl.BlockSpec((FFN_FIRST_ROW_TILE, d), lambda i: (i, 0))
    return pl.pallas_call(
        _ffn_first_body,
        grid=(n // FFN_FIRST_ROW_TILE,),
        in_specs=[row, _resident((1, d)), _resident_slice(f["g"]), _resident_slice(f["u"]),
                  _resident_slice(f["d"])],
        out_specs=row,
        out_shape=jax.ShapeDtypeStruct((n, d), F32),
        compiler_params=_params("parallel"),
        name="ffn_first",
    )(x, g, f["g"][0], f["u"][0], f["d"][0])


def _ffn_second(x, a, w_o, g, f, g_post):
    n, d = x.shape
    row = pl.BlockSpec((FFN_SECOND_ROW_TILE, d), lambda i: (i, 0))
    return pl.pallas_call(
        _ffn_second_body,
        grid=(n // FFN_SECOND_ROW_TILE,),
        in_specs=[row, pl.BlockSpec((FFN_SECOND_ROW_TILE, a.shape[1]), lambda i: (i, 0)), _resident_slice(w_o),
                  _resident((1, d)), _resident_slice(f["g"]), _resident_slice(f["u"]),
                  _resident_slice(f["d"]), _resident((1, d))],
        out_specs=row,
        out_shape=jax.ShapeDtypeStruct((n, d), F32),
        compiler_params=_params("parallel", vmem_limit_bytes=FFN_SECOND_VMEM_LIMIT_BYTES_V7X),
        name="ffn_second",
    )(x, a, w_o[0], g, f["g"][0], f["u"][0], f["d"][0], g_post)


def _rope_slot(t, gain, cs):
    a = t * gain * cs
    lane = lax.broadcasted_iota(jnp.int32, a.shape, 1)
    return jnp.where(lane < QK_ROPE_DIM, a + pltpu.roll(a, QK_ROPE_DIM, axis=1), 0.0)


def _sumsq_rope(t):
    lane = lax.broadcasted_iota(jnp.int32, t.shape, 1)
    return jnp.sum(jnp.where(lane < QK_ROPE_DIM, t * t, 0.0), axis=-1, keepdims=True)


def _mla_proj_body(h_ref, g_ref, wdq_ref, qa_ref, wuqt_ref, wdkv_ref, kva_ref, wk_ref, wvt_ref,
                   qgt_ref, kg_ref, cs_ref, cst_ref, qt_ref, k_ref, vt_ref):
    hn = _rms(h_ref[...], g_ref[...]).astype(BF16)
    cs = cs_ref[...]
    kg = kg_ref[...]
    inv_dim = 1.0 / QK_HEAD_DIM

    ckv_full = _mm(hn, wdkv_ref[...])
    ckv = _rms(ckv_full[:, :KV_LORA_RANK], kva_ref[...]).astype(BF16)
    vt = lax.dot_general(wvt_ref[...], ckv, _NT, preferred_element_type=F32).astype(BF16)
    for hd in range(MLA_HEADS):
        vt_ref[hd * V_SLOT:hd * V_SLOT + V_HEAD_DIM, :] = vt[hd * V_HEAD_DIM:(hd + 1) * V_HEAD_DIM, :]
        vt_ref[hd * V_SLOT + V_HEAD_DIM:(hd + 1) * V_SLOT, :] = jnp.ones((BF16_SUBLANES_V7X, vt.shape[1]), BF16)
    k_pe_raw = ckv_full[:, KV_LORA_RANK:]
    k_pe_ss = _sumsq_rope(k_pe_raw)
    k_pe = _rope_slot(k_pe_raw, kg[:, QK_NOPE_DIM:], cs)
    cq = _rms(_mm(hn, wdq_ref[...]), qa_ref[...]).astype(BF16)
    scale = QK_HEAD_DIM ** -0.5 * math.log2(math.e)
    cst = cst_ref[...]
    qgt = qgt_ref[...]
    for hd in range(MLA_HEADS):
        if hd % 2 == 0:
            k_pair = _mm(ckv, wk_ref[:, hd * QK_NOPE_DIM:(hd + 2) * QK_NOPE_DIM])
        kn = k_pair[:, (hd % 2) * QK_NOPE_DIM:(hd % 2 + 1) * QK_NOPE_DIM]
        r = lax.rsqrt((jnp.sum(kn * kn, axis=-1, keepdims=True) + k_pe_ss) * inv_dim + EPS)
        k_ref[hd, :, :QK_NOPE_DIM] = (kn * r).astype(BF16)
        k_ref[hd, :, QK_NOPE_DIM:] = (k_pe * r).astype(BF16)
        base = hd * QK_SLOT
        qt = lax.dot_general(wuqt_ref[base:base + QK_SLOT, :], cq, _NT, preferred_element_type=F32)
        qn, qp = qt[:QK_NOPE_DIM], qt[QK_NOPE_DIM:]
        pe = qp[:QK_ROPE_DIM]
        ss = jnp.sum(qn * qn, axis=0, keepdims=True) + jnp.sum(pe * pe, axis=0, keepdims=True)
        r = lax.rsqrt(ss * inv_dim + EPS) * scale
        a = qp * qgt[QK_NOPE_DIM:] * cst
        qt_ref[base:base + QK_NOPE_DIM, :] = (qn * r * qgt[:QK_NOPE_DIM]).astype(BF16)
        qt_ref[base + QK_NOPE_DIM:base + QK_HEAD_DIM, :] = (
            (a[:QK_ROPE_DIM] + a[QK_ROPE_DIM:]) * r).astype(BF16)
        qt_ref[base + QK_HEAD_DIM:base + QK_SLOT, :] = jnp.zeros((QK_SLOT - QK_HEAD_DIM, qt.shape[1]), BF16)


def _mla_proj(h, g, w, cs, batch, length):
    n, d = h.shape
    nblk = length // ROW_TILE
    slots = MLA_HEADS * QK_SLOT
    dv = MLA_HEADS * V_SLOT
    row = lambda width: pl.BlockSpec((ROW_TILE, width), lambda i: (i, 0))
    return pl.pallas_call(
        _mla_proj_body,
        grid=(n // ROW_TILE,),
        in_specs=[row(d), _resident((1, d)), _resident(w["dq"].shape), _resident((1, Q_LORA_RANK)),
                  _resident(w["uqt"].shape), _resident(w["dkv"].shape), _resident((1, KV_LORA_RANK)),
                  _resident(w["k"].shape), _resident(w["vt"].shape),
                  _resident((QK_SLOT, ROW_TILE)), _resident((1, QK_SLOT)),
                  pl.BlockSpec((ROW_TILE, LANES_V7X), lambda i: (i % nblk, 0)),
                  pl.BlockSpec((LANES_V7X, ROW_TILE), lambda i: (0, i % nblk))],
        out_specs=[pl.BlockSpec((None, slots, ROW_TILE), lambda i: (i // nblk, 0, i % nblk)),
                   pl.BlockSpec((None, MLA_HEADS, ROW_TILE, QK_SLOT), lambda i: (i // nblk, 0, i % nblk, 0)),
                   pl.BlockSpec((None, dv, ROW_TILE), lambda i: (i // nblk, 0, i % nblk))],
        out_shape=[jax.ShapeDtypeStruct((batch, slots, length), BF16),
                   jax.ShapeDtypeStruct((batch, MLA_HEADS, length, QK_SLOT), BF16),
                   jax.ShapeDtypeStruct((batch, dv, length), BF16)],
        compiler_params=_params("parallel"),
        name="mla_proj",
    )(h, g, w["dq"], w["qa"], w["uqt"], w["dkv"], w["kva"], w["k"], w["vt"], w["qgt"], w["kg"], cs, cs.T)


def _attn_body(qt_ref, k_ref, vt_ref, o_ref, s_ref, smax_ref, m_ref, acc_ref):
    length = k_ref.shape[0]
    n_chunks = length // ATTN_TK
    n_q = length // ATTN_TQ

    def scores(q_tile, j, slot):
        q_off = pl.multiple_of(q_tile * ATTN_TQ, ATTN_TQ)
        st = _mm(k_ref[j * ATTN_TK:(j + 1) * ATTN_TK, :], qt_ref[:, pl.ds(q_off, ATTN_TQ)])
        s_ref[slot] = st
        smax_ref[slot] = jnp.max(st, axis=0, keepdims=True)

    def update(j, slot):
        m_old = m_ref[...]
        m_new = jnp.maximum(m_old, smax_ref[slot])
        p = jnp.exp2(s_ref[slot] - m_new).astype(BF16)
        alpha = jnp.exp2(m_old - m_new)
        acc_ref[...] = alpha * acc_ref[...] + _mm(vt_ref[:, j * ATTN_TK:(j + 1) * ATTN_TK], p)
        m_ref[...] = m_new

    for j in range(ATTN_LOOKAHEAD):
        scores(0, j, j)

    def q_tile(tile):
        m_ref[...] = jnp.full(m_ref.shape, -jnp.inf, F32)
        acc_ref[...] = jnp.zeros(acc_ref.shape, F32)
        next_tile = lax.rem(tile + 1, n_q)
        for j in range(n_chunks):
            ahead = j + ATTN_LOOKAHEAD
            scores(tile if ahead < n_chunks else next_tile, ahead % n_chunks, ahead % ATTN_BUFFERS)
            update(j, j % ATTN_BUFFERS)
        o = acc_ref[:V_HEAD_DIM, :] / acc_ref[V_HEAD_DIM:V_HEAD_DIM + 1, :]
        o_ref[pl.ds(pl.multiple_of(tile * ATTN_TQ, ATTN_TQ), ATTN_TQ), :] = o.T.astype(BF16)

    def trip(t, carry):
        for i in range(ATTN_Q_PER_TRIP):
            q_tile(t * ATTN_Q_PER_TRIP + i)
        return carry

    lax.fori_loop(0, n_q // ATTN_Q_PER_TRIP, trip, 0)


def _attention(qt, k, vt, batch, length):
    assert length % (ATTN_TQ * ATTN_Q_PER_TRIP) == 0 and length % (ATTN_TK * ATTN_BUFFERS) == 0
    out = pl.pallas_call(
        _attn_body,
        grid=(batch, MLA_HEADS),
        in_specs=[pl.BlockSpec((None, QK_SLOT, length), lambda b, h: (b, h, 0)),
                  pl.BlockSpec((None, None, length, QK_SLOT), lambda b, h: (b, h, 0, 0)),
                  pl.BlockSpec((None, V_SLOT, length), lambda b, h: (b, h, 0))],
        out_specs=pl.BlockSpec((None, length, V_HEAD_DIM), lambda b, h: (b, 0, h)),
        out_shape=jax.ShapeDtypeStruct((batch, length, MLA_HEADS * V_HEAD_DIM), BF16),
        scratch_shapes=[pltpu.VMEM((ATTN_BUFFERS, ATTN_TK, ATTN_TQ), F32),
                        pltpu.VMEM((ATTN_BUFFERS, 1, ATTN_TQ), F32), pltpu.VMEM((1, ATTN_TQ), F32),
                        pltpu.VMEM((V_SLOT, ATTN_TQ), F32)],
        compiler_params=_params("parallel", "parallel"),
        name="mla_attn",
    )(qt, k, vt)
    return out.reshape(batch * length, MLA_HEADS * V_HEAD_DIM)


def _ret_proj_body(h_ref, g_ref, w_ref, gn_ref, cos_ref, sin_ref, q_ref, k_ref, v_ref, gate_ref):
    hn = _rms(h_ref[...], g_ref[...]).astype(BF16)
    cos = cos_ref[...]
    sin = sin_ref[...]
    hq = RET_HEADS * RET_DK
    hv = RET_HEADS * RET_DV
    half = RET_DK // 2

    def rope_store(dst, col0, mult):
        for hd in range(RET_HEADS):
            x = _mm(hn, w_ref[:, col0 + hd * RET_DK:col0 + (hd + 1) * RET_DK])
            x1, x2 = x[:, :half], x[:, half:]
            dst[hd, :, :half] = ((x1 * cos - x2 * sin) * mult).astype(BF16)
            dst[hd, :, half:] = ((x2 * cos + x1 * sin) * mult).astype(BF16)

    rope_store(q_ref, 0, 1.0)
    rope_store(k_ref, hq, RET_DK ** -0.5)
    v = _mm(hn, w_ref[:, 2 * hq:2 * hq + hv]).astype(BF16)
    gate = _mm(hn, w_ref[:, 2 * hq + hv:])
    gate = (gate * jax.nn.sigmoid(gate) * gn_ref[...]).astype(BF16)
    for hd in range(RET_HEADS):
        v_ref[hd] = v[:, hd * RET_DV:(hd + 1) * RET_DV]
        gate_ref[hd] = gate[:, hd * RET_DV:(hd + 1) * RET_DV]


def _ret_proj(h, g, w_in, gn_g, cos, sin, batch, length):
    n, d = h.shape
    nblk = length // ROW_TILE
    hv = RET_HEADS * RET_DV
    row = lambda width: pl.BlockSpec((ROW_TILE, width), lambda i: (i, 0))
    pos = pl.BlockSpec((ROW_TILE, RET_DK // 2), lambda i: (i % nblk, 0))
    heads = lambda dim: pl.BlockSpec((None, RET_HEADS, ROW_TILE, dim), lambda i: (i // nblk, 0, i % nblk, 0))
    shape = lambda dim: jax.ShapeDtypeStruct((batch, RET_HEADS, length, dim), BF16)
    return pl.pallas_call(
        _ret_proj_body,
        grid=(n // ROW_TILE,),
        in_specs=[row(d), _resident((1, d)), _resident_slice(w_in), _resident((1, hv)), pos, pos],
        out_specs=[heads(RET_DK), heads(RET_DK), heads(RET_DV), heads(RET_DV)],
        out_shape=[shape(RET_DK), shape(RET_DK), shape(RET_DV), shape(RET_DV)],
        compiler_params=_params("parallel"),
        name="ret_proj",
    )(h, g, w_in[0], gn_g, cos, sin)


_BWD, _FWD = 0, 1


def _ret_scan_body(lg_ref, q_ref, k_ref, v_ref, gate_ref, o_ref,
                   state_ref, sb_ref, o_scr, dmat_ref, xi_ref, zeta_ref, gc_ref):
    hd = pl.program_id(1)
    direction = pl.program_id(2)
    blk = pl.program_id(3)
    nblk = pl.num_programs(3)
    c = RET_CHUNK
    n_chunks = RET_BLOCK // c

    @pl.when(blk == 0)
    def _reset_state():
        state_ref[...] = jnp.zeros(state_ref.shape, F32)

    @pl.when((blk == 0) & (direction == _BWD))
    def _decay_tables():
        lg_b = lg_ref[_BWD, hd]
        lg_f = lg_ref[_FWD, hd]
        row = lax.broadcasted_iota(jnp.int32, (c, c), 0)
        col = lax.broadcasted_iota(jnp.int32, (c, c), 1)
        dmat_ref[...] = jnp.where(row >= col,
                                  jnp.exp(lg_f * jnp.maximum(row - col, 0).astype(F32)),
                                  jnp.exp(lg_b * jnp.maximum(col - row, 0).astype(F32)))
        qi = lax.broadcasted_iota(jnp.int32, xi_ref.shape[1:], 0)
        xi_ref[_FWD] = jnp.exp(lg_f * (qi + 1).astype(F32))
        xi_ref[_BWD] = jnp.exp(lg_b * (c - qi).astype(F32))
        kj = lax.broadcasted_iota(jnp.int32, zeta_ref.shape[1:], 0)
        zeta_ref[_FWD] = jnp.exp(lg_f * (c - 1 - kj).astype(F32))
        zeta_ref[_BWD] = jnp.exp(lg_b * kj.astype(F32))
        gc_ref[_FWD] = jnp.exp(jnp.zeros(gc_ref.shape[1:], F32) + lg_f * c)
        gc_ref[_BWD] = jnp.exp(jnp.zeros(gc_ref.shape[1:], F32) + lg_b * c)

    def absorb(ci, d):
        rows = pl.ds(ci * c, c)
        kz_t = (k_ref[rows, :].astype(F32) * zeta_ref[d]).T.astype(BF16)
        state_ref[...] = state_ref[...] * gc_ref[d] + _mm(kz_t, v_ref[rows, :])

    @pl.when(direction == _BWD)
    def _backward_states():
        first = (nblk - 1 - blk) * n_chunks
        for ci in reversed(range(n_chunks)):
            sb_ref[first + ci] = state_ref[...].astype(BF16)
            absorb(ci, _BWD)

    @pl.when(direction == _FWD)
    def _forward():
        first = blk * n_chunks
        for ci in range(n_chunks):
            rows = pl.ds(ci * c, c)
            q = q_ref[rows, :]
            s = lax.dot_general(q, k_ref[rows, :], _NT, preferred_element_type=F32) * dmat_ref[...]
            o_scr[rows, :] = (_mm(s.astype(BF16), v_ref[rows, :])
                              + _mm(q, state_ref[...].astype(BF16)) * xi_ref[_FWD]
                              + _mm(q, sb_ref[first + ci]) * xi_ref[_BWD])
            absorb(ci, _FWD)
            for r in range(ci * c, (ci + 1) * c, RET_NORM_ROWS):
                o = o_scr[pl.ds(r, RET_NORM_ROWS), :]
                dev = o - jnp.mean(o, axis=-1, keepdims=True)
                on = dev * lax.rsqrt(jnp.mean(dev * dev, axis=-1, keepdims=True) + EPS)
                gate = gate_ref[pl.ds(r, RET_NORM_ROWS), :].astype(F32)
                o_ref[pl.ds(r, RET_NORM_ROWS), :] = (gate * on).astype(BF16)


def _ret_scan(q, k, v, gate, lg, batch, length):
    hv = RET_HEADS * RET_DV
    assert length % RET_BLOCK == 0
    nblk = length // RET_BLOCK

    def sweep(b, h, d, i):
        return (b, h, d * i + (1 - d) * (nblk - 1 - i), 0)

    def fwd_only(b, h, d, i):
        return (b, h, d * i, 0)

    out = pl.pallas_call(
        _ret_scan_body,
        grid=(batch, RET_HEADS, 2, nblk),
        in_specs=[pl.BlockSpec(memory_space=pltpu.SMEM),
                  pl.BlockSpec((None, None, RET_BLOCK, RET_DK), fwd_only),
                  pl.BlockSpec((None, None, RET_BLOCK, RET_DK), sweep),
                  pl.BlockSpec((None, None, RET_BLOCK, RET_DV), sweep),
                  pl.BlockSpec((None, None, RET_BLOCK, RET_DV), fwd_only)],
        out_specs=pl.BlockSpec((None, RET_BLOCK, RET_DV), lambda b, h, d, i: (b, d * i, h)),
        out_shape=jax.ShapeDtypeStruct((batch, length, hv), BF16),
        scratch_shapes=[pltpu.VMEM((RET_DK, RET_DV), F32),
                        pltpu.VMEM((length // RET_CHUNK, RET_DK, RET_DV), BF16),
                        pltpu.VMEM((RET_BLOCK, RET_DV), F32),
                        pltpu.VMEM((RET_CHUNK, RET_CHUNK), F32),
                        pltpu.VMEM((2, RET_CHUNK, RET_DV), F32),
                        pltpu.VMEM((2, RET_CHUNK, RET_DK), F32),
                        pltpu.VMEM((2, 1, RET_DV), F32)],
        compiler_params=_params("parallel", "parallel", "arbitrary", "arbitrary"),
        name="ret_scan",
    )(lg, q, k, v, gate)
    return out.reshape(batch * length, hv)


def _swap_halves(a):
    half = a.shape[-1] // 2
    return jnp.concatenate([a[..., half:], a[..., :half]], axis=-1)


def _rope_slot_layout(a):
    pe = a[..., QK_NOPE_DIM:]
    return jnp.concatenate([a[..., :QK_NOPE_DIM], pe, _swap_halves(pe)], axis=-1)


def _mla_weights(w_dq, q_a_norm, w_uq, w_dkv, kv_a_norm, w_ukv, q_norm, k_norm):
    uq = w_uq.reshape(Q_LORA_RANK, MLA_HEADS, QK_HEAD_DIM)
    uq = _rope_slot_layout(uq).reshape(Q_LORA_RANK, MLA_HEADS * QK_SLOT)
    pe = w_dkv[:, KV_LORA_RANK:]
    dkv = jnp.concatenate([w_dkv[:, :KV_LORA_RANK], pe, _swap_halves(pe)], axis=-1)
    ukv = w_ukv.reshape(KV_LORA_RANK, MLA_HEADS, QK_NOPE_DIM + V_HEAD_DIM)
    wk = ukv[..., :QK_NOPE_DIM].reshape(KV_LORA_RANK, MLA_HEADS * QK_NOPE_DIM)
    wvt = ukv[..., QK_NOPE_DIM:].reshape(KV_LORA_RANK, MLA_HEADS * V_HEAD_DIM).T
    return {
        "dq": w_dq.astype(BF16), "qa": q_a_norm[None, :], "uqt": uq.T.astype(BF16),
        "dkv": dkv.astype(BF16), "kva": kv_a_norm[None, :],
        "k": wk.astype(BF16), "vt": wvt.astype(BF16),
        "qgt": jnp.broadcast_to(
            jnp.concatenate([q_norm[:QK_NOPE_DIM] * k_norm[:QK_NOPE_DIM],
                             _rope_slot_layout(q_norm)[QK_NOPE_DIM:]])[:, None], (QK_SLOT, ROW_TILE)),
        "kg": _rope_slot_layout(k_norm)[None, :],
    }


def _rope_tables(length, dim):
    inv = ROPE_BASE ** (-jnp.arange(0, dim, 2, dtype=F32) / dim)
    ang = jnp.arange(length, dtype=F32)[:, None] * inv[None, :]
    return jnp.cos(ang), jnp.sin(ang)


def _trunk(x, p):
    batch, length, d = x.shape
    assert d == D_MODEL and length % max(ROW_TILE, FFN_FIRST_ROW_TILE, FFN_SECOND_ROW_TILE) == 0
    x = x.reshape(batch * length, d)
    cos_m, sin_m = _rope_tables(length, QK_ROPE_DIM)
    cs_mla = jnp.concatenate([cos_m, cos_m, -sin_m, sin_m], axis=-1)
    cos_r, sin_r = _rope_tables(length, RET_DK)
    for i in range(DEPTH):
        j = i // 2
        ng = p["norm_g"][i]
        f = p["ffn"][i]
        h = _ffn_first(x, ng[0:1], f[0])
        if i % 2 == 0:
            w = p["mla"][j]
            qt, k, vt = _mla_proj(h, ng[1:2], w, cs_mla, batch, length)
            o = _attention(qt, k, vt, batch, length)
        else:
            w = p["ret"][j]
            q, k, v, gate = _ret_proj(h, ng[1:2], w["in"], w["gn"], cos_r, sin_r, batch, length)
            o = _ret_scan(q, k, v, gate, w["lg"], batch, length)
        x = _ffn_second(h, o, w["o"], ng[2:3], f[1], ng[3:4])
    return x.reshape(batch, length, d)


def kernel(x_prompt, x_sample, norm_g, ffn_w_gate, ffn_w_up, ffn_w_down, mla_w_dq, mla_q_a_norm, mla_w_uq, mla_w_dkv, mla_kv_a_norm, mla_w_ukv, mla_q_norm, mla_k_norm, mla_w_o, ret_w_in, ret_decay_fwd, ret_decay_bwd, ret_gn_g, ret_w_o):
    wg, wu, wd = ffn_w_gate.astype(BF16), ffn_w_up.astype(BF16), ffn_w_down.astype(BF16)
    mla_o, ret_in, ret_o = mla_w_o.astype(BF16), ret_w_in.astype(BF16), ret_w_o.astype(BF16)
    p = {
        "norm_g": norm_g,
        "ffn": [[{"g": (wg, (i, s)), "u": (wu, (i, s)), "d": (wd, (i, s))} for s in range(2)]
                for i in range(DEPTH)],
        "mla": [], "ret": [],
    }
    for j in range(mla_w_dq.shape[0]):
        w = _mla_weights(mla_w_dq[j], mla_q_a_norm[j], mla_w_uq[j], mla_w_dkv[j], mla_kv_a_norm[j],
                         mla_w_ukv[j], mla_q_norm[j], mla_k_norm[j])
        w["o"] = (mla_o, (j,))
        p["mla"].append(w)
    for j in range(ret_w_in.shape[0]):
        lg = jnp.stack([jnp.log1p(-jnp.exp(ret_decay_bwd[j].astype(F32))),
                        jnp.log1p(-jnp.exp(ret_decay_fwd[j].astype(F32)))])
        p["ret"].append({"in": (ret_in, (j,)), "lg": lg, "gn": ret_gn_g[j][None, :], "o": (ret_o, (j,))})
    return _trunk(x_prompt, p), _trunk(x_sample, p)
```
